```python
import math
import jax, jax.numpy as jnp
from jax import lax
import numpy as np

D_MODEL = 1024
BATCH = 2
SEQ = 8192
DEPTH = 1

CHUNK = 64
Q_BLOCK = 128
RET_HEADS = 4
RET_DK = 128
RET_DV = 256
RET_QK = RET_HEADS * RET_DK
RET_V = RET_HEADS * RET_DV
FOX_HEADS = 8
FOX_DH = 128
FOX_W = FOX_HEADS * FOX_DH
D_FF = (((8 * D_MODEL + 2) // 3 + 255) // 256) * 256
ROPE_THETA = 10000.0
EPS = 1e-6
SPLIT = (RET_QK, RET_QK, RET_V, RET_V, FOX_W, FOX_W, FOX_W, FOX_HEADS, D_MODEL, D_MODEL)
D_IN = 2 * RET_QK + 2 * RET_V + 3 * FOX_W + FOX_HEADS + 2 * D_MODEL

kernel_name = "hybrid_retention_fox_gated_block"


def rms_norm(x, w):
    xf = x.astype(jnp.float32)
    y = xf * lax.rsqrt(jnp.mean(xf * xf, axis=-1, keepdims=True) + EPS)
    return (y * w.astype(jnp.float32)).astype(x.dtype)


def modulate(h, shift, scale):
    return h * (1.0 + scale[:, None, :]) + shift[:, None, :]


def rotary(x, pos):
    half = x.shape[-1] // 2
    freqs = ROPE_THETA ** (-jnp.arange(half, dtype=jnp.float32) / half)
    ang = pos.astype(jnp.float32)[:, None] * freqs[None, :]
    cos = jnp.cos(ang)[None, :, None, :]
    sin = jnp.sin(ang)[None, :, None, :]
    xf = x.astype(jnp.float32)
    x1, x2 = xf[..., :half], xf[..., half:]
    return jnp.concatenate([x1 * cos - x2 * sin, x1 * sin + x2 * cos], axis=-1).astype(x.dtype)


def retention(q, k, v):
    B, S, H, dk = q.shape
    dv = v.shape[-1]
    N = S // CHUNK
    log_gamma = jnp.log(1.0 - jnp.exp2(-5.0 - jnp.arange(H, dtype=jnp.float32)))
    qc = q.astype(jnp.float32).reshape(B, N, CHUNK, H, dk)
    kc = k.astype(jnp.float32).reshape(B, N, CHUNK, H, dk)
    vc = v.astype(jnp.float32).reshape(B, N, CHUNK, H, dv)
    idx = jnp.arange(CHUNK, dtype=jnp.float32)
    d_intra = jnp.exp(log_gamma[:, None, None] * jnp.abs(idx[:, None] - idx[None, :]))
    scores = jnp.einsum('bnihd,bnjhd->bnhij', qc, kc) * d_intra
    intra = jnp.einsum('bnhij,bnjhe->bnihe', scores, vc)
    k_dec = jnp.exp(log_gamma[None, :] * (CHUNK - idx)[:, None])
    summ = jnp.einsum('bnjhd,bnjhe->bnhde', kc * k_dec[None, None, :, :, None], vc)
    chunk_decay = jnp.exp(log_gamma * CHUNK)[None, :, None, None]

    def step(state, a):
        return chunk_decay * state + a, state

    _, s_in = lax.scan(step, jnp.zeros((B, H, dk, dv), jnp.float32), jnp.moveaxis(summ, 1, 0))
    s_in = jnp.moveaxis(s_in, 0, 1)
    q_dec = jnp.exp(log_gamma[None, :] * idx[:, None])
    inter = jnp.einsum('bnihd,bnhde->bnihe', qc * q_dec[None, None, :, :, None], s_in)
    return (intra + inter).reshape(B, S, H, dv)


def forgetting_attention(q, k, v, log_f):
    B, S, H, d = q.shape
    nq = S // Q_BLOCK
    scale = 1.0 / math.sqrt(d)
    F = jnp.cumsum(log_f, axis=1).transpose(0, 2, 1)
    kt = k.transpose(0, 2, 1, 3)
    vt = v.transpose(0, 2, 1, 3)
    qb = jnp.moveaxis(q.transpose(0, 2, 1, 3).reshape(B, H, nq, Q_BLOCK, d), 2, 0)
    Fqb = jnp.moveaxis(F.reshape(B, H, nq, Q_BLOCK), 2, 0)
    kpos = jnp.arange(S)
    posb = kpos.reshape(nq, Q_BLOCK)

    def block(args):
        qi, fi, pi = args
        s = jnp.einsum('bhqd,bhkd->bhqk', qi, kt).astype(jnp.float32) * scale
        s = s + fi[..., :, None] - F[:, :, None, :]
        s = jnp.where((pi[:, None] >= kpos[None, :])[None, None], s, -jnp.inf)
        p = jax.nn.softmax(s, axis=-1)
        return jnp.einsum('bhqk,bhkd->bhqd', p.astype(vt.dtype), vt)

    out = lax.map(block, (qb, Fqb, posb))
    out = jnp.moveaxis(out, 0, 2).reshape(B, H, S, d).transpose(0, 2, 1, 3)
    return out.reshape(B, S, H * d)


def setup_inputs(seed: int = 0) -> dict:
    key = jax.random.key(seed)
    ks = jax.random.split(key, 16)
    f32 = jnp.float32
    L, D = DEPTH, D_MODEL
    nrm = lambda k, shape, fan: jax.random.normal(k, shape, f32) * (fan ** -0.5)
    return {
        "x": jax.random.normal(ks[0], (BATCH, SEQ, D), f32),
        "c": jax.random.normal(ks[1], (BATCH, D), f32),
        "ada_w": nrm(ks[2], (L, D, 6 * D), D) * 0.5,
        "ada_b": 0.01 * jax.random.normal(ks[3], (L, 6 * D), f32),
        "norm1_w": 1.0 + 0.01 * jax.random.normal(ks[4], (L, D), f32),
        "w_in": nrm(ks[5], (L, D, D_IN), D),
        "b_f": 1.0 + 3.0 * jax.random.uniform(ks[6], (L, FOX_HEADS), f32),
        "ret_proj": nrm(ks[7], (L, RET_V, D), RET_V),
        "fox_proj": nrm(ks[8], (L, FOX_W, D), FOX_W),
        "w_out": nrm(ks[9], (L, D, D), D),
        "norm2_w": 1.0 + 0.01 * jax.random.normal(ks[10], (L, D), f32),
        "w_gate": nrm(ks[11], (L, D, D_FF), D),
        "w_up": nrm(ks[12], (L, D, D_FF), D),
        "w_down": nrm(ks[13], (L, D_FF, D), D_FF),
        "norm_f_w": 1.0 + 0.01 * jax.random.normal(ks[14], (D,), f32),
    }


def reference(x, c, ada_w, ada_b, norm1_w, w_in, b_f, ret_proj, fox_proj, w_out,
              norm2_w, w_gate, w_up, w_down, norm_f_w):
    B, S, D = x.shape
    pos = jnp.arange(S)
    offs = [int(o) for o in np.cumsum(SPLIT)[:-1]]
    c_act = jax.nn.silu(c)
    for l in range(DEPTH):
        mod = c_act @ ada_w[l] + ada_b[l]
        sh1, sc1, g1, sh2, sc2, g2 = jnp.split(mod, 6, axis=-1)

        h = modulate(rms_norm(x, norm1_w[l]), sh1, sc1)
        z = h @ w_in[l]
        rq, rk, rv, rg, fq, fk, fv, ff, gr, gf = jnp.split(z, offs, axis=-1)

        rq = rotary(rq.reshape(B, S, RET_HEADS, RET_DK), pos)
        rk = rotary(rk.reshape(B, S, RET_HEADS, RET_DK), pos) * (RET_DK ** -0.5)
        ro = retention(rq, rk, rv.reshape(B, S, RET_HEADS, RET_DV))
        ro = ro * lax.rsqrt(jnp.mean(ro * ro, axis=-1, keepdims=True) + EPS)
        y_ret = (jax.nn.silu(rg.astype(jnp.float32)) * ro.reshape(B, S, RET_V)).astype(x.dtype)

        log_f = jax.nn.log_sigmoid((ff + b_f[l]).astype(jnp.float32))
        y_fox = forgetting_attention(fq.reshape(B, S, FOX_HEADS, FOX_DH),
                                     fk.reshape(B, S, FOX_HEADS, FOX_DH),
                                     fv.reshape(B, S, FOX_HEADS, FOX_DH), log_f)

        merged = jax.nn.sigmoid(gr) * (y_ret @ ret_proj[l]) + jax.nn.sigmoid(gf) * (y_fox @ fox_proj[l])
        x = x + g1[:, None, :] * (merged @ w_out[l])

        h2 = modulate(rms_norm(x, norm2_w[l]), sh2, sc2)
        ffn = (jax.nn.silu(h2 @ w_gate[l]) * (h2 @ w_up[l])) @ w_down[l]
        x = x + g2[:, None, :] * ffn
    return rms_norm(x, norm_f_w)
```

```python
import functools
import math

import numpy as np
import jax
import jax.numpy as jnp
from jax import lax
from jax.experimental import pallas as pl
from jax.experimental.pallas import tpu as pltpu

F32 = jnp.float32
BF16 = jnp.bfloat16

EPS = 1e-6
ROPE_THETA = 10000.0
RET_HEADS = 4
RET_DK = 128
RET_DV = 256
FOX_HEADS = 8
FOX_DH = 128
RET_CHUNK = 64
LANES = 128
AUG_STRIDE = 16

NEG_BIG = -1e30

_NT = (((1,), (1,)), ((), ()))
_TN = (((0,), (0,)), ((), ()))


def _cparams(sem, vmem_mb):
    return pltpu.CompilerParams(dimension_semantics=sem,
                                vmem_limit_bytes=vmem_mb * 1024 * 1024)


def _ada_kernel(ct_ref, w_ref, b_ref, o_ref):
    ct = ct_ref[...]
    act = ct * jax.nn.sigmoid(ct)
    w = w_ref[...]
    for b in range(o_ref.shape[0]):
        o_ref[b:b + 1, :] = (jnp.sum(act[:, b:b + 1] * w, axis=0, keepdims=True)
                             + b_ref[...])


def _ada(c, w, b):
    B, D = c.shape
    N = w.shape[1]
    tn = 1536
    return pl.pallas_call(
        _ada_kernel,
        out_shape=jax.ShapeDtypeStruct((B, N), F32),
        grid=(N // tn,),
        in_specs=[pl.BlockSpec((D, B), lambda j: (0, 0)),
                  pl.BlockSpec((D, tn), lambda j: (0, j)),
                  pl.BlockSpec((1, tn), lambda j: (0, j))],
        out_specs=pl.BlockSpec((B, tn), lambda j: (0, j)),
        compiler_params=_cparams(("arbitrary",), 40),
        name="ada",
    )(c.T, w, b.reshape(1, N))


def _inproj_kernel(x_ref, nw_ref, sc_ref, sh_ref, cos_ref, sin_ref, w_ref, wff_ref,
                   z_ref, ff_ref, h_scr, *, fox_scale, rk_scale):
    j = pl.program_id(1)

    @pl.when(j == 0)
    def _():
        x = x_ref[...]
        ms = jnp.mean(x * x, axis=-1, keepdims=True)
        y = x * lax.rsqrt(ms + EPS) * nw_ref[...]
        h = (y * (1.0 + sc_ref[0]) + sh_ref[0]).astype(BF16)
        h_scr[...] = h
        ff_ref[...] = jnp.dot(h, wff_ref[...], preferred_element_type=F32)

    acc = jnp.dot(h_scr[...], w_ref[...], preferred_element_type=F32)

    def rotary(scale):
        cos = cos_ref[...]
        sin = sin_ref[...]
        for hh in range(acc.shape[1] // RET_DK):
            a = acc[:, hh * RET_DK:(hh + 1) * RET_DK]
            o = a * cos + pltpu.roll(a, RET_DK // 2, axis=1) * sin
            if scale is not None:
                o = o * scale
            z_ref[:, hh * RET_DK:(hh + 1) * RET_DK] = o.astype(BF16)

    @pl.when(j == 0)
    def _():
        rotary(None)

    @pl.when(j == 1)
    def _():
        rotary(rk_scale)

    @pl.when(((j >= 2) & (j < 4)) | ((j >= 8) & (j < 12)))
    def _():
        z_ref[...] = acc.astype(BF16)

    @pl.when((j >= 4) & (j < 6))
    def _():
        z_ref[...] = (acc * jax.nn.sigmoid(acc)).astype(BF16)

    @pl.when((j >= 6) & (j < 8))
    def _():
        z_ref[...] = (acc * fox_scale).astype(BF16)

    @pl.when(j >= 12)
    def _():
        z_ref[...] = jax.nn.sigmoid(acc).astype(BF16)


def _inproj(x2, nw, sc, sh, cos_t, sin_t, w, wff, S):
    T, D = x2.shape
    N = w.shape[1]
    tm, tn = 1024, 512
    spb = S // tm
    kern = functools.partial(_inproj_kernel, fox_scale=1.0 / math.sqrt(FOX_DH),
                             rk_scale=RET_DK ** -0.5)
    return pl.pallas_call(
        kern,
        out_shape=(jax.ShapeDtypeStruct((T, N), BF16),
                   jax.ShapeDtypeStruct((T, LANES), F32)),
        grid=(T // tm, N // tn),
        in_specs=[pl.BlockSpec((tm, D), lambda i, j: (i, 0)),
                  pl.BlockSpec((1, D), lambda i, j: (0, 0)),
                  pl.BlockSpec((1, 1, D), lambda i, j: (i // spb, 0, 0)),
                  pl.BlockSpec((1, 1, D), lambda i, j: (i // spb, 0, 0)),
                  pl.BlockSpec((tm, LANES), lambda i, j: (i % spb, 0)),
                  pl.BlockSpec((tm, LANES), lambda i, j: (i % spb, 0)),
                  pl.BlockSpec((D, tn), lambda i, j: (0, j)),
                  pl.BlockSpec((D, LANES), lambda i, j: (0, 0))],
        out_specs=(pl.BlockSpec((tm, tn), lambda i, j: (i, j)),
                   pl.BlockSpec((tm, LANES), lambda i, j: (i, 0))),
        scratch_shapes=[pltpu.VMEM((tm, D), BF16)],
        compiler_params=_cparams(("arbitrary", "arbitrary"), 48),
        name="inproj",
    )(x2, nw, sc, sh, cos_t, sin_t, w, wff)


def _split3(v):
    p0 = v.astype(BF16)
    r1 = v - p0.astype(F32)
    p1 = r1.astype(BF16)
    r2 = r1 - p1.astype(F32)
    p2 = r2.astype(BF16)
    return p0, p1, p2


def _foxprep_kernel(ff_ref, bf_ref, tri_ref, qa_ref, ka_ref, carry):
    @pl.when(pl.program_id(1) == 0)
    def _():
        carry[...] = jnp.zeros_like(carry)

    xv = ff_ref[0] + bf_ref[...]
    lf = jnp.minimum(xv, 0.0) - jnp.log(1.0 + jnp.exp(-jnp.abs(xv)))
    tri = tri_ref[...]
    p0, p1, p2 = _split3(lf)
    cs = (jnp.dot(tri, p0, preferred_element_type=F32)
          + jnp.dot(tri, p1, preferred_element_type=F32)
          + jnp.dot(tri, p2, preferred_element_type=F32))
    fc = cs + carry[...]
    tb = fc.shape[0]
    carry[...] = fc[tb - 1:tb, :]
    f0, f1, f2 = [p.astype(F32) for p in _split3(fc)]
    c = lax.broadcasted_iota(jnp.int32, fc.shape, 1) % AUG_STRIDE
    qa_ref[0] = jnp.where(c == 0, f0, jnp.where(c == 1, f1, jnp.where(
        c == 2, f2, jnp.where(c < 6, 1.0, 0.0)))).astype(BF16)
    ka_ref[0] = jnp.where(c < 3, 1.0, jnp.where(c == 3, -f0, jnp.where(
        c == 4, -f1, jnp.where(c == 5, -f2, 0.0)))).astype(BF16)


def _foxprep(ff3, bf_l):
    B, S, _ = ff3.shape
    tb = 256
    tri = jnp.asarray(np.tril(np.ones((tb, tb), np.float32)), BF16)
    return pl.pallas_call(
        _foxprep_kernel,
        out_shape=(jax.ShapeDtypeStruct((B, S, LANES), BF16),
                   jax.ShapeDtypeStruct((B, S, LANES), BF16)),
        grid=(B, S // tb),
        in_specs=[pl.BlockSpec((1, tb, LANES), lambda b, i: (b, i, 0)),
                  pl.BlockSpec((1, LANES), lambda b, i: (0, 0)),
                  pl.BlockSpec((tb, tb), lambda b, i: (0, 0))],
        out_specs=(pl.BlockSpec((1, tb, LANES), lambda b, i: (b, i, 0)),
                   pl.BlockSpec((1, tb, LANES), lambda b, i: (b, i, 0))),
        scratch_shapes=[pltpu.VMEM((1, LANES), F32)],
        compiler_params=_cparams(("arbitrary", "arbitrary"), 32),
        name="foxprep",
    )(ff3, bf_l, tri)


def _ret_consts(L):
    hs = np.arange(RET_HEADS, dtype=np.float64)
    log_gamma = np.log(1.0 - np.exp2(-5.0 - hs))
    idx = np.arange(L, dtype=np.float64)
    dist = np.abs(idx[:, None] - idx[None, :])
    chunk_ok = (idx[None, :] // RET_CHUNK) <= (idx[:, None] // RET_CHUNK)
    dmask = np.exp(log_gamma[:, None, None] * dist[None]) * chunk_ok[None]
    qdec = np.exp(log_gamma[:, None] * idx[None, :])[..., None]
    kdec = np.exp(log_gamma[:, None] * (L - idx)[None, :])[..., None]
    bdec = [float(v) for v in np.exp(log_gamma * L)]
    return (jnp.asarray(dmask, F32), jnp.asarray(qdec, F32), jnp.asarray(kdec, F32), bdec)


def _ret_kernel(rq_ref, rk_ref, rv_ref, rg_ref, gr_ref, dm_ref, qd_ref, kd_ref, wr_ref,
                o_ref, st_scr, y_scr, *, bdec):
    @pl.when(pl.program_id(1) == 0)
    def _():
        st_scr[...] = jnp.zeros_like(st_scr)

    for h in range(RET_HEADS):
        q = rq_ref[0, :, h * RET_DK:(h + 1) * RET_DK]
        k = rk_ref[0, :, h * RET_DK:(h + 1) * RET_DK]
        v = rv_ref[0, :, h * RET_DV:(h + 1) * RET_DV]
        s = lax.dot_general(q, k, _NT, preferred_element_type=F32) * dm_ref[h]
        intra = jnp.dot(s.astype(BF16), v, preferred_element_type=F32)
        st = st_scr[h]
        inter = jnp.dot(q, st.astype(BF16), preferred_element_type=F32) * qd_ref[h]
        ks = (k.astype(F32) * kd_ref[h]).astype(BF16)
        st_scr[h] = bdec[h] * st + lax.dot_general(ks, v, _TN, preferred_element_type=F32)
        ro = intra + inter
        ron = ro * lax.rsqrt(jnp.mean(ro * ro, axis=-1, keepdims=True) + EPS)
        g = rg_ref[0, :, h * RET_DV:(h + 1) * RET_DV].astype(F32)
        y_scr[:, h * RET_DV:(h + 1) * RET_DV] = (g * ron).astype(BF16)

    proj = jnp.dot(y_scr[...], wr_ref[...], preferred_element_type=F32)
    o_ref[0] = (gr_ref[0].astype(F32) * proj).astype(BF16)


def _retention(z3, wr, D):
    B, S, _ = z3.shape
    L = 256
    dmask, qdec, kdec, bdec = _ret_consts(L)
    QK = RET_HEADS * RET_DK
    V = RET_HEADS * RET_DV
    kern = functools.partial(_ret_kernel, bdec=bdec)
    full3 = lambda b, i: (0, 0, 0)
    return pl.pallas_call(
        kern,
        out_shape=jax.ShapeDtypeStruct((B, S, D), BF16),
        grid=(B, S // L),
        in_specs=[pl.BlockSpec((1, L, QK), lambda b, i: (b, i, 0)),
                  pl.BlockSpec((1, L, QK), lambda b, i: (b, i, 1)),
                  pl.BlockSpec((1, L, V), lambda b, i: (b, i, 1)),
                  pl.BlockSpec((1, L, V), lambda b, i: (b, i, 2)),
                  pl.BlockSpec((1, L, D), lambda b, i: (b, i, 6)),
                  pl.BlockSpec((RET_HEADS, L, L), full3),
                  pl.BlockSpec((RET_HEADS, L, 1), full3),
                  pl.BlockSpec((RET_HEADS, L, 1), full3),
                  pl.BlockSpec((V, D), lambda b, i: (0, 0))],
        out_specs=pl.BlockSpec((1, L, D), lambda b, i: (b, i, 0)),
        scratch_shapes=[pltpu.VMEM((RET_HEADS, RET_DK, RET_DV), F32),
                        pltpu.VMEM((L, V), BF16)],
        compiler_params=_cparams(("arbitrary", "arbitrary"), 40),
        name="ret",
    )(z3, z3, z3, z3, z3, dmask, qdec, kdec, wr)


def _fox_kernel(q_ref, qa_ref, k_ref, ka_ref, v_ref, o_ref,
                qx_scr, m_scr, l_scr, acc_scr, *, tq):
    h = pl.program_id(1)
    i = pl.program_id(2)
    lane = lax.broadcasted_iota(jnp.int32, (tq, LANES), 1)
    qx_scr[:, :FOX_DH] = q_ref[0]
    qa = qa_ref[0].astype(F32)
    qx_scr[:, FOX_DH:] = jnp.where((lane // AUG_STRIDE) == h, qa, 0.0).astype(BF16)
    m_scr[...] = jnp.full(m_scr.shape, NEG_BIG, F32)
    l_scr[...] = jnp.zeros_like(l_scr)
    acc_scr[...] = jnp.zeros_like(acc_scr)

    def step(j, masked):
        off = pl.multiple_of(j * tq, tq)
        kx = jnp.concatenate([k_ref[0, pl.ds(off, tq), :], ka_ref[0, pl.ds(off, tq), :]],
                             axis=1)
        s = lax.dot_general(qx_scr[...], kx, _NT, preferred_element_type=F32)
        if masked:
            row = lax.broadcasted_iota(jnp.int32, s.shape, 0)
            col = lax.broadcasted_iota(jnp.int32, s.shape, 1)
            s = jnp.where(row >= col, s, NEG_BIG)
        m_prev = m_scr[...]
        m_new = jnp.maximum(m_prev, jnp.max(s, axis=-1, keepdims=True))
        alpha = jnp.exp(m_prev - m_new)
        p = jnp.exp(s - m_new)
        l_scr[...] = alpha * l_scr[...] + jnp.sum(p, axis=-1, keepdims=True)
        acc_scr[...] = alpha * acc_scr[...] + jnp.dot(
            p.astype(BF16), v_ref[0, pl.ds(off, tq), :], preferred_element_type=F32)
        m_scr[...] = m_new

    def body(j, carry):
        step(j, False)
        return carry

    lax.fori_loop(0, i, body, 0)
    step(i, True)
    o_ref[0] = (acc_scr[...] / l_scr[...]).astype(BF16)


def _fox(z3, qa, ka):
    B, S, _ = z3.shape
    tq = 512
    W = FOX_HEADS * FOX_DH
    qb, kb, vb = 3 * W // FOX_DH, 4 * W // FOX_DH, 5 * W // FOX_DH
    kern = functools.partial(_fox_kernel, tq=tq)
    return pl.pallas_call(
        kern,
        out_shape=jax.ShapeDtypeStruct((B, S, W), BF16),
        grid=(B, FOX_HEADS, S // tq),
        in_specs=[pl.BlockSpec((1, tq, FOX_DH), lambda b, h, i: (b, i, qb + h)),
                  pl.BlockSpec((1, tq, LANES), lambda b, h, i: (b, i, 0)),
                  pl.BlockSpec((1, S, FOX_DH), lambda b, h, i: (b, 0, kb + h)),
                  pl.BlockSpec((1, S, LANES), lambda b, h, i: (b, 0, 0)),
                  pl.BlockSpec((1, S, FOX_DH), lambda b, h, i: (b, 0, vb + h))],
        out_specs=pl.BlockSpec((1, tq, FOX_DH), lambda b, h, i: (b, i, h)),
        scratch_shapes=[pltpu.VMEM((tq, 2 * FOX_DH), BF16),
                        pltpu.VMEM((tq, 1), F32),
                        pltpu.VMEM((tq, 1), F32),
                        pltpu.VMEM((tq, FOX_DH), F32)],
        compiler_params=_cparams(("arbitrary", "arbitrary", "arbitrary"), 48),
        name="fox",
    )(z3, qa, z3, ka, z3)


def _merge_kernel(a_ref, yf_ref, gf_ref, x_ref, g1_ref, wf_ref, wo_ref, o_ref):
    fo = jnp.dot(yf_ref[...], wf_ref[...], preferred_element_type=F32)
    merged = a_ref[...].astype(F32) + gf_ref[...].astype(F32) * fo
    out = jnp.dot(merged.astype(BF16), wo_ref[...], preferred_element_type=F32)
    o_ref[...] = x_ref[...] + g1_ref[0] * out


def _merge(a2, yf2, z2, x2, g1, wf, wo, S):
    T, D = x2.shape
    tm = 512
    spb = S // tm
    row = lambda i: (i, 0)
    return pl.pallas_call(
        _merge_kernel,
        out_shape=jax.ShapeDtypeStruct((T, D), F32),
        grid=(T // tm,),
        in_specs=[pl.BlockSpec((tm, D), row),
                  pl.BlockSpec((tm, D), row),
                  pl.BlockSpec((tm, D), lambda i: (i, 7)),
                  pl.BlockSpec((tm, D), row),
                  pl.BlockSpec((1, 1, D), lambda i: (i // spb, 0, 0)),
                  pl.BlockSpec((D, D), lambda i: (0, 0)),
                  pl.BlockSpec((D, D), lambda i: (0, 0))],
        out_specs=pl.BlockSpec((tm, D), row),
        compiler_params=_cparams(("arbitrary",), 40),
        name="merge",
    )(a2, yf2, z2, x2, g1, wf, wo)


def _ffn_kernel(x_ref, nw_ref, sc_ref, sh_ref, g2_ref, wg_ref, wu_ref, wd_ref, nf_ref,
                o_ref, h_scr, acc_scr):
    j = pl.program_id(1)

    @pl.when(j == 0)
    def _():
        x = x_ref[...]
        ms = jnp.mean(x * x, axis=-1, keepdims=True)
        y = x * lax.rsqrt(ms + EPS) * nw_ref[...]
        h_scr[...] = (y * (1.0 + sc_ref[0]) + sh_ref[0]).astype(BF16)
        acc_scr[...] = jnp.zeros_like(acc_scr)

    h = h_scr[...]
    g = jnp.dot(h, wg_ref[...], preferred_element_type=F32)
    u = jnp.dot(h, wu_ref[...], preferred_element_type=F32)
    a = (g * jax.nn.sigmoid(g) * u).astype(BF16)
    acc_scr[...] += jnp.dot(a, wd_ref[...], preferred_element_type=F32)

    @pl.when(j == pl.num_programs(1) - 1)
    def _():
        x2 = x_ref[...] + g2_ref[0] * acc_scr[...]
        ms = jnp.mean(x2 * x2, axis=-1, keepdims=True)
        o_ref[...] = x2 * lax.rsqrt(ms + EPS) * nf_ref[...]


def _ffn(x1, nw, sc, sh, g2, wg, wu, wd, nf, S):
    T, D = x1.shape
    FF = wg.shape[1]
    tm, tf = 512, FF // 2
    spb = S // tm
    bidx = lambda i, j: (i // spb, 0, 0)
    return pl.pallas_call(
        _ffn_kernel,
        out_shape=jax.ShapeDtypeStruct((T, D), F32),
        grid=(T // tm, FF // tf),
        in_specs=[pl.BlockSpec((tm, D), lambda i, j: (i, 0)),
                  pl.BlockSpec((1, D), lambda i, j: (0, 0)),
                  pl.BlockSpec((1, 1, D), bidx),
                  pl.BlockSpec((1, 1, D), bidx),
                  pl.BlockSpec((1, 1, D), bidx),
                  pl.BlockSpec((D, tf), lambda i, j: (0, j)),
                  pl.BlockSpec((D, tf), lambda i, j: (0, j)),
                  pl.BlockSpec((tf, D), lambda i, j: (j, 0)),
                  pl.BlockSpec((1, D), lambda i, j: (0, 0))],
        out_specs=pl.BlockSpec((tm, D), lambda i, j: (i, 0)),
        scratch_shapes=[pltpu.VMEM((tm, D), BF16), pltpu.VMEM((tm, D), F32)],
        compiler_params=_cparams(("arbitrary", "arbitrary"), 56),
        name="ffn",
    )(x1, nw, sc, sh, g2, wg, wu, wd, nf)


def _rope_tables(S):
    half = RET_DK // 2
    freqs = ROPE_THETA ** (-jnp.arange(half, dtype=F32) / half)
    ang = jnp.arange(S, dtype=F32)[:, None] * freqs[None, :]
    cos, sin = jnp.cos(ang), jnp.sin(ang)
    return (jnp.concatenate([cos, cos], axis=1), jnp.concatenate([-sin, sin], axis=1))


def _aug_lanes(v8):
    rep = jnp.repeat(v8[..., None], 6, axis=-1)
    pad = jnp.zeros(v8.shape + (AUG_STRIDE - 6,), v8.dtype)
    return jnp.concatenate([rep, pad], axis=-1).reshape(v8.shape[:-1] + (LANES,))


def _layer(x, mod, norm1_w, w_in, b_f, ret_proj, fox_proj, w_out, norm2_w,
           w_gate, w_up, w_down, norm_out_w, tables):
    B, S, D = x.shape
    T = B * S
    sh1, sc1, g1, sh2, sc2, g2 = [m.reshape(B, 1, D) for m in jnp.split(mod, 6, axis=-1)]
    n_main = w_in.shape[1] - FOX_HEADS - 2 * D
    w_main = jnp.concatenate([w_in[:, :n_main], w_in[:, n_main + FOX_HEADS:]],
                             axis=1).astype(BF16)
    w_ff = _aug_lanes(w_in[:, n_main:n_main + FOX_HEADS]).astype(BF16)
    bf_l = _aug_lanes(b_f).reshape(1, LANES)

    x2 = x.reshape(T, D)
    z, ff = _inproj(x2, norm1_w.reshape(1, D), sc1, sh1, tables[0], tables[1],
                    w_main, w_ff, S)
    z3 = z.reshape(B, S, z.shape[1])
    qa, ka = _foxprep(ff.reshape(B, S, LANES), bf_l)
    a = _retention(z3, ret_proj.astype(BF16), D)
    yf = _fox(z3, qa, ka)
    x1 = _merge(a.reshape(T, D), yf.reshape(T, D), z, x2, g1,
                fox_proj.astype(BF16), w_out.astype(BF16), S)
    out = _ffn(x1, norm2_w.reshape(1, D), sc2, sh2, g2, w_gate.astype(BF16),
               w_up.astype(BF16), w_down.astype(BF16), norm_out_w.reshape(1, D), S)
    return out.reshape(B, S, D)


def kernel(x, c, ada_w, ada_b, norm1_w, w_in, b_f, ret_proj, fox_proj, w_out,
           norm2_w, w_gate, w_up, w_down, norm_f_w):
    depth = ada_w.shape[0]
    assert depth == 1, "the final RMSNorm is fused into the last layer's channel mixer"
    tables = _rope_tables(x.shape[1])
    l = 0
    mod = _ada(c, ada_w[l], ada_b[l])
    return _layer(x, mod, norm1_w[l], w_in[l], b_f[l], ret_proj[l], fox_proj[l], w_out[l],
                  norm2_w[l], w_gate[l], w_up[l], w_down[l], norm_f_w, tables)
```

```python
import functools
import math

import numpy as np
import jax
import jax.numpy as jnp
from jax import lax
from jax.experimental import pallas as pl
from jax.experimental.pallas import tpu as pltpu

F32 = jnp.float32
BF16 = jnp.bfloat16

EPS = 1e-6
ROPE_THETA = 10000.0
RET_HEADS = 4
RET_DK = 128
RET_DV = 256
FOX_HEADS = 8
FOX_DH = 128
RET_CHUNK = 64
LANES = 128
AUG_STRIDE = 16

NEG_BIG = -1e30
LOG2E = math.log2(math.e)

_NT = (((1,), (1,)), ((), ()))
_TN = (((0,), (0,)), ((), ()))


def _cparams(sem, vmem_mb):
    return pltpu.CompilerParams(dimension_semantics=sem,
                                vmem_limit_bytes=vmem_mb * 1024 * 1024)


def _ada_kernel(ct_ref, w_ref, b_ref, o_ref):
    ct = ct_ref[...]
    act = ct * jax.nn.sigmoid(ct)
    w = w_ref[...]
    for b in range(o_ref.shape[0]):
        o_ref[b:b + 1, :] = (jnp.sum(act[:, b:b + 1] * w, axis=0, keepdims=True)
                             + b_ref[...])


def _ada(c, w, b):
    B, D = c.shape
    N = w.shape[1]
    tn = 1536
    return pl.pallas_call(
        _ada_kernel,
        out_shape=jax.ShapeDtypeStruct((B, N), F32),
        grid=(N // tn,),
        in_specs=[pl.BlockSpec((D, B), lambda j: (0, 0)),
                  pl.BlockSpec((D, tn), lambda j: (0, j)),
                  pl.BlockSpec((1, tn), lambda j: (0, j))],
        out_specs=pl.BlockSpec((B, tn), lambda j: (0, j)),
        compiler_params=_cparams(("arbitrary",), 40),
        name="ada",
    )(c.T, w, b.reshape(1, N))


def _inproj_kernel(x_ref, nw_ref, sc_ref, sh_ref, cos_ref, sin_ref, w_ref, wff_ref,
                   z_ref, ff_ref, h_scr, *, fox_scale, rk_scale):
    j = pl.program_id(1)

    @pl.when(j == 0)
    def _():
        x = x_ref[...]
        ms = jnp.mean(x * x, axis=-1, keepdims=True)
        y = x * lax.rsqrt(ms + EPS) * nw_ref[...]
        h = (y * (1.0 + sc_ref[0]) + sh_ref[0]).astype(BF16)
        h_scr[...] = h
        ff_ref[...] = jnp.dot(h, wff_ref[...], preferred_element_type=F32)

    acc = jnp.dot(h_scr[...], w_ref[...], preferred_element_type=F32)

    def rotary(scale):
        cos = cos_ref[...]
        sin = sin_ref[...]
        for hh in range(acc.shape[1] // RET_DK):
            a = acc[:, hh * RET_DK:(hh + 1) * RET_DK]
            o = a * cos + pltpu.roll(a, RET_DK // 2, axis=1) * sin
            if scale is not None:
                o = o * scale
            z_ref[:, hh * RET_DK:(hh + 1) * RET_DK] = o.astype(BF16)

    @pl.when(j == 0)
    def _():
        rotary(None)

    @pl.when(j == 1)
    def _():
        rotary(rk_scale)

    @pl.when(((j >= 2) & (j < 4)) | ((j >= 8) & (j < 12)))
    def _():
        z_ref[...] = acc.astype(BF16)

    @pl.when((j >= 4) & (j < 6))
    def _():
        z_ref[...] = (acc * jax.nn.sigmoid(acc)).astype(BF16)

    @pl.when((j >= 6) & (j < 8))
    def _():
        z_ref[...] = (acc * fox_scale).astype(BF16)

    @pl.when(j >= 12)
    def _():
        z_ref[...] = jax.nn.sigmoid(acc).astype(BF16)


def _inproj(x2, nw, sc, sh, cos_t, sin_t, w, wff, S):
    T, D = x2.shape
    N = w.shape[1]
    tm, tn = 1024, 512
    spb = S // tm
    kern = functools.partial(_inproj_kernel, fox_scale=LOG2E / math.sqrt(FOX_DH),
                             rk_scale=RET_DK ** -0.5)
    return pl.pallas_call(
        kern,
        out_shape=(jax.ShapeDtypeStruct((T, N), BF16),
                   jax.ShapeDtypeStruct((T, LANES), F32)),
        grid=(T // tm, N // tn),
        in_specs=[pl.BlockSpec((tm, D), lambda i, j: (i, 0)),
                  pl.BlockSpec((1, D), lambda i, j: (0, 0)),
                  pl.BlockSpec((1, 1, D), lambda i, j: (i // spb, 0, 0)),
                  pl.BlockSpec((1, 1, D), lambda i, j: (i // spb, 0, 0)),
                  pl.BlockSpec((tm, LANES), lambda i, j: (i % spb, 0)),
                  pl.BlockSpec((tm, LANES), lambda i, j: (i % spb, 0)),
                  pl.BlockSpec((D, tn), lambda i, j: (0, j)),
                  pl.BlockSpec((D, LANES), lambda i, j: (0, 0))],
        out_specs=(pl.BlockSpec((tm, tn), lambda i, j: (i, j)),
                   pl.BlockSpec((tm, LANES), lambda i, j: (i, 0))),
        scratch_shapes=[pltpu.VMEM((tm, D), BF16)],
        compiler_params=_cparams(("arbitrary", "arbitrary"), 48),
        name="inproj",
    )(x2, nw, sc, sh, cos_t, sin_t, w, wff)


def _split3(v):
    p0 = v.astype(BF16)
    r1 = v - p0.astype(F32)
    p1 = r1.astype(BF16)
    r2 = r1 - p1.astype(F32)
    p2 = r2.astype(BF16)
    return p0, p1, p2


def _foxprep_kernel(ff_ref, bf_ref, tri_ref, qa_ref, ka_ref, carry):
    @pl.when(pl.program_id(1) == 0)
    def _():
        carry[...] = jnp.zeros_like(carry)

    xv = ff_ref[0] + bf_ref[...]
    lf = jnp.minimum(xv, 0.0) - jnp.log(1.0 + jnp.exp(-jnp.abs(xv)))
    tri = tri_ref[...]
    p0, p1, p2 = _split3(lf)
    cs = (jnp.dot(tri, p0, preferred_element_type=F32)
          + jnp.dot(tri, p1, preferred_element_type=F32)
          + jnp.dot(tri, p2, preferred_element_type=F32))
    fc = cs + carry[...]
    tb = fc.shape[0]
    carry[...] = fc[tb - 1:tb, :]
    f0, f1, f2 = [p.astype(F32) for p in _split3(fc * LOG2E)]
    c = lax.broadcasted_iota(jnp.int32, fc.shape, 1) % AUG_STRIDE
    qa_ref[0] = jnp.where(c == 0, f0, jnp.where(c == 1, f1, jnp.where(
        c == 2, f2, jnp.where(c < 6, 1.0, 0.0)))).astype(BF16)
    ka_ref[0] = jnp.where(c < 3, 1.0, jnp.where(c == 3, -f0, jnp.where(
        c == 4, -f1, jnp.where(c == 5, -f2, 0.0)))).astype(BF16)


def _foxprep(ff3, bf_l):
    B, S, _ = ff3.shape
    tb = 256
    tri = jnp.asarray(np.tril(np.ones((tb, tb), np.float32)), BF16)
    return pl.pallas_call(
        _foxprep_kernel,
        out_shape=(jax.ShapeDtypeStruct((B, S, LANES), BF16),
                   jax.ShapeDtypeStruct((B, S, LANES), BF16)),
        grid=(B, S // tb),
        in_specs=[pl.BlockSpec((1, tb, LANES), lambda b, i: (b, i, 0)),
                  pl.BlockSpec((1, LANES), lambda b, i: (0, 0)),
                  pl.BlockSpec((tb, tb), lambda b, i: (0, 0))],
        out_specs=(pl.BlockSpec((1, tb, LANES), lambda b, i: (b, i, 0)),
                   pl.BlockSpec((1, tb, LANES), lambda b, i: (b, i, 0))),
        scratch_shapes=[pltpu.VMEM((1, LANES), F32)],
        compiler_params=_cparams(("arbitrary", "arbitrary"), 32),
        name="foxprep",
    )(ff3, bf_l, tri)


def _ret_consts(L):
    hs = np.arange(RET_HEADS, dtype=np.float64)
    log_gamma = np.log(1.0 - np.exp2(-5.0 - hs))
    idx = np.arange(L, dtype=np.float64)
    dist = np.abs(idx[:, None] - idx[None, :])
    chunk_ok = (idx[None, :] // RET_CHUNK) <= (idx[:, None] // RET_CHUNK)
    dmask = np.exp(log_gamma[:, None, None] * dist[None]) * chunk_ok[None]
    qdec = np.exp(log_gamma[:, None] * idx[None, :])[..., None]
    kdec = np.exp(log_gamma[:, None] * (L - idx)[None, :])[..., None]
    bdec = [float(v) for v in np.exp(log_gamma * L)]
    return (jnp.asarray(dmask, F32), jnp.asarray(qdec, F32), jnp.asarray(kdec, F32), bdec)


def _ret_kernel(rq_ref, rk_ref, rv_ref, rg_ref, gr_ref, dm_ref, qd_ref, kd_ref, wr_ref,
                o_ref, st_scr, y_scr, *, bdec):
    @pl.when(pl.program_id(1) == 0)
    def _():
        st_scr[...] = jnp.zeros_like(st_scr)

    for h in range(RET_HEADS):
        q = rq_ref[0, :, h * RET_DK:(h + 1) * RET_DK]
        k = rk_ref[0, :, h * RET_DK:(h + 1) * RET_DK]
        v = rv_ref[0, :, h * RET_DV:(h + 1) * RET_DV]
        s = lax.dot_general(q, k, _NT, preferred_element_type=F32) * dm_ref[h]
        intra = jnp.dot(s.astype(BF16), v, preferred_element_type=F32)
        st = st_scr[h]
        inter = jnp.dot(q, st.astype(BF16), preferred_element_type=F32) * qd_ref[h]
        ks = (k.astype(F32) * kd_ref[h]).astype(BF16)
        st_scr[h] = bdec[h] * st + lax.dot_general(ks, v, _TN, preferred_element_type=F32)
        ro = intra + inter
        ron = ro * lax.rsqrt(jnp.mean(ro * ro, axis=-1, keepdims=True) + EPS)
        g = rg_ref[0, :, h * RET_DV:(h + 1) * RET_DV].astype(F32)
        y_scr[:, h * RET_DV:(h + 1) * RET_DV] = (g * ron).astype(BF16)

    proj = jnp.dot(y_scr[...], wr_ref[...], preferred_element_type=F32)
    o_ref[0] = (gr_ref[0].astype(F32) * proj).astype(BF16)


def _retention(z3, wr, D):
    B, S, _ = z3.shape
    L = 256
    dmask, qdec, kdec, bdec = _ret_consts(L)
    QK = RET_HEADS * RET_DK
    V = RET_HEADS * RET_DV
    kern = functools.partial(_ret_kernel, bdec=bdec)
    full3 = lambda b, i: (0, 0, 0)
    return pl.pallas_call(
        kern,
        out_shape=jax.ShapeDtypeStruct((B, S, D), BF16),
        grid=(B, S // L),
        in_specs=[pl.BlockSpec((1, L, QK), lambda b, i: (b, i, 0)),
                  pl.BlockSpec((1, L, QK), lambda b, i: (b, i, 1)),
                  pl.BlockSpec((1, L, V), lambda b, i: (b, i, 1)),
                  pl.BlockSpec((1, L, V), lambda b, i: (b, i, 2)),
                  pl.BlockSpec((1, L, D), lambda b, i: (b, i, 6)),
                  pl.BlockSpec((RET_HEADS, L, L), full3),
                  pl.BlockSpec((RET_HEADS, L, 1), full3),
                  pl.BlockSpec((RET_HEADS, L, 1), full3),
                  pl.BlockSpec((V, D), lambda b, i: (0, 0))],
        out_specs=pl.BlockSpec((1, L, D), lambda b, i: (b, i, 0)),
        scratch_shapes=[pltpu.VMEM((RET_HEADS, RET_DK, RET_DV), F32),
                        pltpu.VMEM((L, V), BF16)],
        compiler_params=_cparams(("arbitrary", "arbitrary"), 40),
        name="ret",
    )(z3, z3, z3, z3, z3, dmask, qdec, kdec, wr)


def _fox_kernel(q_ref, qa_ref, k_ref, ka_ref, vt_ref, o_ref,
                kx_scr, vt_scr, qxt_scr, pa_scr, pb_scr, m_scr, al_scr, acc_scr, *, tq, tk):
    h = pl.program_id(1)
    i = pl.program_id(2)
    nkb = vt_scr.shape[0]

    @pl.when(i == 0)
    def _():
        kx_scr[:, :FOX_DH] = k_ref[0]
        kx_scr[:, FOX_DH:] = ka_ref[0]
        for n in range(nkb):
            vt_scr[n, :FOX_DH, :] = vt_ref[0, :, n * tk:(n + 1) * tk]
            vt_scr[n, FOX_DH:, :] = jnp.ones((vt_scr.shape[1] - FOX_DH, tk), BF16)

    lane = lax.broadcasted_iota(jnp.int32, (tq, LANES), 1)
    qa = jnp.where((lane // AUG_STRIDE) == h, qa_ref[0].astype(F32), 0.0)
    qxt_scr[:FOX_DH, :] = q_ref[0].astype(F32).T.astype(BF16)
    qxt_scr[FOX_DH:, :] = qa.T.astype(BF16)
    m_scr[...] = jnp.full(m_scr.shape, NEG_BIG, F32)
    al_scr[...] = jnp.ones_like(al_scr)
    acc_scr[...] = jnp.zeros_like(acc_scr)
    pb_scr[...] = jnp.zeros_like(pb_scr)

    def pv(blk_prev, p_prev):
        acc_scr[...] = al_scr[...] * acc_scr[...] + jnp.dot(
            vt_scr[blk_prev], p_prev[...], preferred_element_type=F32)

    def step(blk, blk_prev, p_prev, p_new, diag_shift):
        off = pl.multiple_of(blk * tk, tk)
        s = jnp.dot(kx_scr[pl.ds(off, tk), :], qxt_scr[...],
                    preferred_element_type=F32)
        if diag_shift is not None:
            key = lax.broadcasted_iota(jnp.int32, s.shape, 0) + diag_shift
            qry = lax.broadcasted_iota(jnp.int32, s.shape, 1)
            s = jnp.where(key <= qry, s, NEG_BIG)
        m_prev = m_scr[...]
        m_new = jnp.maximum(m_prev, jnp.max(s, axis=0, keepdims=True))
        alpha = jnp.exp2(m_prev - m_new)
        p_new[...] = jnp.exp2(s - m_new).astype(BF16)
        pv(blk_prev, p_prev)
        m_scr[...] = m_new
        al_scr[...] = alpha

    def body(jj, carry):
        step(2 * jj, jnp.maximum(2 * jj - 1, 0), pb_scr, pa_scr, None)
        step(2 * jj + 1, 2 * jj, pa_scr, pb_scr, None)
        return carry

    lax.fori_loop(0, i, body, 0)
    step(2 * i, jnp.maximum(2 * i - 1, 0), pb_scr, pa_scr, 0)
    step(2 * i + 1, 2 * i, pa_scr, pb_scr, tk)
    pv(2 * i + 1, pb_scr)
    acc = acc_scr[...]
    o_ref[0] = (acc[:FOX_DH] / acc[FOX_DH:FOX_DH + 1]).T.astype(BF16)


def _fox(z3, qa, ka, vt):
    B, S, _ = z3.shape
    tq, tk = 1024, 512
    assert tq == 2 * tk
    ones_rows = 16
    W = FOX_HEADS * FOX_DH
    qb, kb = 3 * W // FOX_DH, 4 * W // FOX_DH
    kern = functools.partial(_fox_kernel, tq=tq, tk=tk)
    return pl.pallas_call(
        kern,
        out_shape=jax.ShapeDtypeStruct((B, S, W), BF16),
        grid=(B, FOX_HEADS, S // tq),
        in_specs=[pl.BlockSpec((1, tq, FOX_DH), lambda b, h, i: (b, i, qb + h)),
                  pl.BlockSpec((1, tq, LANES), lambda b, h, i: (b, i, 0)),
                  pl.BlockSpec((1, S, FOX_DH), lambda b, h, i: (b, 0, kb + h)),
                  pl.BlockSpec((1, S, LANES), lambda b, h, i: (b, 0, 0)),
                  pl.BlockSpec((1, FOX_DH, S), lambda b, h, i: (b, h, 0))],
        out_specs=pl.BlockSpec((1, tq, FOX_DH), lambda b, h, i: (b, i, h)),
        scratch_shapes=[pltpu.VMEM((S, 2 * FOX_DH), BF16),
                        pltpu.VMEM((S // tk, FOX_DH + ones_rows, tk), BF16),
                        pltpu.VMEM((2 * FOX_DH, tq), BF16),
                        pltpu.VMEM((tk, tq), BF16),
                        pltpu.VMEM((tk, tq), BF16),
                        pltpu.VMEM((1, tq), F32),
                        pltpu.VMEM((1, tq), F32),
                        pltpu.VMEM((FOX_DH + ones_rows, tq), F32)],
        compiler_params=_cparams(("arbitrary", "arbitrary", "arbitrary"), 48),
        name="fox",
    )(z3, qa, z3, ka, vt)


def _merge_kernel(a_ref, yf_ref, gf_ref, x_ref, g1_ref, wf_ref, wo_ref, o_ref):
    fo = jnp.dot(yf_ref[...], wf_ref[...], preferred_element_type=F32)
    merged = a_ref[...].astype(F32) + gf_ref[...].astype(F32) * fo
    out = jnp.dot(merged.astype(BF16), wo_ref[...], preferred_element_type=F32)
    o_ref[...] = x_ref[...] + g1_ref[0] * out


def _merge(a2, yf2, z2, x2, g1, wf, wo, S):
    T, D = x2.shape
    tm = 512
    spb = S // tm
    row = lambda i: (i, 0)
    return pl.pallas_call(
        _merge_kernel,
        out_shape=jax.ShapeDtypeStruct((T, D), F32),
        grid=(T // tm,),
        in_specs=[pl.BlockSpec((tm, D), row),
                  pl.BlockSpec((tm, D), row),
                  pl.BlockSpec((tm, D), lambda i: (i, 7)),
                  pl.BlockSpec((tm, D), row),
                  pl.BlockSpec((1, 1, D), lambda i: (i // spb, 0, 0)),
                  pl.BlockSpec((D, D), lambda i: (0, 0)),
                  pl.BlockSpec((D, D), lambda i: (0, 0))],
        out_specs=pl.BlockSpec((tm, D), row),
        compiler_params=_cparams(("arbitrary",), 40),
        name="merge",
    )(a2, yf2, z2, x2, g1, wf, wo)


def _ffn_kernel(x_ref, nw_ref, sc_ref, sh_ref, g2_ref, wg_ref, wu_ref, wd_ref, nf_ref,
                o_ref, h_scr, acc_scr):
    j = pl.program_id(1)

    @pl.when(j == 0)
    def _():
        x = x_ref[...]
        ms = jnp.mean(x * x, axis=-1, keepdims=True)
        y = x * lax.rsqrt(ms + EPS) * nw_ref[...]
        h_scr[...] = (y * (1.0 + sc_ref[0]) + sh_ref[0]).astype(BF16)
        acc_scr[...] = jnp.zeros_like(acc_scr)

    h = h_scr[...]
    g = jnp.dot(h, wg_ref[...], preferred_element_type=F32)
    u = jnp.dot(h, wu_ref[...], preferred_element_type=F32)
    a = (g * jax.nn.sigmoid(g) * u).astype(BF16)
    acc_scr[...] += jnp.dot(a, wd_ref[...], preferred_element_type=F32)

    @pl.when(j == pl.num_programs(1) - 1)
    def _():
        x2 = x_ref[...] + g2_ref[0] * acc_scr[...]
        ms = jnp.mean(x2 * x2, axis=-1, keepdims=True)
        o_ref[...] = x2 * lax.rsqrt(ms + EPS) * nf_ref[...]


def _ffn(x1, nw, sc, sh, g2, wg, wu, wd, nf, S):
    T, D = x1.shape
    FF = wg.shape[1]
    tm, tf = 512, FF // 2
    spb = S // tm
    bidx = lambda i, j: (i // spb, 0, 0)
    return pl.pallas_call(
        _ffn_kernel,
        out_shape=jax.ShapeDtypeStruct((T, D), F32),
        grid=(T // tm, FF // tf),
        in_specs=[pl.BlockSpec((tm, D), lambda i, j: (i, 0)),
                  pl.BlockSpec((1, D), lambda i, j: (0, 0)),
                  pl.BlockSpec((1, 1, D), bidx),
                  pl.BlockSpec((1, 1, D), bidx),
                  pl.BlockSpec((1, 1, D), bidx),
                  pl.BlockSpec((D, tf), lambda i, j: (0, j)),
                  pl.BlockSpec((D, tf), lambda i, j: (0, j)),
                  pl.BlockSpec((tf, D), lambda i, j: (j, 0)),
                  pl.BlockSpec((1, D), lambda i, j: (0, 0))],
        out_specs=pl.BlockSpec((tm, D), lambda i, j: (i, 0)),
        scratch_shapes=[pltpu.VMEM((tm, D), BF16), pltpu.VMEM((tm, D), F32)],
        compiler_params=_cparams(("arbitrary", "arbitrary"), 56),
        name="ffn",
    )(x1, nw, sc, sh, g2, wg, wu, wd, nf)


def _rope_tables(S):
    half = RET_DK // 2
    freqs = ROPE_THETA ** (-jnp.arange(half, dtype=F32) / half)
    ang = jnp.arange(S, dtype=F32)[:, None] * freqs[None, :]
    cos, sin = jnp.cos(ang), jnp.sin(ang)
    return (jnp.concatenate([cos, cos], axis=1), jnp.concatenate([-sin, sin], axis=1))


def _aug_lanes(v8):
    rep = jnp.repeat(v8[..., None], 6, axis=-1)
    pad = jnp.zeros(v8.shape + (AUG_STRIDE - 6,), v8.dtype)
    return jnp.concatenate([rep, pad], axis=-1).reshape(v8.shape[:-1] + (LANES,))


def _layer(x, mod, norm1_w, w_in, b_f, ret_proj, fox_proj, w_out, norm2_w,
           w_gate, w_up, w_down, norm_out_w, tables):
    B, S, D = x.shape
    T = B * S
    sh1, sc1, g1, sh2, sc2, g2 = [m.reshape(B, 1, D) for m in jnp.split(mod, 6, axis=-1)]
    n_main = w_in.shape[1] - FOX_HEADS - 2 * D
    w_main = jnp.concatenate([w_in[:, :n_main], w_in[:, n_main + FOX_HEADS:]],
                             axis=1).astype(BF16)
    w_ff = _aug_lanes(w_in[:, n_main:n_main + FOX_HEADS]).astype(BF16)
    bf_l = _aug_lanes(b_f).reshape(1, LANES)

    x2 = x.reshape(T, D)
    z, ff = _inproj(x2, norm1_w.reshape(1, D), sc1, sh1, tables[0], tables[1],
                    w_main, w_ff, S)
    z3 = z.reshape(B, S, z.shape[1])
    qa, ka = _foxprep(ff.reshape(B, S, LANES), bf_l)
    a = _retention(z3, ret_proj.astype(BF16), D)
    vt = jnp.swapaxes(z3[:, :, 5 * D:6 * D], 1, 2)
    yf = _fox(z3, qa, ka, vt)
    x1 = _merge(a.reshape(T, D), yf.reshape(T, D), z, x2, g1,
                fox_proj.astype(BF16), w_out.astype(BF16), S)
    out = _ffn(x1, norm2_w.reshape(1, D), sc2, sh2, g2, w_gate.astype(BF16),
               w_up.astype(BF16), w_down.astype(BF16), norm_out_w.reshape(1, D), S)
    return out.reshape(B, S, D)


def kernel(x, c, ada_w, ada_b, norm1_w, w_in, b_f, ret_proj, fox_proj, w_out,
           norm2_w, w_gate, w_up, w_down, norm_f_w):
    depth = ada_w.shape[0]
    assert depth == 1, "the final RMSNorm is fused into the last layer's channel mixer"
    tables = _rope_tables(x.shape[1])
    l = 0
    mod = _ada(c, ada_w[l], ada_b[l])
    return _layer(x, mod, norm1_w[l], w_in[l], b_f[l], ret_proj[l], fox_proj[l], w_out[l],
                  norm2_w[l], w_gate[l], w_up[l], w_down[l], norm_f_w, tables)
```

```python
import functools
import math

import numpy as np
import jax
import jax.numpy as jnp
from jax import lax
from jax.experimental import pallas as pl
from jax.experimental.pallas import tpu as pltpu

F32 = jnp.float32
BF16 = jnp.bfloat16

EPS = 1e-6
ROPE_THETA = 10000.0
RET_HEADS = 4
RET_DK = 128
RET_DV = 256
FOX_HEADS = 8
FOX_DH = 128
RET_CHUNK = 64
LANES = 128
AUG_STRIDE = 16

NEG_BIG = -1e30
LOG2E = math.log2(math.e)

_NT = (((1,), (1,)), ((), ()))
_TN = (((0,), (0,)), ((), ()))


def _cparams(sem, vmem_mb, flags=None):
    return pltpu.CompilerParams(dimension_semantics=sem,
                                vmem_limit_bytes=vmem_mb * 1024 * 1024, flags=flags)


def _ada_kernel(ct_ref, w_ref, b_ref, o_ref):
    ct = ct_ref[...]
    act = ct * jax.nn.sigmoid(ct)
    w = w_ref[...]
    for b in range(o_ref.shape[0]):
        o_ref[b:b + 1, :] = (jnp.sum(act[:, b:b + 1] * w, axis=0, keepdims=True)
                             + b_ref[...])


def _ada(c, w, b):
    B, D = c.shape
    N = w.shape[1]
    tn = 1536
    return pl.pallas_call(
        _ada_kernel,
        out_shape=jax.ShapeDtypeStruct((B, N), F32),
        grid=(N // tn,),
        in_specs=[pl.BlockSpec((D, B), lambda j: (0, 0)),
                  pl.BlockSpec((D, tn), lambda j: (0, j)),
                  pl.BlockSpec((1, tn), lambda j: (0, j))],
        out_specs=pl.BlockSpec((B, tn), lambda j: (0, j)),
        compiler_params=_cparams(("arbitrary",), 40),
        name="ada",
    )(c.T, w, b.reshape(1, N))


INPROJ_CHUNK = 512
Z_COLS = {"rq": 0, "rk": 512, "rv": 1024, "rg": 2048, "fq": 3072, "fk": 4096,
          "gr": 5120, "gf": 6144, "end": 7168}


def _split3(v):
    p0 = v.astype(BF16)
    r1 = v - p0.astype(F32)
    p1 = r1.astype(BF16)
    r2 = r1 - p1.astype(F32)
    p2 = r2.astype(BF16)
    return p0, p1, p2


def _inproj_kernel(x_ref, nw_ref, sc_ref, sh_ref, cos_ref, sin_ref, w_ref, wvt_ref, wff_ref,
                   bf_ref, tri_ref, z_ref, vt_ref, qa_ref, ka_ref, carry,
                   *, fox_scale, rk_scale, spb):
    x = x_ref[...]
    ms = jnp.mean(x * x, axis=-1, keepdims=True)
    y = x * lax.rsqrt(ms + EPS) * nw_ref[...]
    h = (y * (1.0 + sc_ref[0]) + sh_ref[0]).astype(BF16)
    cos = cos_ref[...]
    sin = sin_ref[...]
    cw = INPROJ_CHUNK

    def chunk(c0):
        return jnp.dot(h, w_ref[:, c0:c0 + cw], preferred_element_type=F32)

    def rotary(c0, scale):
        acc = chunk(c0)
        for hh in range(cw // RET_DK):
            a = acc[:, hh * RET_DK:(hh + 1) * RET_DK]
            o = a * cos + pltpu.roll(a, RET_DK // 2, axis=1) * sin
            if scale is not None:
                o = o * scale
            z_ref[:, c0 + hh * RET_DK:c0 + (hh + 1) * RET_DK] = o.astype(BF16)

    def group(name, nxt, fn):
        for c0 in range(Z_COLS[name], Z_COLS[nxt], cw):
            z_ref[:, c0:c0 + cw] = fn(chunk(c0)).astype(BF16)

    rotary(Z_COLS["rq"], None)
    rotary(Z_COLS["rk"], rk_scale)
    group("rv", "rg", lambda a: a)
    group("rg", "fq", lambda a: a * jax.nn.sigmoid(a))
    group("fq", "fk", lambda a: a * fox_scale)
    group("fk", "gr", lambda a: a)
    group("gr", "end", jax.nn.sigmoid)
    for r0 in range(0, wvt_ref.shape[0], cw):
        vt_ref[0, r0:r0 + cw, :] = lax.dot_general(
            wvt_ref[r0:r0 + cw, :], h, _NT, preferred_element_type=F32).astype(BF16)

    xv = jnp.dot(h, wff_ref[...], preferred_element_type=F32) + bf_ref[...]
    lf = jnp.minimum(xv, 0.0) - jnp.log(1.0 + jnp.exp(-jnp.abs(xv)))
    tri = tri_ref[...]
    p0, p1, p2 = _split3(lf)
    cs = (jnp.dot(tri, p0, preferred_element_type=F32)
          + jnp.dot(tri, p1, preferred_element_type=F32)
          + jnp.dot(tri, p2, preferred_element_type=F32))
    first = (pl.program_id(0) % spb) == 0
    fc = cs + jnp.where(first, 0.0, carry[...])
    tm = fc.shape[0]
    carry[...] = fc[tm - 1:tm, :]
    f0, f1, f2 = [p.astype(F32) for p in _split3(fc * LOG2E)]
    c = lax.broadcasted_iota(jnp.int32, fc.shape, 1) % AUG_STRIDE
    qa_ref[0] = jnp.where(c == 0, f0, jnp.where(c == 1, f1, jnp.where(
        c == 2, f2, jnp.where(c < 6, 1.0, 0.0)))).astype(BF16)
    ka_ref[0] = jnp.where(c < 3, 1.0, jnp.where(c == 3, -f0, jnp.where(
        c == 4, -f1, jnp.where(c == 5, -f2, 0.0)))).astype(BF16)


def _inproj(x2, nw, sc, sh, cos_t, sin_t, w, wvt, wff, bf_l, B, S):
    T, D = x2.shape
    N = w.shape[1]
    NV = wvt.shape[0]
    tm = 512
    spb = S // tm
    tri = jnp.asarray(np.tril(np.ones((tm, tm), np.float32)), BF16)
    kern = functools.partial(_inproj_kernel, fox_scale=LOG2E / math.sqrt(FOX_DH),
                             rk_scale=RET_DK ** -0.5, spb=spb)
    const = lambda shape: pl.BlockSpec(shape, lambda i: (0,) * len(shape),
                                       pipeline_mode=pl.Buffered(1))
    return pl.pallas_call(
        kern,
        out_shape=(jax.ShapeDtypeStruct((T, N), BF16),
                   jax.ShapeDtypeStruct((B, NV, S), BF16),
                   jax.ShapeDtypeStruct((B, S, LANES), BF16),
                   jax.ShapeDtypeStruct((B, S, LANES), BF16)),
        grid=(T // tm,),
        in_specs=[pl.BlockSpec((tm, D), lambda i: (i, 0)),
                  const((1, D)),
                  pl.BlockSpec((1, 1, D), lambda i: (i // spb, 0, 0)),
                  pl.BlockSpec((1, 1, D), lambda i: (i // spb, 0, 0)),
                  pl.BlockSpec((tm, LANES), lambda i: (i % spb, 0)),
                  pl.BlockSpec((tm, LANES), lambda i: (i % spb, 0)),
                  const((D, N)),
                  const((NV, D)),
                  const((D, LANES)),
                  const((1, LANES)),
                  const((tm, tm))],
        out_specs=(pl.BlockSpec((tm, N), lambda i: (i, 0)),
                   pl.BlockSpec((1, NV, tm), lambda i: (i // spb, 0, i % spb)),
                   pl.BlockSpec((1, tm, LANES), lambda i: (i // spb, i % spb, 0)),
                   pl.BlockSpec((1, tm, LANES), lambda i: (i // spb, i % spb, 0))),
        scratch_shapes=[pltpu.VMEM((1, LANES), F32)],
        compiler_params=_cparams(("arbitrary",), 56),
        name="inproj",
    )(x2, nw, sc, sh, cos_t, sin_t, w, wvt, wff, bf_l, tri)


def _ret_consts(L):
    hs = np.arange(RET_HEADS, dtype=np.float64)
    log_gamma = np.log(1.0 - np.exp2(-5.0 - hs))
    idx = np.arange(L, dtype=np.float64)
    dist = np.abs(idx[:, None] - idx[None, :])
    chunk_ok = (idx[None, :] // RET_CHUNK) <= (idx[:, None] // RET_CHUNK)
    dmask = np.exp(log_gamma[:, None, None] * dist[None]) * chunk_ok[None]
    qdec = np.exp(log_gamma[:, None] * idx[None, :])[..., None]
    kdec = np.exp(log_gamma[:, None] * (L - idx)[None, :])[..., None]
    bdec = [float(v) for v in np.exp(log_gamma * L)]
    return (jnp.asarray(dmask, F32), jnp.asarray(qdec, F32), jnp.asarray(kdec, F32), bdec)


def _ret_kernel(rq_ref, rk_ref, rv_ref, rg_ref, gr_ref, dm_ref, qd_ref, kd_ref, wr_ref,
                o_ref, st_scr, y_scr, *, bdec):
    @pl.when(pl.program_id(1) == 0)
    def _():
        st_scr[...] = jnp.zeros_like(st_scr)

    for h in range(RET_HEADS):
        q = rq_ref[0, :, h * RET_DK:(h + 1) * RET_DK]
        k = rk_ref[0, :, h * RET_DK:(h + 1) * RET_DK]
        v = rv_ref[0, :, h * RET_DV:(h + 1) * RET_DV]
        s = lax.dot_general(q, k, _NT, preferred_element_type=F32) * dm_ref[h]
        intra = jnp.dot(s.astype(BF16), v, preferred_element_type=F32)
        st = st_scr[h]
        inter = jnp.dot(q, st.astype(BF16), preferred_element_type=F32) * qd_ref[h]
        ks = (k.astype(F32) * kd_ref[h]).astype(BF16)
        st_scr[h] = bdec[h] * st + lax.dot_general(ks, v, _TN, preferred_element_type=F32)
        ro = intra + inter
        ron = ro * lax.rsqrt(jnp.mean(ro * ro, axis=-1, keepdims=True) + EPS)
        g = rg_ref[0, :, h * RET_DV:(h + 1) * RET_DV].astype(F32)
        y_scr[:, h * RET_DV:(h + 1) * RET_DV] = (g * ron).astype(BF16)

    proj = jnp.dot(y_scr[...], wr_ref[...], preferred_element_type=F32)
    o_ref[0] = (gr_ref[0].astype(F32) * proj).astype(BF16)


def _retention(z3, wr, D):
    B, S, _ = z3.shape
    L = 256
    dmask, qdec, kdec, bdec = _ret_consts(L)
    QK = RET_HEADS * RET_DK
    V = RET_HEADS * RET_DV
    kern = functools.partial(_ret_kernel, bdec=bdec)
    full3 = lambda b, i: (0, 0, 0)
    return pl.pallas_call(
        kern,
        out_shape=jax.ShapeDtypeStruct((B, S, D), BF16),
        grid=(B, S // L),
        in_specs=[pl.BlockSpec((1, L, QK), lambda b, i: (b, i, 0)),
                  pl.BlockSpec((1, L, QK), lambda b, i: (b, i, 1)),
                  pl.BlockSpec((1, L, V), lambda b, i: (b, i, 1)),
                  pl.BlockSpec((1, L, V), lambda b, i: (b, i, 2)),
                  pl.BlockSpec((1, L, D), lambda b, i: (b, i, 5)),
                  pl.BlockSpec((RET_HEADS, L, L), full3),
                  pl.BlockSpec((RET_HEADS, L, 1), full3),
                  pl.BlockSpec((RET_HEADS, L, 1), full3),
                  pl.BlockSpec((V, D), lambda b, i: (0, 0))],
        out_specs=pl.BlockSpec((1, L, D), lambda b, i: (b, i, 0)),
        scratch_shapes=[pltpu.VMEM((RET_HEADS, RET_DK, RET_DV), F32),
                        pltpu.VMEM((L, V), BF16)],
        compiler_params=_cparams(("arbitrary", "arbitrary"), 40),
        name="ret",
    )(z3, z3, z3, z3, z3, dmask, qdec, kdec, wr)


def _fox_kernel(q_ref, qa_ref, k_ref, ka_ref, vt_ref, o_ref,
                kx_scr, vt_scr, qxt_scr, sa_scr, sb_scr, mba_scr, mbb_scr, pa_scr, pb_scr,
                m_scr, al_scr, acc_scr, *, tq, tk):
    h = pl.program_id(1)
    i = pl.program_id(2)
    nkb = vt_scr.shape[0]

    @pl.when(i == 0)
    def _():
        kx_scr[:, :FOX_DH] = k_ref[0]
        kx_scr[:, FOX_DH:] = ka_ref[0]
        for n in range(nkb):
            vt_scr[n, :FOX_DH, :] = vt_ref[0, :, n * tk:(n + 1) * tk]
            vt_scr[n, FOX_DH:, :] = jnp.ones((vt_scr.shape[1] - FOX_DH, tk), BF16)

    lane = lax.broadcasted_iota(jnp.int32, (tq, LANES), 1)
    qa = jnp.where((lane // AUG_STRIDE) == h, qa_ref[0].astype(F32), 0.0)
    qxt_scr[:FOX_DH, :] = q_ref[0].astype(F32).T.astype(BF16)
    qxt_scr[FOX_DH:, :] = qa.T.astype(BF16)
    m_scr[...] = jnp.full(m_scr.shape, NEG_BIG, F32)
    al_scr[...] = jnp.ones_like(al_scr)
    acc_scr[...] = jnp.zeros_like(acc_scr)
    pb_scr[...] = jnp.zeros_like(pb_scr)

    def scores(blk, s_out, mb_out):
        off = pl.multiple_of(blk * tk, tk)
        s = jnp.dot(kx_scr[pl.ds(off, tk), :], qxt_scr[...],
                    preferred_element_type=F32)
        s_out[...] = s
        if mb_out is not None:
            mb_out[...] = jnp.max(s, axis=0, keepdims=True)

    def softmax(s_in, mb_in, p_out, diag_shift):
        s = s_in[...]
        if diag_shift is None:
            mb = mb_in[...]
        else:
            key = lax.broadcasted_iota(jnp.int32, s.shape, 0) + diag_shift
            qry = lax.broadcasted_iota(jnp.int32, s.shape, 1)
            s = jnp.where(key <= qry, s, NEG_BIG)
            mb = jnp.max(s, axis=0, keepdims=True)
        m_prev = m_scr[...]
        m_new = jnp.maximum(m_prev, mb)
        p_out[...] = jnp.exp2((s - m_new).astype(BF16))
        m_scr[...] = m_new
        return jnp.exp2(m_prev - m_new)

    def pv(blk_prev, p_in):
        acc_scr[...] = al_scr[...] * acc_scr[...] + jnp.dot(
            vt_scr[blk_prev], p_in[...], preferred_element_type=F32)

    def step(n, cur, nxt, p_prv, diag_shift):
        s_cur, mb_cur, p_cur = cur
        if nxt is not None:
            scores(n + 1, nxt[0], nxt[1])
        alpha = softmax(s_cur, mb_cur, p_cur, diag_shift)
        pv(jnp.maximum(n - 1, 0), p_prv)
        al_scr[...] = alpha

    buf_a = (sa_scr, mba_scr, pa_scr)
    buf_b = (sb_scr, mbb_scr, pb_scr)

    def body(jj, carry):
        step(2 * jj, buf_a, buf_b, pb_scr, None)
        step(2 * jj + 1, buf_b, buf_a, pa_scr, None)
        return carry

    scores(0, sa_scr, mba_scr)
    lax.fori_loop(0, i, body, 0)
    step(2 * i, buf_a, (sb_scr, None), pb_scr, 0)
    step(2 * i + 1, buf_b, None, pa_scr, tk)
    pv(2 * i + 1, pb_scr)
    acc = acc_scr[...]
    o_ref[0] = (acc[:FOX_DH] / acc[FOX_DH:FOX_DH + 1]).T.astype(BF16)


def _fox(z3, qa, ka, vt):
    B, S, _ = z3.shape
    tq, tk = 1024, 512
    assert tq == 2 * tk
    ones_rows = 16
    W = FOX_HEADS * FOX_DH
    qb, kb = Z_COLS["fq"] // FOX_DH, Z_COLS["fk"] // FOX_DH
    kern = functools.partial(_fox_kernel, tq=tq, tk=tk)
    return pl.pallas_call(
        kern,
        out_shape=jax.ShapeDtypeStruct((B, S, W), BF16),
        grid=(B, FOX_HEADS, S // tq),
        in_specs=[pl.BlockSpec((1, tq, FOX_DH), lambda b, h, i: (b, i, qb + h)),
                  pl.BlockSpec((1, tq, LANES), lambda b, h, i: (b, i, 0)),
                  pl.BlockSpec((1, S, FOX_DH), lambda b, h, i: (b, 0, kb + h)),
                  pl.BlockSpec((1, S, LANES), lambda b, h, i: (b, 0, 0)),
                  pl.BlockSpec((1, FOX_DH, S), lambda b, h, i: (b, h, 0))],
        out_specs=pl.BlockSpec((1, tq, FOX_DH), lambda b, h, i: (b, i, h)),
        scratch_shapes=[pltpu.VMEM((S, 2 * FOX_DH), BF16),
                        pltpu.VMEM((S // tk, FOX_DH + ones_rows, tk), BF16),
                        pltpu.VMEM((2 * FOX_DH, tq), BF16),
                        pltpu.VMEM((tk, tq), F32),
                        pltpu.VMEM((tk, tq), F32),
                        pltpu.VMEM((1, tq), F32),
                        pltpu.VMEM((1, tq), F32),
                        pltpu.VMEM((tk, tq), BF16),
                        pltpu.VMEM((tk, tq), BF16),
                        pltpu.VMEM((1, tq), F32),
                        pltpu.VMEM((1, tq), F32),
                        pltpu.VMEM((FOX_DH + ones_rows, tq), F32)],
        compiler_params=_cparams(("arbitrary", "arbitrary", "arbitrary"), 48),
        name="fox",
    )(z3, qa, z3, ka, vt)


def _merge_kernel(a_ref, yf_ref, gf_ref, x_ref, g1_ref, wf_ref, wo_ref, o_ref):
    fo = jnp.dot(yf_ref[...], wf_ref[...], preferred_element_type=F32)
    merged = a_ref[...].astype(F32) + gf_ref[...].astype(F32) * fo
    out = jnp.dot(merged.astype(BF16), wo_ref[...], preferred_element_type=F32)
    o_ref[...] = x_ref[...] + g1_ref[0] * out


def _merge(a2, yf2, z2, x2, g1, wf, wo, S):
    T, D = x2.shape
    tm = 512
    spb = S // tm
    row = lambda i: (i, 0)
    return pl.pallas_call(
        _merge_kernel,
        out_shape=jax.ShapeDtypeStruct((T, D), F32),
        grid=(T // tm,),
        in_specs=[pl.BlockSpec((tm, D), row),
                  pl.BlockSpec((tm, D), row),
                  pl.BlockSpec((tm, D), lambda i: (i, 6)),
                  pl.BlockSpec((tm, D), row),
                  pl.BlockSpec((1, 1, D), lambda i: (i // spb, 0, 0)),
                  pl.BlockSpec((D, D), lambda i: (0, 0)),
                  pl.BlockSpec((D, D), lambda i: (0, 0))],
        out_specs=pl.BlockSpec((tm, D), row),
        compiler_params=_cparams(("arbitrary",), 40),
        name="merge",
    )(a2, yf2, z2, x2, g1, wf, wo)


def _ffn_kernel(x_ref, nw_ref, sc_ref, sh_ref, g2_ref, wg_ref, wu_ref, wd_ref, nf_ref,
                o_ref, h_scr, acc_scr):
    j = pl.program_id(1)

    @pl.when(j == 0)
    def _():
        x = x_ref[...]
        ms = jnp.mean(x * x, axis=-1, keepdims=True)
        y = x * lax.rsqrt(ms + EPS) * nw_ref[...]
        h_scr[...] = (y * (1.0 + sc_ref[0]) + sh_ref[0]).astype(BF16)
        acc_scr[...] = jnp.zeros_like(acc_scr)

    h = h_scr[...]
    g = jnp.dot(h, wg_ref[...], preferred_element_type=F32)
    u = jnp.dot(h, wu_ref[...], preferred_element_type=F32)
    a = (g * jax.nn.sigmoid(g) * u).astype(BF16)
    acc_scr[...] += jnp.dot(a, wd_ref[...], preferred_element_type=F32)

    @pl.when(j == pl.num_programs(1) - 1)
    def _():
        x2 = x_ref[...] + g2_ref[0] * acc_scr[...]
        ms = jnp.mean(x2 * x2, axis=-1, keepdims=True)
        o_ref[...] = x2 * lax.rsqrt(ms + EPS) * nf_ref[...]


def _ffn(x1, nw, sc, sh, g2, wg, wu, wd, nf, S):
    T, D = x1.shape
    FF = wg.shape[1]
    tm, tf = 512, FF // 2
    spb = S // tm
    bidx = lambda i, j: (i // spb, 0, 0)
    return pl.pallas_call(
        _ffn_kernel,
        out_shape=jax.ShapeDtypeStruct((T, D), F32),
        grid=(T // tm, FF // tf),
        in_specs=[pl.BlockSpec((tm, D), lambda i, j: (i, 0)),
                  pl.BlockSpec((1, D), lambda i, j: (0, 0)),
                  pl.BlockSpec((1, 1, D), bidx),
                  pl.BlockSpec((1, 1, D), bidx),
                  pl.BlockSpec((1, 1, D), bidx),
                  pl.BlockSpec((D, tf), lambda i, j: (0, j)),
                  pl.BlockSpec((D, tf), lambda i, j: (0, j)),
                  pl.BlockSpec((tf, D), lambda i, j: (j, 0)),
                  pl.BlockSpec((1, D), lambda i, j: (0, 0))],
        out_specs=pl.BlockSpec((tm, D), lambda i, j: (i, 0)),
        scratch_shapes=[pltpu.VMEM((tm, D), BF16), pltpu.VMEM((tm, D), F32)],
        compiler_params=_cparams(("arbitrary", "arbitrary"), 56),
        name="ffn",
    )(x1, nw, sc, sh, g2, wg, wu, wd, nf)


def _rope_tables(S):
    half = RET_DK // 2
    freqs = ROPE_THETA ** (-jnp.arange(half, dtype=F32) / half)
    ang = jnp.arange(S, dtype=F32)[:, None] * freqs[None, :]
    cos, sin = jnp.cos(ang), jnp.sin(ang)
    return (jnp.concatenate([cos, cos], axis=1), jnp.concatenate([-sin, sin], axis=1))


def _aug_lanes(v8):
    rep = jnp.repeat(v8[..., None], 6, axis=-1)
    pad = jnp.zeros(v8.shape + (AUG_STRIDE - 6,), v8.dtype)
    return jnp.concatenate([rep, pad], axis=-1).reshape(v8.shape[:-1] + (LANES,))


def _layer(x, mod, norm1_w, w_in, b_f, ret_proj, fox_proj, w_out, norm2_w,
           w_gate, w_up, w_down, norm_out_w, tables):
    B, S, D = x.shape
    T = B * S
    sh1, sc1, g1, sh2, sc2, g2 = [m.reshape(B, 1, D) for m in jnp.split(mod, 6, axis=-1)]
    offs = np.cumsum([0, RET_HEADS * RET_DK, RET_HEADS * RET_DK, RET_HEADS * RET_DV,
                      RET_HEADS * RET_DV, FOX_HEADS * FOX_DH, FOX_HEADS * FOX_DH,
                      FOX_HEADS * FOX_DH, FOX_HEADS, D, D])
    o_fv, o_ff, o_gr = int(offs[6]), int(offs[7]), int(offs[8])
    w_main = jnp.concatenate([w_in[:, :o_fv], w_in[:, o_gr:]], axis=1).astype(BF16)
    w_vt = w_in[:, o_fv:o_ff].T.astype(BF16)
    w_ff = _aug_lanes(w_in[:, o_ff:o_gr]).astype(BF16)
    bf_l = _aug_lanes(b_f).reshape(1, LANES)

    x2 = x.reshape(T, D)
    z, vt, qa, ka = _inproj(x2, norm1_w.reshape(1, D), sc1, sh1, tables[0], tables[1],
                            w_main, w_vt, w_ff, bf_l, B, S)
    z3 = z.reshape(B, S, z.shape[1])
    a = _retention(z3, ret_proj.astype(BF16), D)
    yf = _fox(z3, qa, ka, vt)
    x1 = _merge(a.reshape(T, D), yf.reshape(T, D), z, x2, g1,
                fox_proj.astype(BF16), w_out.astype(BF16), S)
    out = _ffn(x1, norm2_w.reshape(1, D), sc2, sh2, g2, w_gate.astype(BF16),
               w_up.astype(BF16), w_down.astype(BF16), norm_out_w.reshape(1, D), S)
    return out.reshape(B, S, D)


def kernel(x, c, ada_w, ada_b, norm1_w, w_in, b_f, ret_proj, fox_proj, w_out,
           norm2_w, w_gate, w_up, w_down, norm_f_w):
    depth = ada_w.shape[0]
    assert depth == 1, "the final RMSNorm is fused into the last layer's channel mixer"
    tables = _rope_tables(x.shape[1])
    l = 0
    mod = _ada(c, ada_w[l], ada_b[l])
    return _layer(x, mod, norm1_w[l], w_in[l], b_f[l], ret_proj[l], fox_proj[l], w_out[l],
                  norm2_w[l], w_gate[l], w_up[l], w_down[l], norm_f_w, tables)
```

```python
import functools
import math

import numpy as np
import jax
import jax.numpy as jnp
from jax import lax
from jax.experimental import pallas as pl
from jax.experimental.pallas import tpu as pltpu

F32 = jnp.float32
BF16 = jnp.bfloat16

EPS = 1e-6
ROPE_THETA = 10000.0
RET_HEADS = 4
RET_DK = 128
RET_DV = 256
FOX_HEADS = 8
FOX_DH = 128
RET_CHUNK = 64
LANES = 128
AUG_STRIDE = 16

NEG_BIG = -1e30
LOG2E = math.log2(math.e)

_NT = (((1,), (1,)), ((), ()))
_TN = (((0,), (0,)), ((), ()))


def _cparams(sem, vmem_mb, flags=None):
    return pltpu.CompilerParams(dimension_semantics=sem,
                                vmem_limit_bytes=vmem_mb * 1024 * 1024, flags=flags)


def _ada_kernel(ct_ref, w_ref, b_ref, o_ref):
    ct = ct_ref[...]
    act = ct * jax.nn.sigmoid(ct)
    w = w_ref[...]
    for b in range(o_ref.shape[0]):
        o_ref[b:b + 1, :] = (jnp.sum(act[:, b:b + 1] * w, axis=0, keepdims=True)
                             + b_ref[...])


def _ada(c, w, b):
    B, D = c.shape
    N = w.shape[1]
    tn = 1536
    return pl.pallas_call(
        _ada_kernel,
        out_shape=jax.ShapeDtypeStruct((B, N), F32),
        grid=(N // tn,),
        in_specs=[pl.BlockSpec((D, B), lambda j: (0, 0)),
                  pl.BlockSpec((D, tn), lambda j: (0, j)),
                  pl.BlockSpec((1, tn), lambda j: (0, j))],
        out_specs=pl.BlockSpec((B, tn), lambda j: (0, j)),
        compiler_params=_cparams(("arbitrary",), 40),
        name="ada",
    )(c.T, w, b.reshape(1, N))


INPROJ_CHUNK = 512
Z_COLS = {"rq": 0, "rk": 512, "rv": 1024, "rg": 2048, "fq": 3072, "fk": 4096,
          "gr": 5120, "gf": 6144, "end": 7168}


def _split3(v):
    p0 = v.astype(BF16)
    r1 = v - p0.astype(F32)
    p1 = r1.astype(BF16)
    r2 = r1 - p1.astype(F32)
    p2 = r2.astype(BF16)
    return p0, p1, p2


def _inproj_kernel(x_ref, nw_ref, sc_ref, sh_ref, cos_ref, sin_ref, wa_ref, wb_ref, wvt_ref,
                   wff_ref, bf_ref, tri_ref, z_ref, vt_ref, qa_ref, ka_ref, carry,
                   *, fox_scale, rk_scale, spb):
    x = x_ref[...]
    ms = jnp.mean(x * x, axis=-1, keepdims=True)
    y = x * lax.rsqrt(ms + EPS) * nw_ref[...]
    h = (y * (1.0 + sc_ref[0]) + sh_ref[0]).astype(BF16)
    cos = cos_ref[...]
    sin = sin_ref[...]
    cw = INPROJ_CHUNK

    def chunk(c0):
        na = wa_ref.shape[1]
        w = wa_ref[:, c0:c0 + cw] if c0 < na else wb_ref[:, c0 - na:c0 - na + cw]
        return jnp.dot(h, w, preferred_element_type=F32)

    def rotary(c0, scale):
        acc = chunk(c0)
        for hh in range(cw // RET_DK):
            a = acc[:, hh * RET_DK:(hh + 1) * RET_DK]
            o = a * cos + pltpu.roll(a, RET_DK // 2, axis=1) * sin
            if scale is not None:
                o = o * scale
            z_ref[:, c0 + hh * RET_DK:c0 + (hh + 1) * RET_DK] = o.astype(BF16)

    def group(name, nxt, fn):
        for c0 in range(Z_COLS[name], Z_COLS[nxt], cw):
            z_ref[:, c0:c0 + cw] = fn(chunk(c0)).astype(BF16)

    rotary(Z_COLS["rq"], None)
    rotary(Z_COLS["rk"], rk_scale)
    group("rv", "rg", lambda a: a)
    group("rg", "fq", lambda a: a * jax.nn.sigmoid(a))
    group("fq", "fk", lambda a: a * fox_scale)
    group("fk", "gr", lambda a: a)
    group("gr", "end", jax.nn.sigmoid)
    for r0 in range(0, wvt_ref.shape[0], cw):
        vt_ref[0, r0:r0 + cw, :] = lax.dot_general(
            wvt_ref[r0:r0 + cw, :], h, _NT, preferred_element_type=F32).astype(BF16)

    xv = jnp.dot(h, wff_ref[...], preferred_element_type=F32) + bf_ref[...]
    lf = jnp.minimum(xv, 0.0) - jnp.log(1.0 + jnp.exp(-jnp.abs(xv)))
    tri = tri_ref[...]
    p0, p1, p2 = _split3(lf)
    cs = (jnp.dot(tri, p0, preferred_element_type=F32)
          + jnp.dot(tri, p1, preferred_element_type=F32)
          + jnp.dot(tri, p2, preferred_element_type=F32))
    first = (pl.program_id(0) % spb) == 0
    fc = cs + jnp.where(first, 0.0, carry[...])
    tm = fc.shape[0]
    carry[...] = fc[tm - 1:tm, :]
    f0, f1, f2 = [p.astype(F32) for p in _split3(fc * LOG2E)]
    c = lax.broadcasted_iota(jnp.int32, fc.shape, 1) % AUG_STRIDE
    qa_ref[0] = jnp.where(c == 0, f0, jnp.where(c == 1, f1, jnp.where(
        c == 2, f2, jnp.where(c < 6, 1.0, 0.0)))).astype(BF16)
    ka_ref[0] = jnp.where(c < 3, 1.0, jnp.where(c == 3, -f0, jnp.where(
        c == 4, -f1, jnp.where(c == 5, -f2, 0.0)))).astype(BF16)


def _inproj(x2, nw, sc, sh, cos_t, sin_t, wa, wb, wvt, wff, bf_l, B, S):
    T, D = x2.shape
    NA, NB = wa.shape[1], wb.shape[1]
    N = NA + NB
    assert N == Z_COLS["end"] and NA % INPROJ_CHUNK == 0
    NV = wvt.shape[0]
    tm = 512
    spb = S // tm
    tri = jnp.asarray(np.tril(np.ones((tm, tm), np.float32)), BF16)
    kern = functools.partial(_inproj_kernel, fox_scale=LOG2E / math.sqrt(FOX_DH),
                             rk_scale=RET_DK ** -0.5, spb=spb)
    const = lambda shape: pl.BlockSpec(shape, lambda i: (0,) * len(shape),
                                       pipeline_mode=pl.Buffered(1))
    return pl.pallas_call(
        kern,
        out_shape=(jax.ShapeDtypeStruct((T, N), BF16),
                   jax.ShapeDtypeStruct((B, NV, S), BF16),
                   jax.ShapeDtypeStruct((B, S, LANES), BF16),
                   jax.ShapeDtypeStruct((B, S, LANES), BF16)),
        grid=(T // tm,),
        in_specs=[pl.BlockSpec((tm, D), lambda i: (i, 0)),
                  const((1, D)),
                  pl.BlockSpec((1, 1, D), lambda i: (i // spb, 0, 0)),
                  pl.BlockSpec((1, 1, D), lambda i: (i // spb, 0, 0)),
                  pl.BlockSpec((tm, LANES), lambda i: (i % spb, 0)),
                  pl.BlockSpec((tm, LANES), lambda i: (i % spb, 0)),
                  const((D, NA)),
                  const((D, NB)),
                  const((NV, D)),
                  const((D, LANES)),
                  const((1, LANES)),
                  const((tm, tm))],
        out_specs=(pl.BlockSpec((tm, N), lambda i: (i, 0)),
                   pl.BlockSpec((1, NV, tm), lambda i: (i // spb, 0, i % spb)),
                   pl.BlockSpec((1, tm, LANES), lambda i: (i // spb, i % spb, 0)),
                   pl.BlockSpec((1, tm, LANES), lambda i: (i // spb, i % spb, 0))),
        scratch_shapes=[pltpu.VMEM((1, LANES), F32)],
        compiler_params=_cparams(("arbitrary",), 56),
        name="inproj",
    )(x2, nw, sc, sh, cos_t, sin_t, wa, wb, wvt, wff, bf_l, tri)


def _ret_consts(L):
    hs = np.arange(RET_HEADS, dtype=np.float64)
    log_gamma = np.log(1.0 - np.exp2(-5.0 - hs))
    idx = np.arange(L, dtype=np.float64)
    dist = np.abs(idx[:, None] - idx[None, :])
    chunk_ok = (idx[None, :] // RET_CHUNK) <= (idx[:, None] // RET_CHUNK)
    dmask = np.exp(log_gamma[:, None, None] * dist[None]) * chunk_ok[None]
    qdec = np.exp(log_gamma[:, None] * idx[None, :])[..., None]
    kdec = np.exp(log_gamma[:, None] * (L - idx)[None, :])[..., None]
    bdec = [float(v) for v in np.exp(log_gamma * L)]
    return (jnp.asarray(dmask, F32), jnp.asarray(qdec, F32), jnp.asarray(kdec, F32), bdec)


def _ret_kernel(rq_ref, rk_ref, rv_ref, rg_ref, gr_ref, dm_ref, qd_ref, kd_ref, wr_ref,
                o_ref, st_scr, y_scr, *, bdec):
    @pl.when(pl.program_id(1) == 0)
    def _():
        st_scr[...] = jnp.zeros_like(st_scr)

    for h in range(RET_HEADS):
        q = rq_ref[0, :, h * RET_DK:(h + 1) * RET_DK]
        k = rk_ref[0, :, h * RET_DK:(h + 1) * RET_DK]
        v = rv_ref[0, :, h * RET_DV:(h + 1) * RET_DV]
        s = lax.dot_general(q, k, _NT, preferred_element_type=F32) * dm_ref[h]
        intra = jnp.dot(s.astype(BF16), v, preferred_element_type=F32)
        st = st_scr[h]
        inter = jnp.dot(q, st.astype(BF16), preferred_element_type=F32) * qd_ref[h]
        ks = (k.astype(F32) * kd_ref[h]).astype(BF16)
        st_scr[h] = bdec[h] * st + lax.dot_general(ks, v, _TN, preferred_element_type=F32)
        ro = intra + inter
        ron = ro * lax.rsqrt(jnp.mean(ro * ro, axis=-1, keepdims=True) + EPS)
        g = rg_ref[0, :, h * RET_DV:(h + 1) * RET_DV].astype(F32)
        y_scr[:, h * RET_DV:(h + 1) * RET_DV] = (g * ron).astype(BF16)

    proj = jnp.dot(y_scr[...], wr_ref[...], preferred_element_type=F32)
    o_ref[0] = (gr_ref[0].astype(F32) * proj).astype(BF16)


def _retention(z3, wr, D):
    B, S, _ = z3.shape
    L = 256
    dmask, qdec, kdec, bdec = _ret_consts(L)
    QK = RET_HEADS * RET_DK
    V = RET_HEADS * RET_DV
    kern = functools.partial(_ret_kernel, bdec=bdec)
    full3 = lambda b, i: (0, 0, 0)
    return pl.pallas_call(
        kern,
        out_shape=jax.ShapeDtypeStruct((B, S, D), BF16),
        grid=(B, S // L),
        in_specs=[pl.BlockSpec((1, L, QK), lambda b, i: (b, i, 0)),
                  pl.BlockSpec((1, L, QK), lambda b, i: (b, i, 1)),
                  pl.BlockSpec((1, L, V), lambda b, i: (b, i, 1)),
                  pl.BlockSpec((1, L, V), lambda b, i: (b, i, 2)),
                  pl.BlockSpec((1, L, D), lambda b, i: (b, i, 5)),
                  pl.BlockSpec((RET_HEADS, L, L), full3),
                  pl.BlockSpec((RET_HEADS, L, 1), full3),
                  pl.BlockSpec((RET_HEADS, L, 1), full3),
                  pl.BlockSpec((V, D), lambda b, i: (0, 0))],
        out_specs=pl.BlockSpec((1, L, D), lambda b, i: (b, i, 0)),
        scratch_shapes=[pltpu.VMEM((RET_HEADS, RET_DK, RET_DV), F32),
                        pltpu.VMEM((L, V), BF16)],
        compiler_params=_cparams(("arbitrary", "arbitrary"), 40),
        name="ret",
    )(z3, z3, z3, z3, z3, dmask, qdec, kdec, wr)


FOX_CHUNK = 128


def _fox_kernel(q_ref, qa_ref, k_ref, ka_ref, vt_ref, o_ref,
                kx_scr, vt_scr, qxt_scr, sa_scr, sb_scr, mba_scr, mbb_scr, pa_scr, pb_scr,
                m_scr, al_scr, acc_scr, *, tq, tk):
    h = pl.program_id(1)
    i = pl.program_id(2)
    nkb = vt_scr.shape[0]

    @pl.when(i == 0)
    def _():
        kx_scr[:, :FOX_DH] = k_ref[0]
        kx_scr[:, FOX_DH:] = ka_ref[0]
        for n in range(nkb):
            vt_scr[n, :FOX_DH, :] = vt_ref[0, :, n * tk:(n + 1) * tk]
            vt_scr[n, FOX_DH:, :] = jnp.ones((vt_scr.shape[1] - FOX_DH, tk), BF16)

    lane = lax.broadcasted_iota(jnp.int32, (tq, LANES), 1)
    qa = jnp.where((lane // AUG_STRIDE) == h, qa_ref[0].astype(F32), 0.0)
    qxt_scr[:FOX_DH, :] = q_ref[0].astype(F32).T.astype(BF16)
    qxt_scr[FOX_DH:, :] = qa.T.astype(BF16)
    m_scr[...] = jnp.full(m_scr.shape, NEG_BIG, F32)
    al_scr[...] = jnp.ones_like(al_scr)
    acc_scr[...] = jnp.zeros_like(acc_scr)
    pb_scr[...] = jnp.zeros_like(pb_scr)

    def scores(blk, s_out, mb_out, cs):
        off = pl.multiple_of(blk * tk, tk)
        s = jnp.dot(kx_scr[pl.ds(off, tk), :], qxt_scr[:, cs],
                    preferred_element_type=F32)
        s_out[:, cs] = s
        if mb_out is not None:
            mb_out[:, cs] = jnp.max(s, axis=0, keepdims=True)

    def softmax(s_in, mb_in, p_out, diag_shift, cs):
        s = s_in[:, cs]
        if diag_shift is None:
            mb = mb_in[:, cs]
        else:
            key = lax.broadcasted_iota(jnp.int32, s.shape, 0) + diag_shift
            qry = lax.broadcasted_iota(jnp.int32, s.shape, 1) + cs.start
            s = jnp.where(key <= qry, s, NEG_BIG)
            mb = jnp.max(s, axis=0, keepdims=True)
        m_prev = m_scr[:, cs]
        m_new = jnp.maximum(m_prev, mb)
        p_out[:, cs] = jnp.exp2((s - m_new).astype(BF16))
        m_scr[:, cs] = m_new
        return jnp.exp2(m_prev - m_new)

    def pv(blk_prev, p_in, cs):
        acc_scr[:, cs] = al_scr[:, cs] * acc_scr[:, cs] + jnp.dot(
            vt_scr[blk_prev], p_in[:, cs], preferred_element_type=F32)

    chunks = [slice(c0, c0 + FOX_CHUNK) for c0 in range(0, tq, FOX_CHUNK)]

    def step(n, cur, nxt, p_prv, diag_shift):
        s_cur, mb_cur, p_cur = cur
        for cs in chunks:
            if nxt is not None:
                scores(n + 1, nxt[0], nxt[1], cs)
            alpha = softmax(s_cur, mb_cur, p_cur, diag_shift, cs)
            pv(jnp.maximum(n - 1, 0), p_prv, cs)
            al_scr[:, cs] = alpha

    buf_a = (sa_scr, mba_scr, pa_scr)
    buf_b = (sb_scr, mbb_scr, pb_scr)

    def body(jj, carry):
        step(2 * jj, buf_a, buf_b, pb_scr, None)
        step(2 * jj + 1, buf_b, buf_a, pa_scr, None)
        return carry

    for cs in chunks:
        scores(0, sa_scr, mba_scr, cs)
    lax.fori_loop(0, i, body, 0)
    step(2 * i, buf_a, (sb_scr, None), pb_scr, 0)
    step(2 * i + 1, buf_b, None, pa_scr, tk)
    for cs in chunks:
        pv(2 * i + 1, pb_scr, cs)
    acc = acc_scr[...]
    o_ref[0] = (acc[:FOX_DH] / acc[FOX_DH:FOX_DH + 1]).T.astype(BF16)


def _fox(z3, qa, ka, vt):
    B, S, _ = z3.shape
    tq, tk = 1024, 512
    assert tq == 2 * tk
    ones_rows = 16
    W = FOX_HEADS * FOX_DH
    qb, kb = Z_COLS["fq"] // FOX_DH, Z_COLS["fk"] // FOX_DH
    kern = functools.partial(_fox_kernel, tq=tq, tk=tk)
    return pl.pallas_call(
        kern,
        out_shape=jax.ShapeDtypeStruct((B, S, W), BF16),
        grid=(B, FOX_HEADS, S // tq),
        in_specs=[pl.BlockSpec((1, tq, FOX_DH), lambda b, h, i: (b, i, qb + h)),
                  pl.BlockSpec((1, tq, LANES), lambda b, h, i: (b, i, 0)),
                  pl.BlockSpec((1, S, FOX_DH), lambda b, h, i: (b, 0, kb + h)),
                  pl.BlockSpec((1, S, LANES), lambda b, h, i: (b, 0, 0)),
                  pl.BlockSpec((1, FOX_DH, S), lambda b, h, i: (b, h, 0))],
        out_specs=pl.BlockSpec((1, tq, FOX_DH), lambda b, h, i: (b, i, h)),
        scratch_shapes=[pltpu.VMEM((S, 2 * FOX_DH), BF16),
                        pltpu.VMEM((S // tk, FOX_DH + ones_rows, tk), BF16),
                        pltpu.VMEM((2 * FOX_DH, tq), BF16),
                        pltpu.VMEM((tk, tq), F32),
                        pltpu.VMEM((tk, tq), F32),
                        pltpu.VMEM((1, tq), F32),
                        pltpu.VMEM((1, tq), F32),
                        pltpu.VMEM((tk, tq), BF16),
                        pltpu.VMEM((tk, tq), BF16),
                        pltpu.VMEM((1, tq), F32),
                        pltpu.VMEM((1, tq), F32),
                        pltpu.VMEM((FOX_DH + ones_rows, tq), F32)],
        compiler_params=_cparams(("arbitrary", "arbitrary", "arbitrary"), 48),
        name="fox",
    )(z3, qa, z3, ka, vt)


FFN_CHUNK = 256


def _tail_kernel(a_ref, yf_ref, gf_ref, x_ref, g1_ref, sc_ref, sh_ref, g2_ref, n2_ref, nf_ref,
                 wf_ref, wo_ref, wg_ref, wu_ref, wd_ref, o_ref, act_scr):
    fo = jnp.dot(yf_ref[...], wf_ref[...], preferred_element_type=F32)
    merged = a_ref[...].astype(F32) + gf_ref[...].astype(F32) * fo
    out = jnp.dot(merged.astype(BF16), wo_ref[...], preferred_element_type=F32)
    x1 = x_ref[...] + g1_ref[0] * out

    ms = jnp.mean(x1 * x1, axis=-1, keepdims=True)
    y = x1 * lax.rsqrt(ms + EPS) * n2_ref[...]
    h = (y * (1.0 + sc_ref[0]) + sh_ref[0]).astype(BF16)
    for c0 in range(0, wg_ref.shape[1], FFN_CHUNK):
        g = jnp.dot(h, wg_ref[:, c0:c0 + FFN_CHUNK], preferred_element_type=F32)
        u = jnp.dot(h, wu_ref[:, c0:c0 + FFN_CHUNK], preferred_element_type=F32)
        act_scr[:, c0:c0 + FFN_CHUNK] = (g * jax.nn.sigmoid(g) * u).astype(BF16)
    ffn = jnp.dot(act_scr[...], wd_ref[...], preferred_element_type=F32)

    x2 = x1 + g2_ref[0] * ffn
    ms2 = jnp.mean(x2 * x2, axis=-1, keepdims=True)
    o_ref[...] = x2 * lax.rsqrt(ms2 + EPS) * nf_ref[...]


def _tail(a2, yf2, z2, x2, g1, sc2, sh2, g2, n2, nf, wf, wo, wg, wu, wd, S):
    T, D = x2.shape
    FF = wg.shape[1]
    assert FF % FFN_CHUNK == 0
    tm = 512
    spb = S // tm
    row = lambda i: (i, 0)
    bidx = lambda i: (i // spb, 0, 0)
    const = lambda shape: pl.BlockSpec(shape, lambda i: (0,) * len(shape),
                                       pipeline_mode=pl.Buffered(1))
    return pl.pallas_call(
        _tail_kernel,
        out_shape=jax.ShapeDtypeStruct((T, D), F32),
        grid=(T // tm,),
        in_specs=[pl.BlockSpec((tm, D), row),
                  pl.BlockSpec((tm, D), row),
                  pl.BlockSpec((tm, D), lambda i: (i, Z_COLS["gf"] // D)),
                  pl.BlockSpec((tm, D), row),
                  pl.BlockSpec((1, 1, D), bidx),
                  pl.BlockSpec((1, 1, D), bidx),
                  pl.BlockSpec((1, 1, D), bidx),
                  pl.BlockSpec((1, 1, D), bidx),
                  const((1, D)),
                  const((1, D)),
                  const((D, D)),
                  const((D, D)),
                  const((D, FF)),
                  const((D, FF)),
                  const((FF, D))],
        out_specs=pl.BlockSpec((tm, D), row),
        scratch_shapes=[pltpu.VMEM((tm, FF), BF16)],
        compiler_params=_cparams(("arbitrary",), 58),
        name="tail",
    )(a2, yf2, z2, x2, g1, sc2, sh2, g2, n2, nf, wf, wo, wg, wu, wd)


def _rope_tables(S):
    half = RET_DK // 2
    freqs = ROPE_THETA ** (-np.arange(half, dtype=np.float64) / half)
    ang = np.arange(S, dtype=np.float64)[:, None] * freqs[None, :]
    cos, sin = np.cos(ang), np.sin(ang)
    return (jnp.asarray(np.concatenate([cos, cos], axis=1), F32),
            jnp.asarray(np.concatenate([-sin, sin], axis=1), F32))


def _aug_lanes(v8):
    rep = jnp.repeat(v8[..., None], 6, axis=-1)
    pad = jnp.zeros(v8.shape + (AUG_STRIDE - 6,), v8.dtype)
    return jnp.concatenate([rep, pad], axis=-1).reshape(v8.shape[:-1] + (LANES,))


def _layer(x, mod, norm1_w, w_in, b_f, ret_proj, fox_proj, w_out, norm2_w,
           w_gate, w_up, w_down, norm_out_w, tables):
    B, S, D = x.shape
    T = B * S
    sh1, sc1, g1, sh2, sc2, g2 = [m.reshape(B, 1, D) for m in jnp.split(mod, 6, axis=-1)]
    offs = np.cumsum([0, RET_HEADS * RET_DK, RET_HEADS * RET_DK, RET_HEADS * RET_DV,
                      RET_HEADS * RET_DV, FOX_HEADS * FOX_DH, FOX_HEADS * FOX_DH,
                      FOX_HEADS * FOX_DH, FOX_HEADS, D, D])
    o_fv, o_ff, o_gr = int(offs[6]), int(offs[7]), int(offs[8])
    w_a = w_in[:, :o_fv].astype(BF16)
    w_b = w_in[:, o_gr:].astype(BF16)
    w_vt = w_in[:, o_fv:o_ff].T.astype(BF16)
    w_ff = _aug_lanes(w_in[:, o_ff:o_gr]).astype(BF16)
    bf_l = _aug_lanes(b_f).reshape(1, LANES)

    x2 = x.reshape(T, D)
    z, vt, qa, ka = _inproj(x2, norm1_w.reshape(1, D), sc1, sh1, tables[0], tables[1],
                            w_a, w_b, w_vt, w_ff, bf_l, B, S)
    z3 = z.reshape(B, S, z.shape[1])
    a = _retention(z3, ret_proj.astype(BF16), D)
    yf = _fox(z3, qa, ka, vt)
    out = _tail(a.reshape(T, D), yf.reshape(T, D), z, x2, g1, sc2, sh2, g2,
                norm2_w.reshape(1, D), norm_out_w.reshape(1, D),
                fox_proj.astype(BF16), w_out.astype(BF16), w_gate.astype(BF16),
                w_up.astype(BF16), w_down.astype(BF16), S)
    return out.reshape(B, S, D)


def kernel(x, c, ada_w, ada_b, norm1_w, w_in, b_f, ret_proj, fox_proj, w_out,
           norm2_w, w_gate, w_up, w_down, norm_f_w):
    depth = ada_w.shape[0]
    assert depth == 1, "the final RMSNorm is fused into the last layer's channel mixer"
    tables = _rope_tables(x.shape[1])
    l = 0
    mod = _ada(c, ada_w[l], ada_b[l])
    return _layer(x, mod, norm1_w[l], w_in[l], b_f[l], ret_proj[l], fox_proj[l], w_out[l],
                  norm2_w[l], w_gate[l], w_up[l], w_down[l], norm_f_w, tables)
```

```python
import functools
import math

import numpy as np
import jax
import jax.numpy as jnp
from jax import lax
from jax.experimental import pallas as pl
from jax.experimental.pallas import tpu as pltpu

F32 = jnp.float32
BF16 = jnp.bfloat16

EPS = 1e-6
ROPE_THETA = 10000.0
RET_HEADS = 4
RET_DK = 128
RET_DV = 256
FOX_HEADS = 8
FOX_DH = 128
RET_CHUNK = 64
LANES = 128
AUG_STRIDE = 16

NEG_BIG = -1e30
LOG2E = math.log2(math.e)

_NT = (((1,), (1,)), ((), ()))
_TN = (((0,), (0,)), ((), ()))


def _cparams(sem, vmem_mb, flags=None):
    return pltpu.CompilerParams(dimension_semantics=sem,
                                vmem_limit_bytes=vmem_mb * 1024 * 1024, flags=flags)


def _ada_kernel(ct_ref, w_ref, b_ref, o_ref):
    ct = ct_ref[...]
    act = ct * jax.nn.sigmoid(ct)
    w = w_ref[...]
    for b in range(o_ref.shape[0]):
        o_ref[b:b + 1, :] = (jnp.sum(act[:, b:b + 1] * w, axis=0, keepdims=True)
                             + b_ref[...])


def _ada(c, w, b):
    B, D = c.shape
    N = w.shape[1]
    tn = 1536
    return pl.pallas_call(
        _ada_kernel,
        out_shape=jax.ShapeDtypeStruct((B, N), F32),
        grid=(N // tn,),
        in_specs=[pl.BlockSpec((D, B), lambda j: (0, 0)),
                  pl.BlockSpec((D, tn), lambda j: (0, j)),
                  pl.BlockSpec((1, tn), lambda j: (0, j))],
        out_specs=pl.BlockSpec((B, tn), lambda j: (0, j)),
        compiler_params=_cparams(("arbitrary",), 40),
        name="ada",
    )(c.T, w, b.reshape(1, N))


INPROJ_CHUNK = 512
Z_COLS = {"rq": 0, "rk": 512, "rv": 1024, "rg": 2048, "fq": 3072, "fk": 4096,
          "gr": 5120, "gf": 6144, "end": 7168}


def _split3(v):
    p0 = v.astype(BF16)
    r1 = v - p0.astype(F32)
    p1 = r1.astype(BF16)
    r2 = r1 - p1.astype(F32)
    p2 = r2.astype(BF16)
    return p0, p1, p2


def _inproj_kernel(x_ref, nw_ref, sc_ref, sh_ref, cos_ref, sin_ref, wa_ref, wb_ref, wvt_ref,
                   wff_ref, bf_ref, tri_ref, z_ref, vt_ref, qa_ref, ka_ref, carry,
                   *, fox_scale, rk_scale, spb):
    x = x_ref[...]
    ms = jnp.mean(x * x, axis=-1, keepdims=True)
    y = x * lax.rsqrt(ms + EPS) * nw_ref[...]
    h = (y * (1.0 + sc_ref[0]) + sh_ref[0]).astype(BF16)
    cos = cos_ref[...]
    sin = sin_ref[...]
    cw = INPROJ_CHUNK

    def chunk(c0):
        na = wa_ref.shape[1]
        w = wa_ref[:, c0:c0 + cw] if c0 < na else wb_ref[:, c0 - na:c0 - na + cw]
        return jnp.dot(h, w, preferred_element_type=F32)

    def rotary(c0, scale):
        acc = chunk(c0)
        for hh in range(cw // RET_DK):
            a = acc[:, hh * RET_DK:(hh + 1) * RET_DK]
            o = a * cos + pltpu.roll(a, RET_DK // 2, axis=1) * sin
            if scale is not None:
                o = o * scale
            z_ref[:, c0 + hh * RET_DK:c0 + (hh + 1) * RET_DK] = o.astype(BF16)

    def group(name, nxt, fn):
        for c0 in range(Z_COLS[name], Z_COLS[nxt], cw):
            z_ref[:, c0:c0 + cw] = fn(chunk(c0)).astype(BF16)

    rotary(Z_COLS["rq"], None)
    rotary(Z_COLS["rk"], rk_scale)
    group("rv", "rg", lambda a: a)
    group("rg", "fq", lambda a: a * jax.nn.sigmoid(a))
    group("fq", "fk", lambda a: a * fox_scale)
    group("fk", "gr", lambda a: a)
    group("gr", "end", jax.nn.sigmoid)
    for r0 in range(0, wvt_ref.shape[0], cw):
        vt_ref[0, r0:r0 + cw, :] = lax.dot_general(
            wvt_ref[r0:r0 + cw, :], h, _NT, preferred_element_type=F32).astype(BF16)

    xv = jnp.dot(h, wff_ref[...], preferred_element_type=F32) + bf_ref[...]
    lf = jnp.minimum(xv, 0.0) - jnp.log(1.0 + jnp.exp(-jnp.abs(xv)))
    tri = tri_ref[...]
    p0, p1, p2 = _split3(lf)
    cs = (jnp.dot(tri, p0, preferred_element_type=F32)
          + jnp.dot(tri, p1, preferred_element_type=F32)
          + jnp.dot(tri, p2, preferred_element_type=F32))
    first = (pl.program_id(0) % spb) == 0
    fc = cs + jnp.where(first, 0.0, carry[...])
    tm = fc.shape[0]
    carry[...] = fc[tm - 1:tm, :]
    f0, f1, f2 = [p.astype(F32) for p in _split3(fc * LOG2E)]
    c = lax.broadcasted_iota(jnp.int32, fc.shape, 1) % AUG_STRIDE
    qa_ref[0] = jnp.where(c == 0, f0, jnp.where(c == 1, f1, jnp.where(
        c == 2, f2, jnp.where(c < 6, 1.0, 0.0)))).astype(BF16)
    ka_ref[0] = jnp.where(c < 3, 1.0, jnp.where(c == 3, -f0, jnp.where(
        c == 4, -f1, jnp.where(c == 5, -f2, 0.0)))).astype(BF16)


def _inproj(x2, nw, sc, sh, cos_t, sin_t, wa, wb, wvt, wff, bf_l, B, S):
    T, D = x2.shape
    NA, NB = wa.shape[1], wb.shape[1]
    N = NA + NB
    assert N == Z_COLS["end"] and NA % INPROJ_CHUNK == 0
    NV = wvt.shape[0]
    tm = 512
    spb = S // tm
    tri = jnp.asarray(np.tril(np.ones((tm, tm), np.float32)), BF16)
    kern = functools.partial(_inproj_kernel, fox_scale=LOG2E / math.sqrt(FOX_DH),
                             rk_scale=RET_DK ** -0.5, spb=spb)
    const = lambda shape: pl.BlockSpec(shape, lambda i: (0,) * len(shape),
                                       pipeline_mode=pl.Buffered(1))
    return pl.pallas_call(
        kern,
        out_shape=(jax.ShapeDtypeStruct((T, N), BF16),
                   jax.ShapeDtypeStruct((B, NV, S), BF16),
                   jax.ShapeDtypeStruct((B, S, LANES), BF16),
                   jax.ShapeDtypeStruct((B, S, LANES), BF16)),
        grid=(T // tm,),
        in_specs=[pl.BlockSpec((tm, D), lambda i: (i, 0)),
                  const((1, D)),
                  pl.BlockSpec((1, 1, D), lambda i: (i // spb, 0, 0)),
                  pl.BlockSpec((1, 1, D), lambda i: (i // spb, 0, 0)),
                  pl.BlockSpec((tm, LANES), lambda i: (i % spb, 0)),
                  pl.BlockSpec((tm, LANES), lambda i: (i % spb, 0)),
                  const((D, NA)),
                  const((D, NB)),
                  const((NV, D)),
                  const((D, LANES)),
                  const((1, LANES)),
                  const((tm, tm))],
        out_specs=(pl.BlockSpec((tm, N), lambda i: (i, 0)),
                   pl.BlockSpec((1, NV, tm), lambda i: (i // spb, 0, i % spb)),
                   pl.BlockSpec((1, tm, LANES), lambda i: (i // spb, i % spb, 0)),
                   pl.BlockSpec((1, tm, LANES), lambda i: (i // spb, i % spb, 0))),
        scratch_shapes=[pltpu.VMEM((1, LANES), F32)],
        compiler_params=_cparams(("arbitrary",), 56),
        name="inproj",
    )(x2, nw, sc, sh, cos_t, sin_t, wa, wb, wvt, wff, bf_l, tri)


def _ret_consts(L):
    hs = np.arange(RET_HEADS, dtype=np.float64)
    log_gamma = np.log(1.0 - np.exp2(-5.0 - hs))
    idx = np.arange(L, dtype=np.float64)
    dist = np.abs(idx[:, None] - idx[None, :])
    chunk_ok = (idx[None, :] // RET_CHUNK) <= (idx[:, None] // RET_CHUNK)
    dmask = np.exp(log_gamma[:, None, None] * dist[None]) * chunk_ok[None]
    qdec = np.exp(log_gamma[:, None] * idx[None, :])[..., None]
    kdec = np.exp(log_gamma[:, None] * (L - idx)[None, :])[..., None]
    bdec = [float(v) for v in np.exp(log_gamma * L)]
    return (jnp.asarray(dmask, F32), jnp.asarray(qdec, F32), jnp.asarray(kdec, F32), bdec)


def _ret_kernel(rq_ref, rk_ref, rv_ref, rg_ref, gr_ref, dm_ref, qd_ref, kd_ref, wr_ref,
                o_ref, st_scr, y_scr, *, bdec):
    @pl.when(pl.program_id(1) == 0)
    def _():
        st_scr[...] = jnp.zeros_like(st_scr)

    for h in range(RET_HEADS):
        q = rq_ref[0, :, h * RET_DK:(h + 1) * RET_DK]
        k = rk_ref[0, :, h * RET_DK:(h + 1) * RET_DK]
        v = rv_ref[0, :, h * RET_DV:(h + 1) * RET_DV]
        s = lax.dot_general(q, k, _NT, preferred_element_type=F32) * dm_ref[h]
        intra = jnp.dot(s.astype(BF16), v, preferred_element_type=F32)
        st = st_scr[h]
        inter = jnp.dot(q, st.astype(BF16), preferred_element_type=F32) * qd_ref[h]
        ks = (k.astype(F32) * kd_ref[h]).astype(BF16)
        st_scr[h] = bdec[h] * st + lax.dot_general(ks, v, _TN, preferred_element_type=F32)
        ro = intra + inter
        ron = ro * lax.rsqrt(jnp.mean(ro * ro, axis=-1, keepdims=True) + EPS)
        g = rg_ref[0, :, h * RET_DV:(h + 1) * RET_DV].astype(F32)
        y_scr[:, h * RET_DV:(h + 1) * RET_DV] = (g * ron).astype(BF16)

    proj = jnp.dot(y_scr[...], wr_ref[...], preferred_element_type=F32)
    o_ref[0] = (gr_ref[0].astype(F32) * proj).astype(BF16)


def _retention(z3, wr, D):
    B, S, _ = z3.shape
    L = 256
    dmask, qdec, kdec, bdec = _ret_consts(L)
    QK = RET_HEADS * RET_DK
    V = RET_HEADS * RET_DV
    kern = functools.partial(_ret_kernel, bdec=bdec)
    full3 = lambda b, i: (0, 0, 0)
    return pl.pallas_call(
        kern,
        out_shape=jax.ShapeDtypeStruct((B, S, D), BF16),
        grid=(B, S // L),
        in_specs=[pl.BlockSpec((1, L, QK), lambda b, i: (b, i, 0)),
                  pl.BlockSpec((1, L, QK), lambda b, i: (b, i, 1)),
                  pl.BlockSpec((1, L, V), lambda b, i: (b, i, 1)),
                  pl.BlockSpec((1, L, V), lambda b, i: (b, i, 2)),
                  pl.BlockSpec((1, L, D), lambda b, i: (b, i, 5)),
                  pl.BlockSpec((RET_HEADS, L, L), full3),
                  pl.BlockSpec((RET_HEADS, L, 1), full3),
                  pl.BlockSpec((RET_HEADS, L, 1), full3),
                  pl.BlockSpec((V, D), lambda b, i: (0, 0))],
        out_specs=pl.BlockSpec((1, L, D), lambda b, i: (b, i, 0)),
        scratch_shapes=[pltpu.VMEM((RET_HEADS, RET_DK, RET_DV), F32),
                        pltpu.VMEM((L, V), BF16)],
        compiler_params=_cparams(("arbitrary", "arbitrary"), 40),
        name="ret",
    )(z3, z3, z3, z3, z3, dmask, qdec, kdec, wr)


FOX_CHUNK = 256


def _fox_kernel(q_ref, qa_ref, k_ref, ka_ref, vt_ref, o_ref,
                kx_scr, vt_scr, qxt_scr, sa_scr, sb_scr, mba_scr, mbb_scr, pa_scr, pb_scr,
                m_scr, al_scr, acc_scr, *, tq, tk):
    h = pl.program_id(1)
    i = pl.program_id(2)
    nkb = vt_scr.shape[0]

    @pl.when(i == 0)
    def _():
        kx_scr[:, :FOX_DH] = k_ref[0]
        kx_scr[:, FOX_DH:] = ka_ref[0]
        for n in range(nkb):
            vt_scr[n, :FOX_DH, :] = vt_ref[0, :, n * tk:(n + 1) * tk]
            vt_scr[n, FOX_DH:, :] = jnp.ones((vt_scr.shape[1] - FOX_DH, tk), BF16)

    lane = lax.broadcasted_iota(jnp.int32, (tq, LANES), 1)
    qa = jnp.where((lane // AUG_STRIDE) == h, qa_ref[0].astype(F32), 0.0)
    qxt_scr[:FOX_DH, :] = q_ref[0].astype(F32).T.astype(BF16)
    qxt_scr[FOX_DH:, :] = qa.T.astype(BF16)
    m_scr[...] = jnp.full(m_scr.shape, NEG_BIG, F32)
    al_scr[...] = jnp.ones_like(al_scr)
    acc_scr[...] = jnp.zeros_like(acc_scr)
    pb_scr[...] = jnp.zeros_like(pb_scr)

    def scores(blk, s_out, mb_out, cs):
        off = pl.multiple_of(blk * tk, tk)
        s = jnp.dot(kx_scr[pl.ds(off, tk), :], qxt_scr[:, cs],
                    preferred_element_type=F32)
        s_out[:, cs] = s
        if mb_out is not None:
            mb_out[:, cs] = jnp.max(s, axis=0, keepdims=True)

    def softmax(s_in, mb_in, p_out, diag_shift, cs):
        s = s_in[:, cs]
        if diag_shift is None:
            mb = mb_in[:, cs]
        else:
            key = lax.broadcasted_iota(jnp.int32, s.shape, 0) + diag_shift
            qry = lax.broadcasted_iota(jnp.int32, s.shape, 1) + cs.start
            s = jnp.where(key <= qry, s, NEG_BIG)
            mb = jnp.max(s, axis=0, keepdims=True)
        m_prev = m_scr[:, cs]
        m_new = jnp.maximum(m_prev, mb)
        p_out[:, cs] = jnp.exp2((s - m_new).astype(BF16))
        m_scr[:, cs] = m_new
        return jnp.exp2(m_prev - m_new)

    def pv(blk_prev, p_in, cs):
        acc_scr[:, cs] = al_scr[:, cs] * acc_scr[:, cs] + jnp.dot(
            vt_scr[blk_prev], p_in[:, cs], preferred_element_type=F32)

    chunks = [slice(c0, c0 + FOX_CHUNK) for c0 in range(0, tq, FOX_CHUNK)]

    def step(n, cur, nxt, p_prv, diag_shift):
        s_cur, mb_cur, p_cur = cur
        for cs in chunks:
            if nxt is not None:
                scores(n + 1, nxt[0], nxt[1], cs)
            alpha = softmax(s_cur, mb_cur, p_cur, diag_shift, cs)
            pv(jnp.maximum(n - 1, 0), p_prv, cs)
            al_scr[:, cs] = alpha

    buf_a = (sa_scr, mba_scr, pa_scr)
    buf_b = (sb_scr, mbb_scr, pb_scr)

    def body(jj, carry):
        step(2 * jj, buf_a, buf_b, pb_scr, None)
        step(2 * jj + 1, buf_b, buf_a, pa_scr, None)
        return carry

    for cs in chunks:
        scores(0, sa_scr, mba_scr, cs)
    lax.fori_loop(0, i, body, 0)
    step(2 * i, buf_a, (sb_scr, None), pb_scr, 0)
    step(2 * i + 1, buf_b, None, pa_scr, tk)
    for cs in chunks:
        pv(2 * i + 1, pb_scr, cs)
    acc = acc_scr[...]
    o_ref[0] = (acc[:FOX_DH] / acc[FOX_DH:FOX_DH + 1]).T.astype(BF16)


def _fox(z3, qa, ka, vt):
    B, S, _ = z3.shape
    tq, tk = 1024, 512
    assert tq == 2 * tk
    ones_rows = 16
    W = FOX_HEADS * FOX_DH
    qb, kb = Z_COLS["fq"] // FOX_DH, Z_COLS["fk"] // FOX_DH
    kern = functools.partial(_fox_kernel, tq=tq, tk=tk)
    return pl.pallas_call(
        kern,
        out_shape=jax.ShapeDtypeStruct((B, S, W), BF16),
        grid=(B, FOX_HEADS, S // tq),
        in_specs=[pl.BlockSpec((1, tq, FOX_DH), lambda b, h, i: (b, i, qb + h)),
                  pl.BlockSpec((1, tq, LANES), lambda b, h, i: (b, i, 0)),
                  pl.BlockSpec((1, S, FOX_DH), lambda b, h, i: (b, 0, kb + h)),
                  pl.BlockSpec((1, S, LANES), lambda b, h, i: (b, 0, 0)),
                  pl.BlockSpec((1, FOX_DH, S), lambda b, h, i: (b, h, 0))],
        out_specs=pl.BlockSpec((1, tq, FOX_DH), lambda b, h, i: (b, i, h)),
        scratch_shapes=[pltpu.VMEM((S, 2 * FOX_DH), BF16),
                        pltpu.VMEM((S // tk, FOX_DH + ones_rows, tk), BF16),
                        pltpu.VMEM((2 * FOX_DH, tq), BF16),
                        pltpu.VMEM((tk, tq), F32),
                        pltpu.VMEM((tk, tq), F32),
                        pltpu.VMEM((1, tq), F32),
                        pltpu.VMEM((1, tq), F32),
                        pltpu.VMEM((tk, tq), BF16),
                        pltpu.VMEM((tk, tq), BF16),
                        pltpu.VMEM((1, tq), F32),
                        pltpu.VMEM((1, tq), F32),
                        pltpu.VMEM((FOX_DH + ones_rows, tq), F32)],
        compiler_params=_cparams(("arbitrary", "arbitrary", "arbitrary"), 48),
        name="fox",
    )(z3, qa, z3, ka, vt)


FFN_CHUNK = 256


def _tail_kernel(a_ref, yf_ref, gf_ref, x_ref, g1_ref, sc_ref, sh_ref, g2_ref, n2_ref, nf_ref,
                 wf_ref, wo_ref, wg_ref, wu_ref, wd_ref, o_ref, act_scr):
    fo = jnp.dot(yf_ref[...], wf_ref[...], preferred_element_type=F32)
    merged = a_ref[...].astype(F32) + gf_ref[...].astype(F32) * fo
    out = jnp.dot(merged.astype(BF16), wo_ref[...], preferred_element_type=F32)
    x1 = x_ref[...] + g1_ref[0] * out

    ms = jnp.mean(x1 * x1, axis=-1, keepdims=True)
    y = x1 * lax.rsqrt(ms + EPS) * n2_ref[...]
    h = (y * (1.0 + sc_ref[0]) + sh_ref[0]).astype(BF16)
    for c0 in range(0, wg_ref.shape[1], FFN_CHUNK):
        g = jnp.dot(h, wg_ref[:, c0:c0 + FFN_CHUNK], preferred_element_type=F32)
        u = jnp.dot(h, wu_ref[:, c0:c0 + FFN_CHUNK], preferred_element_type=F32)
        act_scr[:, c0:c0 + FFN_CHUNK] = (g * jax.nn.sigmoid(g) * u).astype(BF16)
    ffn = jnp.dot(act_scr[...], wd_ref[...], preferred_element_type=F32)

    x2 = x1 + g2_ref[0] * ffn
    ms2 = jnp.mean(x2 * x2, axis=-1, keepdims=True)
    o_ref[...] = x2 * lax.rsqrt(ms2 + EPS) * nf_ref[...]


def _tail(a2, yf2, z2, x2, g1, sc2, sh2, g2, n2, nf, wf, wo, wg, wu, wd, S):
    T, D = x2.shape
    FF = wg.shape[1]
    assert FF % FFN_CHUNK == 0
    tm = 512
    spb = S // tm
    row = lambda i: (i, 0)
    bidx = lambda i: (i // spb, 0, 0)
    const = lambda shape: pl.BlockSpec(shape, lambda i: (0,) * len(shape),
                                       pipeline_mode=pl.Buffered(1))
    return pl.pallas_call(
        _tail_kernel,
        out_shape=jax.ShapeDtypeStruct((T, D), F32),
        grid=(T // tm,),
        in_specs=[pl.BlockSpec((tm, D), row),
                  pl.BlockSpec((tm, D), row),
                  pl.BlockSpec((tm, D), lambda i: (i, Z_COLS["gf"] // D)),
                  pl.BlockSpec((tm, D), row),
                  pl.BlockSpec((1, 1, D), bidx),
                  pl.BlockSpec((1, 1, D), bidx),
                  pl.BlockSpec((1, 1, D), bidx),
                  pl.BlockSpec((1, 1, D), bidx),
                  const((1, D)),
                  const((1, D)),
                  const((D, D)),
                  const((D, D)),
                  const((D, FF)),
                  const((D, FF)),
                  const((FF, D))],
        out_specs=pl.BlockSpec((tm, D), row),
        scratch_shapes=[pltpu.VMEM((tm, FF), BF16)],
        compiler_params=_cparams(("arbitrary",), 58),
        name="tail",
    )(a2, yf2, z2, x2, g1, sc2, sh2, g2, n2, nf, wf, wo, wg, wu, wd)


def _rope_tables(S):
    half = RET_DK // 2
    freqs = ROPE_THETA ** (-np.arange(half, dtype=np.float64) / half)
    ang = np.arange(S, dtype=np.float64)[:, None] * freqs[None, :]
    cos, sin = np.cos(ang), np.sin(ang)
    return (jnp.asarray(np.concatenate([cos, cos], axis=1), F32),
            jnp.asarray(np.concatenate([-sin, sin], axis=1), F32))


def _aug_lanes(v8):
    rep = jnp.repeat(v8[..., None], 6, axis=-1)
    pad = jnp.zeros(v8.shape + (AUG_STRIDE - 6,), v8.dtype)
    return jnp.concatenate([rep, pad], axis=-1).reshape(v8.shape[:-1] + (LANES,))


def _layer(x, mod, norm1_w, w_in, b_f, ret_proj, fox_proj, w_out, norm2_w,
           w_gate, w_up, w_down, norm_out_w, tables):
    B, S, D = x.shape
    T = B * S
    sh1, sc1, g1, sh2, sc2, g2 = [m.reshape(B, 1, D) for m in jnp.split(mod, 6, axis=-1)]
    offs = np.cumsum([0, RET_HEADS * RET_DK, RET_HEADS * RET_DK, RET_HEADS * RET_DV,
                      RET_HEADS * RET_DV, FOX_HEADS * FOX_DH, FOX_HEADS * FOX_DH,
                      FOX_HEADS * FOX_DH, FOX_HEADS, D, D])
    o_fv, o_ff, o_gr = int(offs[6]), int(offs[7]), int(offs[8])
    w_a = w_in[:, :o_fv].astype(BF16)
    w_b = w_in[:, o_gr:].astype(BF16)
    w_vt = w_in[:, o_fv:o_ff].T.astype(BF16)
    w_ff = _aug_lanes(w_in[:, o_ff:o_gr]).astype(BF16)
    bf_l = _aug_lanes(b_f).reshape(1, LANES)

    x2 = x.reshape(T, D)
    z, vt, qa, ka = _inproj(x2, norm1_w.reshape(1, D), sc1, sh1, tables[0], tables[1],
                            w_a, w_b, w_vt, w_ff, bf_l, B, S)
    z3 = z.reshape(B, S, z.shape[1])
    a = _retention(z3, ret_proj.astype(BF16), D)
    yf = _fox(z3, qa, ka, vt)
    out = _tail(a.reshape(T, D), yf.reshape(T, D), z, x2, g1, sc2, sh2, g2,
                norm2_w.reshape(1, D), norm_out_w.reshape(1, D),
                fox_proj.astype(BF16), w_out.astype(BF16), w_gate.astype(BF16),
                w_up.astype(BF16), w_down.astype(BF16), S)
    return out.reshape(B, S, D)


def kernel(x, c, ada_w, ada_b, norm1_w, w_in, b_f, ret_proj, fox_proj, w_out,
           norm2_w, w_gate, w_up, w_down, norm_f_w):
    depth = ada_w.shape[0]
    assert depth == 1, "the final RMSNorm is fused into the last layer's channel mixer"
    tables = _rope_tables(x.shape[1])
    l = 0
    mod = _ada(c, ada_w[l], ada_b[l])
    return _layer(x, mod, norm1_w[l], w_in[l], b_f[l], ret_proj[l], fox_proj[l], w_out[l],
                  norm2_w[l], w_gate[l], w_up[l], w_down[l], norm_f_w, tables)
```

```python
import functools
import math

import numpy as np
import jax
import jax.numpy as jnp
from jax import lax
from jax.experimental import pallas as pl
from jax.experimental.pallas import tpu as pltpu

F32 = jnp.float32
BF16 = jnp.bfloat16

EPS = 1e-6
ROPE_THETA = 10000.0
RET_HEADS = 4
RET_DK = 128
RET_DV = 256
FOX_HEADS = 8
FOX_DH = 128
RET_CHUNK = 64
LANES = 128
AUG_STRIDE = 16

NEG_BIG = -1e30
LOG2E = math.log2(math.e)

_NT = (((1,), (1,)), ((), ()))
_TN = (((0,), (0,)), ((), ()))


def _cparams(sem, vmem_mb, flags=None):
    return pltpu.CompilerParams(dimension_semantics=sem,
                                vmem_limit_bytes=vmem_mb * 1024 * 1024, flags=flags)


def _ada_kernel(ct_ref, w_ref, b_ref, o_ref):
    ct = ct_ref[...]
    act = ct * jax.nn.sigmoid(ct)
    w = w_ref[...]
    for b in range(o_ref.shape[0]):
        o_ref[b:b + 1, :] = (jnp.sum(act[:, b:b + 1] * w, axis=0, keepdims=True)
                             + b_ref[...])


def _ada(c, w, b):
    B, D = c.shape
    N = w.shape[1]
    tn = 1536
    return pl.pallas_call(
        _ada_kernel,
        out_shape=jax.ShapeDtypeStruct((B, N), F32),
        grid=(N // tn,),
        in_specs=[pl.BlockSpec((D, B), lambda j: (0, 0)),
                  pl.BlockSpec((D, tn), lambda j: (0, j)),
                  pl.BlockSpec((1, tn), lambda j: (0, j))],
        out_specs=pl.BlockSpec((B, tn), lambda j: (0, j)),
        compiler_params=_cparams(("arbitrary",), 40),
        name="ada",
    )(c.T, w, b.reshape(1, N))


INPROJ_CHUNK = 512
Z_COLS = {"rq": 0, "rk": 512, "rv": 1024, "rg": 2048, "fq": 3072, "fk": 4096,
          "gr": 5120, "gf": 6144, "end": 7168}


def _split3(v):
    p0 = v.astype(BF16)
    r1 = v - p0.astype(F32)
    p1 = r1.astype(BF16)
    r2 = r1 - p1.astype(F32)
    p2 = r2.astype(BF16)
    return p0, p1, p2


def _inproj_kernel(x_ref, nw_ref, sc_ref, sh_ref, cos_ref, sin_ref, wa_ref, wb_ref, wvt_ref,
                   wff_ref, bf_ref, tri_ref, z_ref, vt_ref, qa_ref, ka_ref, carry,
                   *, fox_scale, rk_scale, spb):
    x = x_ref[...]
    ms = jnp.mean(x * x, axis=-1, keepdims=True)
    y = x * lax.rsqrt(ms + EPS) * nw_ref[...]
    h = (y * (1.0 + sc_ref[0]) + sh_ref[0]).astype(BF16)
    cos = cos_ref[...]
    sin = sin_ref[...]
    cw = INPROJ_CHUNK

    def chunk(c0):
        na = wa_ref.shape[1]
        w = wa_ref[:, c0:c0 + cw] if c0 < na else wb_ref[:, c0 - na:c0 - na + cw]
        return jnp.dot(h, w, preferred_element_type=F32)

    def rotary(c0, scale):
        acc = chunk(c0)
        for hh in range(cw // RET_DK):
            a = acc[:, hh * RET_DK:(hh + 1) * RET_DK]
            o = a * cos + pltpu.roll(a, RET_DK // 2, axis=1) * sin
            if scale is not None:
                o = o * scale
            z_ref[:, c0 + hh * RET_DK:c0 + (hh + 1) * RET_DK] = o.astype(BF16)

    def group(name, nxt, fn):
        for c0 in range(Z_COLS[name], Z_COLS[nxt], cw):
            z_ref[:, c0:c0 + cw] = fn(chunk(c0)).astype(BF16)

    rotary(Z_COLS["rq"], None)
    rotary(Z_COLS["rk"], rk_scale)
    group("rv", "rg", lambda a: a)
    group("rg", "fq", lambda a: a * jax.nn.sigmoid(a))
    group("fq", "fk", lambda a: a * fox_scale)
    group("fk", "gr", lambda a: a)
    group("gr", "end", jax.nn.sigmoid)
    for r0 in range(0, wvt_ref.shape[0], cw):
        vt_ref[0, r0:r0 + cw, :] = lax.dot_general(
            wvt_ref[r0:r0 + cw, :], h, _NT, preferred_element_type=F32).astype(BF16)

    xv = jnp.dot(h, wff_ref[...], preferred_element_type=F32) + bf_ref[...]
    lf = jnp.minimum(xv, 0.0) - jnp.log(1.0 + jnp.exp(-jnp.abs(xv)))
    tri = tri_ref[...]
    p0, p1, p2 = _split3(lf)
    cs = (jnp.dot(tri, p0, preferred_element_type=F32)
          + jnp.dot(tri, p1, preferred_element_type=F32)
          + jnp.dot(tri, p2, preferred_element_type=F32))
    first = (pl.program_id(0) % spb) == 0
    fc = cs + jnp.where(first, 0.0, carry[...])
    tm = fc.shape[0]
    carry[...] = fc[tm - 1:tm, :]
    f0, f1, f2 = [p.astype(F32) for p in _split3(fc * LOG2E)]
    c = lax.broadcasted_iota(jnp.int32, fc.shape, 1) % AUG_STRIDE
    qa_ref[0] = jnp.where(c == 0, f0, jnp.where(c == 1, f1, jnp.where(
        c == 2, f2, jnp.where(c < 6, 1.0, 0.0)))).astype(BF16)
    ka_ref[0] = jnp.where(c < 3, 1.0, jnp.where(c == 3, -f0, jnp.where(
        c == 4, -f1, jnp.where(c == 5, -f2, 0.0)))).astype(BF16)


def _inproj(x2, nw, sc, sh, cos_t, sin_t, wa, wb, wvt, wff, bf_l, B, S):
    T, D = x2.shape
    NA, NB = wa.shape[1], wb.shape[1]
    N = NA + NB
    assert N == Z_COLS["end"] and NA % INPROJ_CHUNK == 0
    NV = wvt.shape[0]
    tm = 512
    spb = S // tm
    tri = jnp.asarray(np.tril(np.ones((tm, tm), np.float32)), BF16)
    kern = functools.partial(_inproj_kernel, fox_scale=LOG2E / math.sqrt(FOX_DH),
                             rk_scale=RET_DK ** -0.5, spb=spb)
    const = lambda shape: pl.BlockSpec(shape, lambda i: (0,) * len(shape),
                                       pipeline_mode=pl.Buffered(1))
    return pl.pallas_call(
        kern,
        out_shape=(jax.ShapeDtypeStruct((T, N), BF16),
                   jax.ShapeDtypeStruct((B, NV, S), BF16),
                   jax.ShapeDtypeStruct((B, S, LANES), BF16),
                   jax.ShapeDtypeStruct((B, S, LANES), BF16)),
        grid=(T // tm,),
        in_specs=[pl.BlockSpec((tm, D), lambda i: (i, 0)),
                  const((1, D)),
                  pl.BlockSpec((1, 1, D), lambda i: (i // spb, 0, 0)),
                  pl.BlockSpec((1, 1, D), lambda i: (i // spb, 0, 0)),
                  pl.BlockSpec((tm, LANES), lambda i: (i % spb, 0)),
                  pl.BlockSpec((tm, LANES), lambda i: (i % spb, 0)),
                  const((D, NA)),
                  const((D, NB)),
                  const((NV, D)),
                  const((D, LANES)),
                  const((1, LANES)),
                  const((tm, tm))],
        out_specs=(pl.BlockSpec((tm, N), lambda i: (i, 0)),
                   pl.BlockSpec((1, NV, tm), lambda i: (i // spb, 0, i % spb)),
                   pl.BlockSpec((1, tm, LANES), lambda i: (i // spb, i % spb, 0)),
                   pl.BlockSpec((1, tm, LANES), lambda i: (i // spb, i % spb, 0))),
        scratch_shapes=[pltpu.VMEM((1, LANES), F32)],
        compiler_params=_cparams(("arbitrary",), 56),
        name="inproj",
    )(x2, nw, sc, sh, cos_t, sin_t, wa, wb, wvt, wff, bf_l, tri)


def _ret_consts(L):
    hs = np.arange(RET_HEADS, dtype=np.float64)
    log_gamma = np.log(1.0 - np.exp2(-5.0 - hs))
    idx = np.arange(L, dtype=np.float64)
    dist = np.abs(idx[:, None] - idx[None, :])
    chunk_ok = (idx[None, :] // RET_CHUNK) <= (idx[:, None] // RET_CHUNK)
    dmask = np.exp(log_gamma[:, None, None] * dist[None]) * chunk_ok[None]
    qdec = np.exp(log_gamma[:, None] * idx[None, :])[..., None]
    kdec = np.exp(log_gamma[:, None] * (L - idx)[None, :])[..., None]
    bdec = [float(v) for v in np.exp(log_gamma * L)]
    return (jnp.asarray(dmask, F32), jnp.asarray(qdec, F32), jnp.asarray(kdec, F32), bdec)


def _ret_kernel(rq_ref, rk_ref, rv_ref, rg_ref, gr_ref, dm_ref, qd_ref, kd_ref, wr_ref,
                o_ref, st_scr, y_scr, *, bdec):
    @pl.when(pl.program_id(1) == 0)
    def _():
        st_scr[...] = jnp.zeros_like(st_scr)

    for h in range(RET_HEADS):
        q = rq_ref[0, :, h * RET_DK:(h + 1) * RET_DK]
        k = rk_ref[0, :, h * RET_DK:(h + 1) * RET_DK]
        v = rv_ref[0, :, h * RET_DV:(h + 1) * RET_DV]
        s = lax.dot_general(q, k, _NT, preferred_element_type=F32) * dm_ref[h]
        intra = jnp.dot(s.astype(BF16), v, preferred_element_type=F32)
        st = st_scr[h]
        inter = jnp.dot(q, st.astype(BF16), preferred_element_type=F32) * qd_ref[h]
        ks = (k.astype(F32) * kd_ref[h]).astype(BF16)
        st_scr[h] = bdec[h] * st + lax.dot_general(ks, v, _TN, preferred_element_type=F32)
        ro = intra + inter
        ron = ro * lax.rsqrt(jnp.mean(ro * ro, axis=-1, keepdims=True) + EPS)
        g = rg_ref[0, :, h * RET_DV:(h + 1) * RET_DV].astype(F32)
        y_scr[:, h * RET_DV:(h + 1) * RET_DV] = (g * ron).astype(BF16)

    proj = jnp.dot(y_scr[...], wr_ref[...], preferred_element_type=F32)
    o_ref[0] = (gr_ref[0].astype(F32) * proj).astype(BF16)


def _retention(z3, wr, D):
    B, S, _ = z3.shape
    L = 256
    dmask, qdec, kdec, bdec = _ret_consts(L)
    QK = RET_HEADS * RET_DK
    V = RET_HEADS * RET_DV
    kern = functools.partial(_ret_kernel, bdec=bdec)
    full3 = lambda b, i: (0, 0, 0)
    return pl.pallas_call(
        kern,
        out_shape=jax.ShapeDtypeStruct((B, S, D), BF16),
        grid=(B, S // L),
        in_specs=[pl.BlockSpec((1, L, QK), lambda b, i: (b, i, 0)),
                  pl.BlockSpec((1, L, QK), lambda b, i: (b, i, 1)),
                  pl.BlockSpec((1, L, V), lambda b, i: (b, i, 1)),
                  pl.BlockSpec((1, L, V), lambda b, i: (b, i, 2)),
                  pl.BlockSpec((1, L, D), lambda b, i: (b, i, 5)),
                  pl.BlockSpec((RET_HEADS, L, L), full3),
                  pl.BlockSpec((RET_HEADS, L, 1), full3),
                  pl.BlockSpec((RET_HEADS, L, 1), full3),
                  pl.BlockSpec((V, D), lambda b, i: (0, 0))],
        out_specs=pl.BlockSpec((1, L, D), lambda b, i: (b, i, 0)),
        scratch_shapes=[pltpu.VMEM((RET_HEADS, RET_DK, RET_DV), F32),
                        pltpu.VMEM((L, V), BF16)],
        compiler_params=_cparams(("arbitrary", "arbitrary"), 40),
        name="ret",
    )(z3, z3, z3, z3, z3, dmask, qdec, kdec, wr)


FOX_CHUNK = 256
FOX_MAX_EXCESS = 64.0


def _fox_kernel(q_ref, qa_ref, k_ref, ka_ref, vt_ref, o_ref,
                kx_scr, vt_scr, qxt_scr, sa_scr, sb_scr, mba_scr, mbb_scr, pa_scr, pb_scr,
                m_scr, al_scr, acc_scr, ex_scr, *, tq, tk):
    h = pl.program_id(1)
    i = pl.program_id(2)
    nkb = vt_scr.shape[0]

    @pl.when(i == 0)
    def _():
        kx_scr[:, :FOX_DH] = k_ref[0]
        kx_scr[:, FOX_DH:] = ka_ref[0]
        for n in range(nkb):
            vt_scr[n, :FOX_DH, :] = vt_ref[0, :, n * tk:(n + 1) * tk]
            vt_scr[n, FOX_DH:, :] = jnp.ones((vt_scr.shape[1] - FOX_DH, tk), BF16)

    lane = lax.broadcasted_iota(jnp.int32, (tq, LANES), 1)
    qa = jnp.where((lane // AUG_STRIDE) == h, qa_ref[0].astype(F32), 0.0)
    qxt_scr[:FOX_DH, :] = q_ref[0].astype(F32).T.astype(BF16)
    qxt_scr[FOX_DH:, :] = qa.T.astype(BF16)

    chunks = [slice(c0, c0 + FOX_CHUNK) for c0 in range(0, tq, FOX_CHUNK)]

    def qk(blk, cs):
        off = pl.multiple_of(blk * tk, tk)
        return jnp.dot(kx_scr[pl.ds(off, tk), :], qxt_scr[:, cs],
                       preferred_element_type=F32)

    def causal(s, cs, key0):
        key = lax.broadcasted_iota(jnp.int32, s.shape, 0) + key0
        qry = lax.broadcasted_iota(jnp.int32, s.shape, 1) + cs.start
        return jnp.where(key <= qry, s, NEG_BIG)

    def emit():
        acc = acc_scr[...]
        o_ref[0] = (acc[:FOX_DH] / acc[FOX_DH:FOX_DH + 1]).T.astype(BF16)

    def pv_acc(blk_prev, p_prv, cs):
        acc_scr[:, cs] = al_scr[:, cs] * (acc_scr[:, cs] + jnp.dot(
            vt_scr[blk_prev], p_prv[:, cs], preferred_element_type=F32))

    def lazy_chunk(blk, cs, p_cur, key0):
        s = qk(blk, cs)
        if key0 is not None:
            s = causal(s, cs, key0)
        mb = jnp.max(s, axis=0, keepdims=True)
        r_old = m_scr[:, cs]
        p_cur[:, cs] = jnp.exp2((s - r_old).astype(BF16))
        r_new = jnp.maximum(r_old, mb)
        m_scr[:, cs] = r_new
        ex_scr[:, cs] = jnp.maximum(ex_scr[:, cs], mb - r_old)
        return jnp.exp2(r_old - r_new)

    def lazy_step(blk, blk_prev, p_cur, p_prv):
        for cs in chunks:
            alpha = lazy_chunk(blk, cs, p_cur, None)
            pv_acc(blk_prev, p_prv, cs)
            al_scr[:, cs] = alpha

    for cs in chunks:
        s = qk(2 * i, cs)
        if cs.start < tk:
            s = causal(s, cs, 0)
        mb = jnp.max(s, axis=0, keepdims=True)
        m_scr[:, cs] = mb
        pa_scr[:, cs] = jnp.exp2((s - mb).astype(BF16))
        al_scr[:, cs] = jnp.ones((1, FOX_CHUNK), F32)
        acc_scr[:, cs] = jnp.zeros((acc_scr.shape[0], FOX_CHUNK), F32)
        ex_scr[:, cs] = jnp.zeros((1, FOX_CHUNK), F32)
    for cs in chunks:
        if cs.start < tk:
            pb_scr[:, cs] = jnp.zeros((tk, FOX_CHUNK), BF16)
            pv_acc(2 * i, pa_scr, cs)
        else:
            alpha = lazy_chunk(2 * i + 1, cs, pb_scr, tk)
            pv_acc(2 * i, pa_scr, cs)
            al_scr[:, cs] = alpha

    def lazy_body(t, carry):
        jj = i - 1 - t
        lazy_step(2 * jj + 1, jnp.where(t == 0, 2 * i + 1, 2 * jj + 2), pa_scr, pb_scr)
        lazy_step(2 * jj, 2 * jj + 1, pb_scr, pa_scr)
        return carry

    lax.fori_loop(0, i, lazy_body, 0)
    last = jnp.where(i > 0, 0, 1)
    for cs in chunks:
        pv_acc(last, pb_scr, cs)
    emit()

    def scores(blk, s_out, mb_out, cs):
        s = qk(blk, cs)
        s_out[:, cs] = s
        if mb_out is not None:
            mb_out[:, cs] = jnp.max(s, axis=0, keepdims=True)

    def softmax(s_in, mb_in, p_out, key0, cs):
        s = s_in[:, cs]
        if key0 is None:
            mb = mb_in[:, cs]
        else:
            s = causal(s, cs, key0)
            mb = jnp.max(s, axis=0, keepdims=True)
        m_prev = m_scr[:, cs]
        m_new = jnp.maximum(m_prev, mb)
        p_out[:, cs] = jnp.exp2((s - m_new).astype(BF16))
        m_scr[:, cs] = m_new
        return jnp.exp2(m_prev - m_new)

    def pv(blk_prev, p_in, cs):
        acc_scr[:, cs] = al_scr[:, cs] * acc_scr[:, cs] + jnp.dot(
            vt_scr[blk_prev], p_in[:, cs], preferred_element_type=F32)

    def step(n, cur, nxt, p_prv, key0):
        s_cur, mb_cur, p_cur = cur
        for cs in chunks:
            if nxt is not None:
                scores(n + 1, nxt[0], nxt[1], cs)
            alpha = softmax(s_cur, mb_cur, p_cur, key0, cs)
            pv(jnp.maximum(n - 1, 0), p_prv, cs)
            al_scr[:, cs] = alpha

    buf_a = (sa_scr, mba_scr, pa_scr)
    buf_b = (sb_scr, mbb_scr, pb_scr)

    def body(jj, carry):
        step(2 * jj, buf_a, buf_b, pb_scr, None)
        step(2 * jj + 1, buf_b, buf_a, pa_scr, None)
        return carry

    @pl.when(jnp.max(ex_scr[...]) > FOX_MAX_EXCESS)
    def _():
        m_scr[...] = jnp.full(m_scr.shape, NEG_BIG, F32)
        al_scr[...] = jnp.ones_like(al_scr)
        acc_scr[...] = jnp.zeros_like(acc_scr)
        pb_scr[...] = jnp.zeros_like(pb_scr)
        for cs in chunks:
            scores(0, sa_scr, mba_scr, cs)
        lax.fori_loop(0, i, body, 0)
        step(2 * i, buf_a, (sb_scr, None), pb_scr, 0)
        step(2 * i + 1, buf_b, None, pa_scr, tk)
        for cs in chunks:
            pv(2 * i + 1, pb_scr, cs)
        emit()


def _fox(z3, qa, ka, vt):
    B, S, _ = z3.shape
    tq, tk = 1024, 512
    assert tq == 2 * tk
    ones_rows = 16
    W = FOX_HEADS * FOX_DH
    qb, kb = Z_COLS["fq"] // FOX_DH, Z_COLS["fk"] // FOX_DH
    kern = functools.partial(_fox_kernel, tq=tq, tk=tk)
    return pl.pallas_call(
        kern,
        out_shape=jax.ShapeDtypeStruct((B, S, W), BF16),
        grid=(B, FOX_HEADS, S // tq),
        in_specs=[pl.BlockSpec((1, tq, FOX_DH), lambda b, h, i: (b, i, qb + h)),
                  pl.BlockSpec((1, tq, LANES), lambda b, h, i: (b, i, 0)),
                  pl.BlockSpec((1, S, FOX_DH), lambda b, h, i: (b, 0, kb + h)),
                  pl.BlockSpec((1, S, LANES), lambda b, h, i: (b, 0, 0)),
                  pl.BlockSpec((1, FOX_DH, S), lambda b, h, i: (b, h, 0))],
        out_specs=pl.BlockSpec((1, tq, FOX_DH), lambda b, h, i: (b, i, h)),
        scratch_shapes=[pltpu.VMEM((S, 2 * FOX_DH), BF16),
                        pltpu.VMEM((S // tk, FOX_DH + ones_rows, tk), BF16),
                        pltpu.VMEM((2 * FOX_DH, tq), BF16),
                        pltpu.VMEM((tk, tq), F32),
                        pltpu.VMEM((tk, tq), F32),
                        pltpu.VMEM((1, tq), F32),
                        pltpu.VMEM((1, tq), F32),
                        pltpu.VMEM((tk, tq), BF16),
                        pltpu.VMEM((tk, tq), BF16),
                        pltpu.VMEM((1, tq), F32),
                        pltpu.VMEM((1, tq), F32),
                        pltpu.VMEM((FOX_DH + ones_rows, tq), F32),
                        pltpu.VMEM((1, tq), F32)],
        compiler_params=_cparams(("arbitrary", "arbitrary", "arbitrary"), 48),
        name="fox",
    )(z3, qa, z3, ka, vt)


FFN_CHUNK = 256


def _tail_kernel(a_ref, yf_ref, gf_ref, x_ref, g1_ref, sc_ref, sh_ref, g2_ref, n2_ref, nf_ref,
                 wf_ref, wo_ref, wg_ref, wu_ref, wd_ref, o_ref, act_scr):
    fo = jnp.dot(yf_ref[...], wf_ref[...], preferred_element_type=F32)
    merged = a_ref[...].astype(F32) + gf_ref[...].astype(F32) * fo
    out = jnp.dot(merged.astype(BF16), wo_ref[...], preferred_element_type=F32)
    x1 = x_ref[...] + g1_ref[0] * out

    ms = jnp.mean(x1 * x1, axis=-1, keepdims=True)
    y = x1 * lax.rsqrt(ms + EPS) * n2_ref[...]
    h = (y * (1.0 + sc_ref[0]) + sh_ref[0]).astype(BF16)
    for c0 in range(0, wg_ref.shape[1], FFN_CHUNK):
        g = jnp.dot(h, wg_ref[:, c0:c0 + FFN_CHUNK], preferred_element_type=F32)
        u = jnp.dot(h, wu_ref[:, c0:c0 + FFN_CHUNK], preferred_element_type=F32)
        act_scr[:, c0:c0 + FFN_CHUNK] = (g * jax.nn.sigmoid(g) * u).astype(BF16)
    ffn = jnp.dot(act_scr[...], wd_ref[...], preferred_element_type=F32)

    x2 = x1 + g2_ref[0] * ffn
    ms2 = jnp.mean(x2 * x2, axis=-1, keepdims=True)
    o_ref[...] = x2 * lax.rsqrt(ms2 + EPS) * nf_ref[...]


def _tail(a2, yf2, z2, x2, g1, sc2, sh2, g2, n2, nf, wf, wo, wg, wu, wd, S):
    T, D = x2.shape
    FF = wg.shape[1]
    assert FF % FFN_CHUNK == 0
    tm = 512
    spb = S // tm
    row = lambda i: (i, 0)
    bidx = lambda i: (i // spb, 0, 0)
    const = lambda shape: pl.BlockSpec(shape, lambda i: (0,) * len(shape),
                                       pipeline_mode=pl.Buffered(1))
    return pl.pallas_call(
        _tail_kernel,
        out_shape=jax.ShapeDtypeStruct((T, D), F32),
        grid=(T // tm,),
        in_specs=[pl.BlockSpec((tm, D), row),
                  pl.BlockSpec((tm, D), row),
                  pl.BlockSpec((tm, D), lambda i: (i, Z_COLS["gf"] // D)),
                  pl.BlockSpec((tm, D), row),
                  pl.BlockSpec((1, 1, D), bidx),
                  pl.BlockSpec((1, 1, D), bidx),
                  pl.BlockSpec((1, 1, D), bidx),
                  pl.BlockSpec((1, 1, D), bidx),
                  const((1, D)),
                  const((1, D)),
                  const((D, D)),
                  const((D, D)),
                  const((D, FF)),
                  const((D, FF)),
                  const((FF, D))],
        out_specs=pl.BlockSpec((tm, D), row),
        scratch_shapes=[pltpu.VMEM((tm, FF), BF16)],
        compiler_params=_cparams(("arbitrary",), 58),
        name="tail",
    )(a2, yf2, z2, x2, g1, sc2, sh2, g2, n2, nf, wf, wo, wg, wu, wd)


def _rope_tables(S):
    half = RET_DK // 2
    freqs = ROPE_THETA ** (-np.arange(half, dtype=np.float64) / half)
    ang = np.arange(S, dtype=np.float64)[:, None] * freqs[None, :]
    cos, sin = np.cos(ang), np.sin(ang)
    return (jnp.asarray(np.concatenate([cos, cos], axis=1), F32),
            jnp.asarray(np.concatenate([-sin, sin], axis=1), F32))


def _aug_lanes(v8):
    rep = jnp.repeat(v8[..., None], 6, axis=-1)
    pad = jnp.zeros(v8.shape + (AUG_STRIDE - 6,), v8.dtype)
    return jnp.concatenate([rep, pad], axis=-1).reshape(v8.shape[:-1] + (LANES,))


def _layer(x, mod, norm1_w, w_in, b_f, ret_proj, fox_proj, w_out, norm2_w,
           w_gate, w_up, w_down, norm_out_w, tables):
    B, S, D = x.shape
    T = B * S
    sh1, sc1, g1, sh2, sc2, g2 = [m.reshape(B, 1, D) for m in jnp.split(mod, 6, axis=-1)]
    offs = np.cumsum([0, RET_HEADS * RET_DK, RET_HEADS * RET_DK, RET_HEADS * RET_DV,
                      RET_HEADS * RET_DV, FOX_HEADS * FOX_DH, FOX_HEADS * FOX_DH,
                      FOX_HEADS * FOX_DH, FOX_HEADS, D, D])
    o_fv, o_ff, o_gr = int(offs[6]), int(offs[7]), int(offs[8])
    w_a = w_in[:, :o_fv].astype(BF16)
    w_b = w_in[:, o_gr:].astype(BF16)
    w_vt = w_in[:, o_fv:o_ff].T.astype(BF16)
    w_ff = _aug_lanes(w_in[:, o_ff:o_gr]).astype(BF16)
    bf_l = _aug_lanes(b_f).reshape(1, LANES)

    x2 = x.reshape(T, D)
    z, vt, qa, ka = _inproj(x2, norm1_w.reshape(1, D), sc1, sh1, tables[0], tables[1],
                            w_a, w_b, w_vt, w_ff, bf_l, B, S)
    z3 = z.reshape(B, S, z.shape[1])
    a = _retention(z3, ret_proj.astype(BF16), D)
    yf = _fox(z3, qa, ka, vt)
    out = _tail(a.reshape(T, D), yf.reshape(T, D), z, x2, g1, sc2, sh2, g2,
                norm2_w.reshape(1, D), norm_out_w.reshape(1, D),
                fox_proj.astype(BF16), w_out.astype(BF16), w_gate.astype(BF16),
                w_up.astype(BF16), w_down.astype(BF16), S)
    return out.reshape(B, S, D)


def kernel(x, c, ada_w, ada_b, norm1_w, w_in, b_f, ret_proj, fox_proj, w_out,
           norm2_w, w_gate, w_up, w_down, norm_f_w):
    depth = ada_w.shape[0]
    assert depth == 1, "the final RMSNorm is fused into the last layer's channel mixer"
    tables = _rope_tables(x.shape[1])
    l = 0
    mod = _ada(c, ada_w[l], ada_b[l])
    return _layer(x, mod, norm1_w[l], w_in[l], b_f[l], ret_proj[l], fox_proj[l], w_out[l],
                  norm2_w[l], w_gate[l], w_up[l], w_down[l], norm_f_w, tables)
```

```python
import functools
import math

import numpy as np
import jax
import jax.numpy as jnp
from jax import lax
from jax.experimental import pallas as pl
from jax.experimental.pallas import tpu as pltpu

F32 = jnp.float32
BF16 = jnp.bfloat16

EPS = 1e-6
ROPE_THETA = 10000.0
RET_HEADS = 4
RET_DK = 128
RET_DV = 256
FOX_HEADS = 8
FOX_DH = 128
RET_CHUNK = 64
LANES = 128
AUG_STRIDE = 16

NEG_BIG = -1e30
LOG2E = math.log2(math.e)

_NT = (((1,), (1,)), ((), ()))
_TN = (((0,), (0,)), ((), ()))


def _cparams(sem, vmem_mb, flags=None):
    return pltpu.CompilerParams(dimension_semantics=sem,
                                vmem_limit_bytes=vmem_mb * 1024 * 1024, flags=flags)


def _ada_kernel(ct_ref, w_ref, b_ref, o_ref):
    ct = ct_ref[...]
    act = ct * jax.nn.sigmoid(ct)
    w = w_ref[...]
    for b in range(o_ref.shape[0]):
        o_ref[b:b + 1, :] = (jnp.sum(act[:, b:b + 1] * w, axis=0, keepdims=True)
                             + b_ref[...])


def _ada(c, w, b):
    B, D = c.shape
    N = w.shape[1]
    tn = 1536
    return pl.pallas_call(
        _ada_kernel,
        out_shape=jax.ShapeDtypeStruct((B, N), F32),
        grid=(N // tn,),
        in_specs=[pl.BlockSpec((D, B), lambda j: (0, 0)),
                  pl.BlockSpec((D, tn), lambda j: (0, j)),
                  pl.BlockSpec((1, tn), lambda j: (0, j))],
        out_specs=pl.BlockSpec((B, tn), lambda j: (0, j)),
        compiler_params=_cparams(("arbitrary",), 40),
        name="ada",
    )(c.T, w, b.reshape(1, N))


INPROJ_CHUNK = 512
Z_COLS = {"rq": 0, "rk": 512, "rv": 1024, "rg": 2048, "fq": 3072, "fk": 4096,
          "gr": 5120, "gf": 6144, "end": 7168}


def _split3(v):
    p0 = v.astype(BF16)
    r1 = v - p0.astype(F32)
    p1 = r1.astype(BF16)
    r2 = r1 - p1.astype(F32)
    p2 = r2.astype(BF16)
    return p0, p1, p2


def _inproj_kernel(x_ref, nw_ref, sc_ref, sh_ref, cos_ref, sin_ref, wa_ref, wb_ref, wvt_ref,
                   wff_ref, bf_ref, tri_ref, z_ref, vt_ref, qa_ref, ka_ref, carry,
                   *, fox_scale, rk_scale, spb):
    x = x_ref[...]
    ms = jnp.mean(x * x, axis=-1, keepdims=True)
    y = x * lax.rsqrt(ms + EPS) * nw_ref[...]
    h = (y * (1.0 + sc_ref[0]) + sh_ref[0]).astype(BF16)
    cos = cos_ref[...]
    sin = sin_ref[...]
    cw = INPROJ_CHUNK

    def chunk(c0):
        na = wa_ref.shape[1]
        w = wa_ref[:, c0:c0 + cw] if c0 < na else wb_ref[:, c0 - na:c0 - na + cw]
        return jnp.dot(h, w, preferred_element_type=F32)

    def rotary(c0, scale):
        acc = chunk(c0)
        for hh in range(cw // RET_DK):
            a = acc[:, hh * RET_DK:(hh + 1) * RET_DK]
            o = a * cos + pltpu.roll(a, RET_DK // 2, axis=1) * sin
            if scale is not None:
                o = o * scale
            z_ref[:, c0 + hh * RET_DK:c0 + (hh + 1) * RET_DK] = o.astype(BF16)

    def group(name, nxt, fn):
        for c0 in range(Z_COLS[name], Z_COLS[nxt], cw):
            z_ref[:, c0:c0 + cw] = fn(chunk(c0)).astype(BF16)

    rotary(Z_COLS["rq"], None)
    rotary(Z_COLS["rk"], rk_scale)
    group("rv", "rg", lambda a: a)
    group("rg", "fq", lambda a: a * jax.nn.sigmoid(a))
    group("fq", "fk", lambda a: a * fox_scale)
    group("fk", "gr", lambda a: a)
    group("gr", "end", jax.nn.sigmoid)
    for r0 in range(0, wvt_ref.shape[0], cw):
        vt_ref[0, r0:r0 + cw, :] = lax.dot_general(
            wvt_ref[r0:r0 + cw, :], h, _NT, preferred_element_type=F32).astype(BF16)

    xv = jnp.dot(h, wff_ref[...], preferred_element_type=F32) + bf_ref[...]
    lf = jnp.minimum(xv, 0.0) - jnp.log(1.0 + jnp.exp(-jnp.abs(xv)))
    tri = tri_ref[...]
    p0, p1, p2 = _split3(lf)
    cs = (jnp.dot(tri, p0, preferred_element_type=F32)
          + jnp.dot(tri, p1, preferred_element_type=F32)
          + jnp.dot(tri, p2, preferred_element_type=F32))
    first = (pl.program_id(0) % spb) == 0
    fc = cs + jnp.where(first, 0.0, carry[...])
    tm = fc.shape[0]
    carry[...] = fc[tm - 1:tm, :]
    f0, f1, f2 = [p.astype(F32) for p in _split3(fc * LOG2E)]
    c = lax.broadcasted_iota(jnp.int32, fc.shape, 1) % AUG_STRIDE
    qa_ref[0] = jnp.where(c == 0, f0, jnp.where(c == 1, f1, jnp.where(
        c == 2, f2, jnp.where(c < 6, 1.0, 0.0)))).astype(BF16)
    ka_ref[0] = jnp.where(c < 3, 1.0, jnp.where(c == 3, -f0, jnp.where(
        c == 4, -f1, jnp.where(c == 5, -f2, 0.0)))).astype(BF16)


def _inproj(x2, nw, sc, sh, cos_t, sin_t, wa, wb, wvt, wff, bf_l, B, S):
    T, D = x2.shape
    NA, NB = wa.shape[1], wb.shape[1]
    N = NA + NB
    assert N == Z_COLS["end"] and NA % INPROJ_CHUNK == 0
    NV = wvt.shape[0]
    tm = 512
    spb = S // tm
    tri = jnp.asarray(np.tril(np.ones((tm, tm), np.float32)), BF16)
    kern = functools.partial(_inproj_kernel, fox_scale=LOG2E / math.sqrt(FOX_DH),
                             rk_scale=RET_DK ** -0.5, spb=spb)
    const = lambda shape: pl.BlockSpec(shape, lambda i: (0,) * len(shape),
                                       pipeline_mode=pl.Buffered(1))
    return pl.pallas_call(
        kern,
        out_shape=(jax.ShapeDtypeStruct((T, N), BF16),
                   jax.ShapeDtypeStruct((B, NV, S), BF16),
                   jax.ShapeDtypeStruct((B, S, LANES), BF16),
                   jax.ShapeDtypeStruct((B, S, LANES), BF16)),
        grid=(T // tm,),
        in_specs=[pl.BlockSpec((tm, D), lambda i: (i, 0)),
                  const((1, D)),
                  pl.BlockSpec((1, 1, D), lambda i: (i // spb, 0, 0)),
                  pl.BlockSpec((1, 1, D), lambda i: (i // spb, 0, 0)),
                  pl.BlockSpec((tm, LANES), lambda i: (i % spb, 0)),
                  pl.BlockSpec((tm, LANES), lambda i: (i % spb, 0)),
                  const((D, NA)),
                  const((D, NB)),
                  const((NV, D)),
                  const((D, LANES)),
                  const((1, LANES)),
                  const((tm, tm))],
        out_specs=(pl.BlockSpec((tm, N), lambda i: (i, 0)),
                   pl.BlockSpec((1, NV, tm), lambda i: (i // spb, 0, i % spb)),
                   pl.BlockSpec((1, tm, LANES), lambda i: (i // spb, i % spb, 0)),
                   pl.BlockSpec((1, tm, LANES), lambda i: (i // spb, i % spb, 0))),
        scratch_shapes=[pltpu.VMEM((1, LANES), F32)],
        compiler_params=_cparams(("arbitrary",), 56),
        name="inproj",
    )(x2, nw, sc, sh, cos_t, sin_t, wa, wb, wvt, wff, bf_l, tri)


def _ret_consts(L):
    hs = np.arange(RET_HEADS, dtype=np.float64)
    log_gamma = np.log(1.0 - np.exp2(-5.0 - hs))
    idx = np.arange(L, dtype=np.float64)
    dist = np.abs(idx[:, None] - idx[None, :])
    chunk_ok = (idx[None, :] // RET_CHUNK) <= (idx[:, None] // RET_CHUNK)
    dmask = np.exp(log_gamma[:, None, None] * dist[None]) * chunk_ok[None]
    qdec = np.exp(log_gamma[:, None] * idx[None, :])[..., None]
    kdec = np.exp(log_gamma[:, None] * (L - idx)[None, :])[..., None]
    bdec = [float(v) for v in np.exp(log_gamma * L)]
    return (jnp.asarray(dmask, F32), jnp.asarray(qdec, F32), jnp.asarray(kdec, F32), bdec)


def _ret_kernel(rq_ref, rk_ref, rv_ref, rg_ref, gr_ref, dm_ref, qd_ref, kd_ref, wr_ref,
                o_ref, st_scr, y_scr, *, bdec):
    @pl.when(pl.program_id(1) == 0)
    def _():
        st_scr[...] = jnp.zeros_like(st_scr)

    for h in range(RET_HEADS):
        q = rq_ref[0, :, h * RET_DK:(h + 1) * RET_DK]
        k = rk_ref[0, :, h * RET_DK:(h + 1) * RET_DK]
        v = rv_ref[0, :, h * RET_DV:(h + 1) * RET_DV]
        s = lax.dot_general(q, k, _NT, preferred_element_type=F32) * dm_ref[h]
        intra = jnp.dot(s.astype(BF16), v, preferred_element_type=F32)
        st = st_scr[h]
        inter = jnp.dot(q, st.astype(BF16), preferred_element_type=F32) * qd_ref[h]
        ks = (k.astype(F32) * kd_ref[h]).astype(BF16)
        st_scr[h] = bdec[h] * st + lax.dot_general(ks, v, _TN, preferred_element_type=F32)
        ro = intra + inter
        ron = ro * lax.rsqrt(jnp.mean(ro * ro, axis=-1, keepdims=True) + EPS)
        g = rg_ref[0, :, h * RET_DV:(h + 1) * RET_DV].astype(F32)
        y_scr[:, h * RET_DV:(h + 1) * RET_DV] = (g * ron).astype(BF16)

    proj = jnp.dot(y_scr[...], wr_ref[...], preferred_element_type=F32)
    o_ref[0] = (gr_ref[0].astype(F32) * proj).astype(BF16)


def _retention(z3, wr, D):
    B, S, _ = z3.shape
    L = 256
    dmask, qdec, kdec, bdec = _ret_consts(L)
    QK = RET_HEADS * RET_DK
    V = RET_HEADS * RET_DV
    kern = functools.partial(_ret_kernel, bdec=bdec)
    full3 = lambda b, i: (0, 0, 0)
    return pl.pallas_call(
        kern,
        out_shape=jax.ShapeDtypeStruct((B, S, D), BF16),
        grid=(B, S // L),
        in_specs=[pl.BlockSpec((1, L, QK), lambda b, i: (b, i, 0)),
                  pl.BlockSpec((1, L, QK), lambda b, i: (b, i, 1)),
                  pl.BlockSpec((1, L, V), lambda b, i: (b, i, 1)),
                  pl.BlockSpec((1, L, V), lambda b, i: (b, i, 2)),
                  pl.BlockSpec((1, L, D), lambda b, i: (b, i, 5)),
                  pl.BlockSpec((RET_HEADS, L, L), full3),
                  pl.BlockSpec((RET_HEADS, L, 1), full3),
                  pl.BlockSpec((RET_HEADS, L, 1), full3),
                  pl.BlockSpec((V, D), lambda b, i: (0, 0))],
        out_specs=pl.BlockSpec((1, L, D), lambda b, i: (b, i, 0)),
        scratch_shapes=[pltpu.VMEM((RET_HEADS, RET_DK, RET_DV), F32),
                        pltpu.VMEM((L, V), BF16)],
        compiler_params=_cparams(("arbitrary", "arbitrary"), 40),
        name="ret",
    )(z3, z3, z3, z3, z3, dmask, qdec, kdec, wr)


FOX_CHUNK = 256
FOX_MAX_EXCESS = 64.0


def _fox_kernel(q_ref, qa_ref, k_ref, ka_ref, vt_ref, o_ref,
                kx_scr, vt_scr, qxt_scr, sa_scr, sb_scr, mba_scr, mbb_scr, pa_scr, pb_scr,
                m_scr, al_scr, acc_scr, ex_scr, *, tq, tk):
    h = pl.program_id(1)
    i = pl.program_id(2)
    nkb = vt_scr.shape[0]

    @pl.when(i == 0)
    def _():
        kx_scr[:, :FOX_DH] = k_ref[0]
        kx_scr[:, FOX_DH:] = ka_ref[0]
        for n in range(nkb):
            vt_scr[n, :FOX_DH, :] = vt_ref[0, :, n * tk:(n + 1) * tk]
            vt_scr[n, FOX_DH:, :] = jnp.ones((vt_scr.shape[1] - FOX_DH, tk), BF16)

    lane = lax.broadcasted_iota(jnp.int32, (tq, LANES), 1)
    qa = jnp.where((lane // AUG_STRIDE) == h, qa_ref[0].astype(F32), 0.0)
    qxt_scr[:FOX_DH, :] = q_ref[0].astype(F32).T.astype(BF16)
    qxt_scr[FOX_DH:, :] = qa.T.astype(BF16)

    chunks = [slice(c0, c0 + FOX_CHUNK) for c0 in range(0, tq, FOX_CHUNK)]

    def qk(blk, cs):
        off = pl.multiple_of(blk * tk, tk)
        return jnp.dot(kx_scr[pl.ds(off, tk), :], qxt_scr[:, cs],
                       preferred_element_type=F32)

    def causal(s, cs, key0):
        key = lax.broadcasted_iota(jnp.int32, s.shape, 0) + key0
        qry = lax.broadcasted_iota(jnp.int32, s.shape, 1) + cs.start
        return jnp.where(key <= qry, s, NEG_BIG)

    def emit():
        acc = acc_scr[...]
        o_ref[0] = (acc[:FOX_DH] / acc[FOX_DH:FOX_DH + 1]).T.astype(BF16)

    def pv_acc(blk_prev, p_prv, cs):
        acc_scr[:, cs] = al_scr[:, cs] * (acc_scr[:, cs] + jnp.dot(
            vt_scr[blk_prev], p_prv[:, cs], preferred_element_type=F32))

    def lazy_chunk(blk, cs, p_cur, key0):
        s = qk(blk, cs)
        if key0 is not None:
            s = causal(s, cs, key0)
        mb = jnp.max(s, axis=0, keepdims=True)
        r_old = m_scr[:, cs]
        p_cur[:, cs] = jnp.exp2((s - r_old).astype(BF16))
        r_new = jnp.maximum(r_old, mb)
        m_scr[:, cs] = r_new
        ex_scr[:, cs] = jnp.maximum(ex_scr[:, cs], mb - r_old)
        return jnp.exp2(r_old - r_new)

    def lazy_step(blk, blk_prev, p_cur, p_prv):
        for cs in chunks:
            alpha = lazy_chunk(blk, cs, p_cur, None)
            pv_acc(blk_prev, p_prv, cs)
            al_scr[:, cs] = alpha

    def own_block(cs):
        return (2 * i, 0) if cs.start < tk else (2 * i + 1, tk)

    for cs in chunks:
        blk, key0 = own_block(cs)
        s = causal(qk(blk, cs), cs, key0)
        mb = jnp.max(s, axis=0, keepdims=True)
        m_scr[:, cs] = mb
        pa_scr[:, cs] = jnp.exp2((s - mb).astype(BF16))
        al_scr[:, cs] = jnp.ones((1, FOX_CHUNK), F32)
        acc_scr[:, cs] = jnp.zeros((acc_scr.shape[0], FOX_CHUNK), F32)
        ex_scr[:, cs] = jnp.zeros((1, FOX_CHUNK), F32)
    for cs in chunks:
        blk, _ = own_block(cs)
        if cs.start < tk:
            pb_scr[:, cs] = jnp.zeros((tk, FOX_CHUNK), BF16)
            pv_acc(blk, pa_scr, cs)
        else:
            alpha = lazy_chunk(2 * i, cs, pb_scr, None)
            pv_acc(blk, pa_scr, cs)
            al_scr[:, cs] = alpha

    def lazy_body(t, carry):
        jj = i - 1 - t
        lazy_step(2 * jj + 1, 2 * jj + 2, pa_scr, pb_scr)
        lazy_step(2 * jj, 2 * jj + 1, pb_scr, pa_scr)
        return carry

    lax.fori_loop(0, i, lazy_body, 0)
    for cs in chunks:
        pv_acc(0, pb_scr, cs)
    emit()

    def scores(blk, s_out, mb_out, cs):
        s = qk(blk, cs)
        s_out[:, cs] = s
        if mb_out is not None:
            mb_out[:, cs] = jnp.max(s, axis=0, keepdims=True)

    def softmax(s_in, mb_in, p_out, key0, cs):
        s = s_in[:, cs]
        if key0 is None:
            mb = mb_in[:, cs]
        else:
            s = causal(s, cs, key0)
            mb = jnp.max(s, axis=0, keepdims=True)
        m_prev = m_scr[:, cs]
        m_new = jnp.maximum(m_prev, mb)
        p_out[:, cs] = jnp.exp2((s - m_new).astype(BF16))
        m_scr[:, cs] = m_new
        return jnp.exp2(m_prev - m_new)

    def pv(blk_prev, p_in, cs):
        acc_scr[:, cs] = al_scr[:, cs] * acc_scr[:, cs] + jnp.dot(
            vt_scr[blk_prev], p_in[:, cs], preferred_element_type=F32)

    def step(n, cur, nxt, p_prv, key0):
        s_cur, mb_cur, p_cur = cur
        for cs in chunks:
            if nxt is not None:
                scores(n + 1, nxt[0], nxt[1], cs)
            alpha = softmax(s_cur, mb_cur, p_cur, key0, cs)
            pv(jnp.maximum(n - 1, 0), p_prv, cs)
            al_scr[:, cs] = alpha

    buf_a = (sa_scr, mba_scr, pa_scr)
    buf_b = (sb_scr, mbb_scr, pb_scr)

    def body(jj, carry):
        step(2 * jj, buf_a, buf_b, pb_scr, None)
        step(2 * jj + 1, buf_b, buf_a, pa_scr, None)
        return carry

    @pl.when(jnp.max(ex_scr[...]) > FOX_MAX_EXCESS)
    def _():
        m_scr[...] = jnp.full(m_scr.shape, NEG_BIG, F32)
        al_scr[...] = jnp.ones_like(al_scr)
        acc_scr[...] = jnp.zeros_like(acc_scr)
        pb_scr[...] = jnp.zeros_like(pb_scr)
        for cs in chunks:
            scores(0, sa_scr, mba_scr, cs)
        lax.fori_loop(0, i, body, 0)
        step(2 * i, buf_a, (sb_scr, None), pb_scr, 0)
        step(2 * i + 1, buf_b, None, pa_scr, tk)
        for cs in chunks:
            pv(2 * i + 1, pb_scr, cs)
        emit()


def _fox(z3, qa, ka, vt):
    B, S, _ = z3.shape
    tq, tk = 1024, 512
    assert tq == 2 * tk
    ones_rows = 16
    W = FOX_HEADS * FOX_DH
    qb, kb = Z_COLS["fq"] // FOX_DH, Z_COLS["fk"] // FOX_DH
    kern = functools.partial(_fox_kernel, tq=tq, tk=tk)
    return pl.pallas_call(
        kern,
        out_shape=jax.ShapeDtypeStruct((B, S, W), BF16),
        grid=(B, FOX_HEADS, S // tq),
        in_specs=[pl.BlockSpec((1, tq, FOX_DH), lambda b, h, i: (b, i, qb + h)),
                  pl.BlockSpec((1, tq, LANES), lambda b, h, i: (b, i, 0)),
                  pl.BlockSpec((1, S, FOX_DH), lambda b, h, i: (b, 0, kb + h)),
                  pl.BlockSpec((1, S, LANES), lambda b, h, i: (b, 0, 0)),
                  pl.BlockSpec((1, FOX_DH, S), lambda b, h, i: (b, h, 0))],
        out_specs=pl.BlockSpec((1, tq, FOX_DH), lambda b, h, i: (b, i, h)),
        scratch_shapes=[pltpu.VMEM((S, 2 * FOX_DH), BF16),
                        pltpu.VMEM((S // tk, FOX_DH + ones_rows, tk), BF16),
                        pltpu.VMEM((2 * FOX_DH, tq), BF16),
                        pltpu.VMEM((tk, tq), F32),
                        pltpu.VMEM((tk, tq), F32),
                        pltpu.VMEM((1, tq), F32),
                        pltpu.VMEM((1, tq), F32),
                        pltpu.VMEM((tk, tq), BF16),
                        pltpu.VMEM((tk, tq), BF16),
                        pltpu.VMEM((1, tq), F32),
                        pltpu.VMEM((1, tq), F32),
                        pltpu.VMEM((FOX_DH + ones_rows, tq), F32),
                        pltpu.VMEM((1, tq), F32)],
        compiler_params=_cparams(("arbitrary", "arbitrary", "arbitrary"), 48),
        name="fox",
    )(z3, qa, z3, ka, vt)


FFN_CHUNK = 256


def _tail_kernel(a_ref, yf_ref, gf_ref, x_ref, g1_ref, sc_ref, sh_ref, g2_ref, n2_ref, nf_ref,
                 wf_ref, wo_ref, wg_ref, wu_ref, wd_ref, o_ref, act_scr):
    fo = jnp.dot(yf_ref[...], wf_ref[...], preferred_element_type=F32)
    merged = a_ref[...].astype(F32) + gf_ref[...].astype(F32) * fo
    out = jnp.dot(merged.astype(BF16), wo_ref[...], preferred_element_type=F32)
    x1 = x_ref[...] + g1_ref[0] * out

    ms = jnp.mean(x1 * x1, axis=-1, keepdims=True)
    y = x1 * lax.rsqrt(ms + EPS) * n2_ref[...]
    h = (y * (1.0 + sc_ref[0]) + sh_ref[0]).astype(BF16)
    for c0 in range(0, wg_ref.shape[1], FFN_CHUNK):
        g = jnp.dot(h, wg_ref[:, c0:c0 + FFN_CHUNK], preferred_element_type=F32)
        u = jnp.dot(h, wu_ref[:, c0:c0 + FFN_CHUNK], preferred_element_type=F32)
        act_scr[:, c0:c0 + FFN_CHUNK] = (g * jax.nn.sigmoid(g) * u).astype(BF16)
    ffn = jnp.dot(act_scr[...], wd_ref[...], preferred_element_type=F32)

    x2 = x1 + g2_ref[0] * ffn
    ms2 = jnp.mean(x2 * x2, axis=-1, keepdims=True)
    o_ref[...] = x2 * lax.rsqrt(ms2 + EPS) * nf_ref[...]


def _tail(a2, yf2, z2, x2, g1, sc2, sh2, g2, n2, nf, wf, wo, wg, wu, wd, S):
    T, D = x2.shape
    FF = wg.shape[1]
    assert FF % FFN_CHUNK == 0
    tm = 512
    spb = S // tm
    row = lambda i: (i, 0)
    bidx = lambda i: (i // spb, 0, 0)
    const = lambda shape: pl.BlockSpec(shape, lambda i: (0,) * len(shape),
                                       pipeline_mode=pl.Buffered(1))
    return pl.pallas_call(
        _tail_kernel,
        out_shape=jax.ShapeDtypeStruct((T, D), F32),
        grid=(T // tm,),
        in_specs=[pl.BlockSpec((tm, D), row),
                  pl.BlockSpec((tm, D), row),
                  pl.BlockSpec((tm, D), lambda i: (i, Z_COLS["gf"] // D)),
                  pl.BlockSpec((tm, D), row),
                  pl.BlockSpec((1, 1, D), bidx),
                  pl.BlockSpec((1, 1, D), bidx),
                  pl.BlockSpec((1, 1, D), bidx),
                  pl.BlockSpec((1, 1, D), bidx),
                  const((1, D)),
                  const((1, D)),
                  const((D, D)),
                  const((D, D)),
                  const((D, FF)),
                  const((D, FF)),
                  const((FF, D))],
        out_specs=pl.BlockSpec((tm, D), row),
        scratch_shapes=[pltpu.VMEM((tm, FF), BF16)],
        compiler_params=_cparams(("arbitrary",), 58),
        name="tail",
    )(a2, yf2, z2, x2, g1, sc2, sh2, g2, n2, nf, wf, wo, wg, wu, wd)


def _rope_tables(S):
    half = RET_DK // 2
    freqs = ROPE_THETA ** (-np.arange(half, dtype=np.float64) / half)
    ang = np.arange(S, dtype=np.float64)[:, None] * freqs[None, :]
    cos, sin = np.cos(ang), np.sin(ang)
    return (jnp.asarray(np.concatenate([cos, cos], axis=1), F32),
            jnp.asarray(np.concatenate([-sin, sin], axis=1), F32))


def _aug_lanes(v8):
    rep = jnp.repeat(v8[..., None], 6, axis=-1)
    pad = jnp.zeros(v8.shape + (AUG_STRIDE - 6,), v8.dtype)
    return jnp.concatenate([rep, pad], axis=-1).reshape(v8.shape[:-1] + (LANES,))


def _layer(x, mod, norm1_w, w_in, b_f, ret_proj, fox_proj, w_out, norm2_w,
           w_gate, w_up, w_down, norm_out_w, tables):
    B, S, D = x.shape
    T = B * S
    sh1, sc1, g1, sh2, sc2, g2 = [m.reshape(B, 1, D) for m in jnp.split(mod, 6, axis=-1)]
    offs = np.cumsum([0, RET_HEADS * RET_DK, RET_HEADS * RET_DK, RET_HEADS * RET_DV,
                      RET_HEADS * RET_DV, FOX_HEADS * FOX_DH, FOX_HEADS * FOX_DH,
                      FOX_HEADS * FOX_DH, FOX_HEADS, D, D])
    o_fv, o_ff, o_gr = int(offs[6]), int(offs[7]), int(offs[8])
    w_a = w_in[:, :o_fv].astype(BF16)
    w_b = w_in[:, o_gr:].astype(BF16)
    w_vt = w_in[:, o_fv:o_ff].T.astype(BF16)
    w_ff = _aug_lanes(w_in[:, o_ff:o_gr]).astype(BF16)
    bf_l = _aug_lanes(b_f).reshape(1, LANES)

    x2 = x.reshape(T, D)
    z, vt, qa, ka = _inproj(x2, norm1_w.reshape(1, D), sc1, sh1, tables[0], tables[1],
                            w_a, w_b, w_vt, w_ff, bf_l, B, S)
    z3 = z.reshape(B, S, z.shape[1])
    a = _retention(z3, ret_proj.astype(BF16), D)
    yf = _fox(z3, qa, ka, vt)
    out = _tail(a.reshape(T, D), yf.reshape(T, D), z, x2, g1, sc2, sh2, g2,
                norm2_w.reshape(1, D), norm_out_w.reshape(1, D),
                fox_proj.astype(BF16), w_out.astype(BF16), w_gate.astype(BF16),
                w_up.astype(BF16), w_down.astype(BF16), S)
    return out.reshape(B, S, D)


def kernel(x, c, ada_w, ada_b, norm1_w, w_in, b_f, ret_proj, fox_proj, w_out,
           norm2_w, w_gate, w_up, w_down, norm_f_w):
    depth = ada_w.shape[0]
    assert depth == 1, "the final RMSNorm is fused into the last layer's channel mixer"
    tables = _rope_tables(x.shape[1])
    l = 0
    mod = _ada(c, ada_w[l], ada_b[l])
    return _layer(x, mod, norm1_w[l], w_in[l], b_f[l], ret_proj[l], fox_proj[l], w_out[l],
                  norm2_w[l], w_gate[l], w_up[l], w_down[l], norm_f_w, tables)
```

```python
import functools
import math

import numpy as np
import jax
import jax.numpy as jnp
from jax import lax
from jax.experimental import pallas as pl
from jax.experimental.pallas import tpu as pltpu

F32 = jnp.float32
BF16 = jnp.bfloat16

EPS = 1e-6
ROPE_THETA = 10000.0
RET_HEADS = 4
RET_DK = 128
RET_DV = 256
FOX_HEADS = 8
FOX_DH = 128
RET_CHUNK = 64
LANES = 128
AUG_STRIDE = 16

NEG_BIG = -1e30
LOG2E = math.log2(math.e)

_NT = (((1,), (1,)), ((), ()))
_TN = (((0,), (0,)), ((), ()))


def _cparams(sem, vmem_mb, flags=None):
    return pltpu.CompilerParams(dimension_semantics=sem,
                                vmem_limit_bytes=vmem_mb * 1024 * 1024, flags=flags)


def _ada_kernel(ct_ref, w_ref, b_ref, o_ref):
    ct = ct_ref[...]
    act = ct * jax.nn.sigmoid(ct)
    w = w_ref[...]
    for b in range(o_ref.shape[0]):
        o_ref[b:b + 1, :] = (jnp.sum(act[:, b:b + 1] * w, axis=0, keepdims=True)
                             + b_ref[...])


def _ada(c, w, b):
    B, D = c.shape
    N = w.shape[1]
    tn = 1536
    return pl.pallas_call(
        _ada_kernel,
        out_shape=jax.ShapeDtypeStruct((B, N), F32),
        grid=(N // tn,),
        in_specs=[pl.BlockSpec((D, B), lambda j: (0, 0)),
                  pl.BlockSpec((D, tn), lambda j: (0, j)),
                  pl.BlockSpec((1, tn), lambda j: (0, j))],
        out_specs=pl.BlockSpec((B, tn), lambda j: (0, j)),
        compiler_params=_cparams(("arbitrary",), 40),
        name="ada",
    )(c.T, w, b.reshape(1, N))


INPROJ_CHUNK = 512
Z_COLS = {"rq": 0, "rk": 512, "rv": 1024, "rg": 2048, "fq": 3072, "fk": 4096,
          "gr": 5120, "gf": 6144, "end": 7168}


def _split3(v):
    p0 = v.astype(BF16)
    r1 = v - p0.astype(F32)
    p1 = r1.astype(BF16)
    r2 = r1 - p1.astype(F32)
    p2 = r2.astype(BF16)
    return p0, p1, p2


def _inproj_kernel(x0_ref, xn_ref, nw_ref, sc0_ref, sh0_ref, scn_ref, shn_ref, cos_ref, sin_ref,
                   wa_ref, wb_ref, wvt_ref, wff_ref, bf_ref, tri_ref,
                   z_ref, vt_ref, qa_ref, ka_ref, h_scr, carry, *, fox_scale, rk_scale, spb):
    i = pl.program_id(0)

    def normed(x_ref, sc_ref, sh_ref):
        x = x_ref[...]
        ms = jnp.mean(x * x, axis=-1, keepdims=True)
        y = x * lax.rsqrt(ms + EPS) * nw_ref[...]
        return (y * (1.0 + sc_ref[0]) + sh_ref[0]).astype(BF16)

    @pl.when(i == 0)
    def _():
        h_scr[0] = normed(x0_ref, sc0_ref, sh0_ref)

    h = h_scr[i % 2]
    cos = cos_ref[...]
    sin = sin_ref[...]
    cw = INPROJ_CHUNK

    def chunk(c0):
        na = wa_ref.shape[1]
        w = wa_ref[:, c0:c0 + cw] if c0 < na else wb_ref[:, c0 - na:c0 - na + cw]
        return jnp.dot(h, w, preferred_element_type=F32)

    def rotary(c0, scale):
        acc = chunk(c0)
        for hh in range(cw // RET_DK):
            a = acc[:, hh * RET_DK:(hh + 1) * RET_DK]
            o = a * cos + pltpu.roll(a, RET_DK // 2, axis=1) * sin
            if scale is not None:
                o = o * scale
            z_ref[:, c0 + hh * RET_DK:c0 + (hh + 1) * RET_DK] = o.astype(BF16)

    def group(name, nxt, fn):
        for c0 in range(Z_COLS[name], Z_COLS[nxt], cw):
            z_ref[:, c0:c0 + cw] = fn(chunk(c0)).astype(BF16)

    h_next = normed(xn_ref, scn_ref, shn_ref)

    xv = jnp.dot(h, wff_ref[...], preferred_element_type=F32) + bf_ref[...]
    lf = jnp.minimum(xv, 0.0) - jnp.log(1.0 + jnp.exp(-jnp.abs(xv)))
    p0, p1, p2 = _split3(lf)

    rotary(Z_COLS["rq"], None)
    rotary(Z_COLS["rk"], rk_scale)

    tri = tri_ref[...]
    cs = (jnp.dot(tri, p0, preferred_element_type=F32)
          + jnp.dot(tri, p1, preferred_element_type=F32)
          + jnp.dot(tri, p2, preferred_element_type=F32))
    first = (i % spb) == 0
    fc = cs + jnp.where(first, 0.0, carry[...])
    tm = fc.shape[0]
    carry[...] = fc[tm - 1:tm, :]

    group("rv", "rg", lambda a: a)

    f0, f1, f2 = [p.astype(F32) for p in _split3(fc * LOG2E)]
    c = lax.broadcasted_iota(jnp.int32, fc.shape, 1) % AUG_STRIDE
    qa_ref[0] = jnp.where(c == 0, f0, jnp.where(c == 1, f1, jnp.where(
        c == 2, f2, jnp.where(c < 6, 1.0, 0.0)))).astype(BF16)
    ka_ref[0] = jnp.where(c < 3, 1.0, jnp.where(c == 3, -f0, jnp.where(
        c == 4, -f1, jnp.where(c == 5, -f2, 0.0)))).astype(BF16)

    group("fk", "gr", lambda a: a)
    group("rg", "fq", lambda a: a * jax.nn.sigmoid(a))
    for r0 in range(0, wvt_ref.shape[0], cw):
        vt_ref[0, r0:r0 + cw, :] = lax.dot_general(
            wvt_ref[r0:r0 + cw, :], h, _NT, preferred_element_type=F32).astype(BF16)
    group("gr", "end", jax.nn.sigmoid)
    group("fq", "fk", lambda a: a * fox_scale)
    h_scr[(i + 1) % 2] = h_next


def _inproj(x2, nw, sc, sh, cos_t, sin_t, wa, wb, wvt, wff, bf_l, B, S):
    T, D = x2.shape
    NA, NB = wa.shape[1], wb.shape[1]
    N = NA + NB
    assert N == Z_COLS["end"] and NA % INPROJ_CHUNK == 0
    NV = wvt.shape[0]
    tm = 512
    spb = S // tm
    tri = jnp.asarray(np.tril(np.ones((tm, tm), np.float32)), BF16)
    kern = functools.partial(_inproj_kernel, fox_scale=LOG2E / math.sqrt(FOX_DH),
                             rk_scale=RET_DK ** -0.5, spb=spb)
    const = lambda shape: pl.BlockSpec(shape, lambda i: (0,) * len(shape),
                                       pipeline_mode=pl.Buffered(1))
    nxt = lambda i: jnp.minimum(i + 1, T // tm - 1)
    return pl.pallas_call(
        kern,
        out_shape=(jax.ShapeDtypeStruct((T, N), BF16),
                   jax.ShapeDtypeStruct((B, NV, S), BF16),
                   jax.ShapeDtypeStruct((B, S, LANES), BF16),
                   jax.ShapeDtypeStruct((B, S, LANES), BF16)),
        grid=(T // tm,),
        in_specs=[const((tm, D)),
                  pl.BlockSpec((tm, D), lambda i: (nxt(i), 0)),
                  const((1, D)),
                  const((1, 1, D)),
                  const((1, 1, D)),
                  pl.BlockSpec((1, 1, D), lambda i: (nxt(i) // spb, 0, 0)),
                  pl.BlockSpec((1, 1, D), lambda i: (nxt(i) // spb, 0, 0)),
                  pl.BlockSpec((tm, LANES), lambda i: (i % spb, 0)),
                  pl.BlockSpec((tm, LANES), lambda i: (i % spb, 0)),
                  const((D, NA)),
                  const((D, NB)),
                  const((NV, D)),
                  const((D, LANES)),
                  const((1, LANES)),
                  const((tm, tm))],
        out_specs=(pl.BlockSpec((tm, N), lambda i: (i, 0)),
                   pl.BlockSpec((1, NV, tm), lambda i: (i // spb, 0, i % spb)),
                   pl.BlockSpec((1, tm, LANES), lambda i: (i // spb, i % spb, 0)),
                   pl.BlockSpec((1, tm, LANES), lambda i: (i // spb, i % spb, 0))),
        scratch_shapes=[pltpu.VMEM((2, tm, D), BF16), pltpu.VMEM((1, LANES), F32)],
        compiler_params=_cparams(("arbitrary",), 56),
        name="inproj",
    )(x2, x2, nw, sc, sh, sc, sh, cos_t, sin_t, wa, wb, wvt, wff, bf_l, tri)


def _ret_consts(L):
    hs = np.arange(RET_HEADS, dtype=np.float64)
    log_gamma = np.log(1.0 - np.exp2(-5.0 - hs))
    idx = np.arange(L, dtype=np.float64)
    dist = np.abs(idx[:, None] - idx[None, :])
    chunk_ok = (idx[None, :] // RET_CHUNK) <= (idx[:, None] // RET_CHUNK)
    dmask = np.exp(log_gamma[:, None, None] * dist[None]) * chunk_ok[None]
    qdec = np.exp(log_gamma[:, None] * idx[None, :])[..., None]
    kdec = np.exp(log_gamma[:, None] * (L - idx)[None, :])[..., None]
    bdec = [float(v) for v in np.exp(log_gamma * L)]
    return (jnp.asarray(dmask, F32), jnp.asarray(qdec, F32), jnp.asarray(kdec, F32), bdec)


def _ret_kernel(rq_ref, rk_ref, rv_ref, rg_ref, gr_ref, dm_ref, qd_ref, kd_ref, wr_ref,
                o_ref, st_scr, y_scr, *, bdec):
    @pl.when(pl.program_id(1) == 0)
    def _():
        st_scr[...] = jnp.zeros_like(st_scr)

    for h in range(RET_HEADS):
        q = rq_ref[0, :, h * RET_DK:(h + 1) * RET_DK]
        k = rk_ref[0, :, h * RET_DK:(h + 1) * RET_DK]
        v = rv_ref[0, :, h * RET_DV:(h + 1) * RET_DV]
        s = lax.dot_general(q, k, _NT, preferred_element_type=F32) * dm_ref[h]
        intra = jnp.dot(s.astype(BF16), v, preferred_element_type=F32)
        st = st_scr[h]
        inter = jnp.dot(q, st.astype(BF16), preferred_element_type=F32) * qd_ref[h]
        ks = (k.astype(F32) * kd_ref[h]).astype(BF16)
        st_scr[h] = bdec[h] * st + lax.dot_general(ks, v, _TN, preferred_element_type=F32)
        ro = intra + inter
        ron = ro * lax.rsqrt(jnp.mean(ro * ro, axis=-1, keepdims=True) + EPS)
        g = rg_ref[0, :, h * RET_DV:(h + 1) * RET_DV].astype(F32)
        y_scr[:, h * RET_DV:(h + 1) * RET_DV] = (g * ron).astype(BF16)

    proj = jnp.dot(y_scr[...], wr_ref[...], preferred_element_type=F32)
    o_ref[0] = (gr_ref[0].astype(F32) * proj).astype(BF16)


def _retention(z3, wr, D):
    B, S, _ = z3.shape
    L = 512
    dmask, qdec, kdec, bdec = _ret_consts(L)
    QK = RET_HEADS * RET_DK
    V = RET_HEADS * RET_DV
    kern = functools.partial(_ret_kernel, bdec=bdec)
    full3 = lambda b, i: (0, 0, 0)
    return pl.pallas_call(
        kern,
        out_shape=jax.ShapeDtypeStruct((B, S, D), BF16),
        grid=(B, S // L),
        in_specs=[pl.BlockSpec((1, L, QK), lambda b, i: (b, i, 0)),
                  pl.BlockSpec((1, L, QK), lambda b, i: (b, i, 1)),
                  pl.BlockSpec((1, L, V), lambda b, i: (b, i, 1)),
                  pl.BlockSpec((1, L, V), lambda b, i: (b, i, 2)),
                  pl.BlockSpec((1, L, D), lambda b, i: (b, i, 5)),
                  pl.BlockSpec((RET_HEADS, L, L), full3),
                  pl.BlockSpec((RET_HEADS, L, 1), full3),
                  pl.BlockSpec((RET_HEADS, L, 1), full3),
                  pl.BlockSpec((V, D), lambda b, i: (0, 0))],
        out_specs=pl.BlockSpec((1, L, D), lambda b, i: (b, i, 0)),
        scratch_shapes=[pltpu.VMEM((RET_HEADS, RET_DK, RET_DV), F32),
                        pltpu.VMEM((L, V), BF16)],
        compiler_params=_cparams(("arbitrary", "arbitrary"), 40),
        name="ret",
    )(z3, z3, z3, z3, z3, dmask, qdec, kdec, wr)


FOX_CHUNK = 256
FOX_MAX_EXCESS = 64.0


def _fox_kernel(q_ref, qa_ref, k_ref, ka_ref, vt_ref, o_ref,
                kx_scr, vt_scr, qxt_scr, sa_scr, sb_scr, mba_scr, mbb_scr, pa_scr, pb_scr,
                m_scr, al_scr, acc_scr, ex_scr, *, tq, tk):
    h = pl.program_id(1)
    i = pl.program_id(2)
    nkb = vt_scr.shape[0]

    @pl.when(i == 0)
    def _():
        kx_scr[:, :FOX_DH] = k_ref[0]
        kx_scr[:, FOX_DH:] = ka_ref[0]
        for n in range(nkb):
            vt_scr[n, :FOX_DH, :] = vt_ref[0, :, n * tk:(n + 1) * tk]
            vt_scr[n, FOX_DH:, :] = jnp.ones((vt_scr.shape[1] - FOX_DH, tk), BF16)

    lane = lax.broadcasted_iota(jnp.int32, (tq, LANES), 1)
    qa = jnp.where((lane // AUG_STRIDE) == h, qa_ref[0].astype(F32), 0.0)
    qxt_scr[:FOX_DH, :] = q_ref[0].astype(F32).T.astype(BF16)
    qxt_scr[FOX_DH:, :] = qa.T.astype(BF16)

    chunks = [slice(c0, c0 + FOX_CHUNK) for c0 in range(0, tq, FOX_CHUNK)]

    def qk(blk, cs):
        off = pl.multiple_of(blk * tk, tk)
        return jnp.dot(kx_scr[pl.ds(off, tk), :], qxt_scr[:, cs],
                       preferred_element_type=F32)

    def causal(s, cs, key0):
        key = lax.broadcasted_iota(jnp.int32, s.shape, 0) + key0
        qry = lax.broadcasted_iota(jnp.int32, s.shape, 1) + cs.start
        return jnp.where(key <= qry, s, NEG_BIG)

    def emit():
        acc = acc_scr[...]
        o_ref[0] = (acc[:FOX_DH] / acc[FOX_DH:FOX_DH + 1]).astype(BF16)

    def pv_acc(blk_prev, p_prv, cs):
        acc_scr[:, cs] = al_scr[:, cs] * (acc_scr[:, cs] + jnp.dot(
            vt_scr[blk_prev], p_prv[:, cs], preferred_element_type=F32))

    def lazy_chunk(blk, cs, p_cur, key0):
        s = qk(blk, cs)
        if key0 is not None:
            s = causal(s, cs, key0)
        mb = jnp.max(s, axis=0, keepdims=True)
        r_old = m_scr[:, cs]
        p_cur[:, cs] = jnp.exp2((s - r_old).astype(BF16))
        r_new = jnp.maximum(r_old, mb)
        m_scr[:, cs] = r_new
        ex_scr[:, cs] = jnp.maximum(ex_scr[:, cs], mb - r_old)
        return jnp.exp2(r_old - r_new)

    def lazy_step(blk, blk_prev, p_cur, p_prv):
        for cs in chunks:
            alpha = lazy_chunk(blk, cs, p_cur, None)
            pv_acc(blk_prev, p_prv, cs)
            al_scr[:, cs] = alpha

    def own_block(cs):
        return (2 * i, 0) if cs.start < tk else (2 * i + 1, tk)

    for cs in chunks:
        blk, key0 = own_block(cs)
        s = causal(qk(blk, cs), cs, key0)
        mb = jnp.max(s, axis=0, keepdims=True)
        m_scr[:, cs] = mb
        pa_scr[:, cs] = jnp.exp2((s - mb).astype(BF16))
        al_scr[:, cs] = jnp.ones((1, FOX_CHUNK), F32)
        acc_scr[:, cs] = jnp.zeros((acc_scr.shape[0], FOX_CHUNK), F32)
        ex_scr[:, cs] = jnp.zeros((1, FOX_CHUNK), F32)
    for cs in chunks:
        blk, _ = own_block(cs)
        if cs.start < tk:
            pb_scr[:, cs] = jnp.zeros((tk, FOX_CHUNK), BF16)
            pv_acc(blk, pa_scr, cs)
        else:
            alpha = lazy_chunk(2 * i, cs, pb_scr, None)
            pv_acc(blk, pa_scr, cs)
            al_scr[:, cs] = alpha

    def lazy_body(t, carry):
        jj = i - 1 - t
        lazy_step(2 * jj + 1, 2 * jj + 2, pa_scr, pb_scr)
        lazy_step(2 * jj, 2 * jj + 1, pb_scr, pa_scr)
        return carry

    lax.fori_loop(0, i, lazy_body, 0)
    for cs in chunks:
        pv_acc(0, pb_scr, cs)
    emit()

    def scores(blk, s_out, mb_out, cs):
        s = qk(blk, cs)
        s_out[:, cs] = s
        if mb_out is not None:
            mb_out[:, cs] = jnp.max(s, axis=0, keepdims=True)

    def softmax(s_in, mb_in, p_out, key0, cs):
        s = s_in[:, cs]
        if key0 is None:
            mb = mb_in[:, cs]
        else:
            s = causal(s, cs, key0)
            mb = jnp.max(s, axis=0, keepdims=True)
        m_prev = m_scr[:, cs]
        m_new = jnp.maximum(m_prev, mb)
        p_out[:, cs] = jnp.exp2((s - m_new).astype(BF16))
        m_scr[:, cs] = m_new
        return jnp.exp2(m_prev - m_new)

    def pv(blk_prev, p_in, cs):
        acc_scr[:, cs] = al_scr[:, cs] * acc_scr[:, cs] + jnp.dot(
            vt_scr[blk_prev], p_in[:, cs], preferred_element_type=F32)

    def step(n, cur, nxt, p_prv, key0):
        s_cur, mb_cur, p_cur = cur
        for cs in chunks:
            if nxt is not None:
                scores(n + 1, nxt[0], nxt[1], cs)
            alpha = softmax(s_cur, mb_cur, p_cur, key0, cs)
            pv(jnp.maximum(n - 1, 0), p_prv, cs)
            al_scr[:, cs] = alpha

    buf_a = (sa_scr, mba_scr, pa_scr)
    buf_b = (sb_scr, mbb_scr, pb_scr)

    def body(jj, carry):
        step(2 * jj, buf_a, buf_b, pb_scr, None)
        step(2 * jj + 1, buf_b, buf_a, pa_scr, None)
        return carry

    @pl.when(jnp.max(ex_scr[...]) > FOX_MAX_EXCESS)
    def _():
        m_scr[...] = jnp.full(m_scr.shape, NEG_BIG, F32)
        al_scr[...] = jnp.ones_like(al_scr)
        acc_scr[...] = jnp.zeros_like(acc_scr)
        pb_scr[...] = jnp.zeros_like(pb_scr)
        for cs in chunks:
            scores(0, sa_scr, mba_scr, cs)
        lax.fori_loop(0, i, body, 0)
        step(2 * i, buf_a, (sb_scr, None), pb_scr, 0)
        step(2 * i + 1, buf_b, None, pa_scr, tk)
        for cs in chunks:
            pv(2 * i + 1, pb_scr, cs)
        emit()


def _fox(z3, qa, ka, vt):
    B, S, _ = z3.shape
    tq, tk = 1024, 512
    assert tq == 2 * tk
    ones_rows = 16
    W = FOX_HEADS * FOX_DH
    qb, kb = Z_COLS["fq"] // FOX_DH, Z_COLS["fk"] // FOX_DH
    kern = functools.partial(_fox_kernel, tq=tq, tk=tk)
    return pl.pallas_call(
        kern,
        out_shape=jax.ShapeDtypeStruct((B, W, S), BF16),
        grid=(B, FOX_HEADS, S // tq),
        in_specs=[pl.BlockSpec((1, tq, FOX_DH), lambda b, h, i: (b, i, qb + h)),
                  pl.BlockSpec((1, tq, LANES), lambda b, h, i: (b, i, 0)),
                  pl.BlockSpec((1, S, FOX_DH), lambda b, h, i: (b, 0, kb + h)),
                  pl.BlockSpec((1, S, LANES), lambda b, h, i: (b, 0, 0)),
                  pl.BlockSpec((1, FOX_DH, S), lambda b, h, i: (b, h, 0))],
        out_specs=pl.BlockSpec((1, FOX_DH, tq), lambda b, h, i: (b, h, i)),
        scratch_shapes=[pltpu.VMEM((S, 2 * FOX_DH), BF16),
                        pltpu.VMEM((S // tk, FOX_DH + ones_rows, tk), BF16),
                        pltpu.VMEM((2 * FOX_DH, tq), BF16),
                        pltpu.VMEM((tk, tq), F32),
                        pltpu.VMEM((tk, tq), F32),
                        pltpu.VMEM((1, tq), F32),
                        pltpu.VMEM((1, tq), F32),
                        pltpu.VMEM((tk, tq), BF16),
                        pltpu.VMEM((tk, tq), BF16),
                        pltpu.VMEM((1, tq), F32),
                        pltpu.VMEM((1, tq), F32),
                        pltpu.VMEM((FOX_DH + ones_rows, tq), F32),
                        pltpu.VMEM((1, tq), F32)],
        compiler_params=_cparams(("arbitrary", "arbitrary", "arbitrary"), 48),
        name="fox",
    )(z3, qa, z3, ka, vt)


FFN_CHUNK = 256


def _tail_kernel(a_ref, yft_ref, gf_ref, x_ref, g1_ref, sc_ref, sh_ref, g2_ref, n2_ref, nf_ref,
                 wf_ref, wo_ref, wg_ref, wu_ref, wd_ref, o_ref, act_scr):
    fo = lax.dot_general(yft_ref[0], wf_ref[...], _TN, preferred_element_type=F32)
    merged = a_ref[...].astype(F32) + gf_ref[...].astype(F32) * fo
    out = jnp.dot(merged.astype(BF16), wo_ref[...], preferred_element_type=F32)
    x1 = x_ref[...] + g1_ref[0] * out

    ms = jnp.mean(x1 * x1, axis=-1, keepdims=True)
    y = x1 * lax.rsqrt(ms + EPS) * n2_ref[...]
    h = (y * (1.0 + sc_ref[0]) + sh_ref[0]).astype(BF16)
    for c0 in range(0, wg_ref.shape[1], FFN_CHUNK):
        g = jnp.dot(h, wg_ref[:, c0:c0 + FFN_CHUNK], preferred_element_type=F32)
        u = jnp.dot(h, wu_ref[:, c0:c0 + FFN_CHUNK], preferred_element_type=F32)
        act_scr[:, c0:c0 + FFN_CHUNK] = (g * jax.nn.sigmoid(g) * u).astype(BF16)
    ffn = jnp.dot(act_scr[...], wd_ref[...], preferred_element_type=F32)

    x2 = x1 + g2_ref[0] * ffn
    ms2 = jnp.mean(x2 * x2, axis=-1, keepdims=True)
    o_ref[...] = x2 * lax.rsqrt(ms2 + EPS) * nf_ref[...]


def _tail(a2, yft, z2, x2, g1, sc2, sh2, g2, n2, nf, wf, wo, wg, wu, wd, S):
    T, D = x2.shape
    FF = wg.shape[1]
    assert FF % FFN_CHUNK == 0
    tm = 512
    spb = S // tm
    row = lambda i: (i, 0)
    bidx = lambda i: (i // spb, 0, 0)
    const = lambda shape: pl.BlockSpec(shape, lambda i: (0,) * len(shape),
                                       pipeline_mode=pl.Buffered(1))
    return pl.pallas_call(
        _tail_kernel,
        out_shape=jax.ShapeDtypeStruct((T, D), F32),
        grid=(T // tm,),
        in_specs=[pl.BlockSpec((tm, D), row),
                  pl.BlockSpec((1, D, tm), lambda i: (i // spb, 0, i % spb)),
                  pl.BlockSpec((tm, D), lambda i: (i, Z_COLS["gf"] // D)),
                  pl.BlockSpec((tm, D), row),
                  pl.BlockSpec((1, 1, D), bidx),
                  pl.BlockSpec((1, 1, D), bidx),
                  pl.BlockSpec((1, 1, D), bidx),
                  pl.BlockSpec((1, 1, D), bidx),
                  const((1, D)),
                  const((1, D)),
                  const((D, D)),
                  const((D, D)),
                  const((D, FF)),
                  const((D, FF)),
                  const((FF, D))],
        out_specs=pl.BlockSpec((tm, D), row),
        scratch_shapes=[pltpu.VMEM((tm, FF), BF16)],
        compiler_params=_cparams(("arbitrary",), 58),
        name="tail",
    )(a2, yft, z2, x2, g1, sc2, sh2, g2, n2, nf, wf, wo, wg, wu, wd)


def _rope_tables(S):
    half = RET_DK // 2
    freqs = ROPE_THETA ** (-np.arange(half, dtype=np.float64) / half)
    ang = np.arange(S, dtype=np.float64)[:, None] * freqs[None, :]
    cos, sin = np.cos(ang), np.sin(ang)
    return (jnp.asarray(np.concatenate([cos, cos], axis=1), F32),
            jnp.asarray(np.concatenate([-sin, sin], axis=1), F32))


def _aug_lanes(v8):
    rep = jnp.repeat(v8[..., None], 6, axis=-1)
    pad = jnp.zeros(v8.shape + (AUG_STRIDE - 6,), v8.dtype)
    return jnp.concatenate([rep, pad], axis=-1).reshape(v8.shape[:-1] + (LANES,))


def _layer(x, mod, norm1_w, w_in, b_f, ret_proj, fox_proj, w_out, norm2_w,
           w_gate, w_up, w_down, norm_out_w, tables):
    B, S, D = x.shape
    T = B * S
    sh1, sc1, g1, sh2, sc2, g2 = [m.reshape(B, 1, D) for m in jnp.split(mod, 6, axis=-1)]
    offs = np.cumsum([0, RET_HEADS * RET_DK, RET_HEADS * RET_DK, RET_HEADS * RET_DV,
                      RET_HEADS * RET_DV, FOX_HEADS * FOX_DH, FOX_HEADS * FOX_DH,
                      FOX_HEADS * FOX_DH, FOX_HEADS, D, D])
    o_fv, o_ff, o_gr = int(offs[6]), int(offs[7]), int(offs[8])
    w_a = w_in[:, :o_fv].astype(BF16)
    w_b = w_in[:, o_gr:].astype(BF16)
    w_vt = w_in[:, o_fv:o_ff].T.astype(BF16)
    w_ff = _aug_lanes(w_in[:, o_ff:o_gr]).astype(BF16)
    bf_l = _aug_lanes(b_f).reshape(1, LANES)

    x2 = x.reshape(T, D)
    z, vt, qa, ka = _inproj(x2, norm1_w.reshape(1, D), sc1, sh1, tables[0], tables[1],
                            w_a, w_b, w_vt, w_ff, bf_l, B, S)
    z3 = z.reshape(B, S, z.shape[1])
    a = _retention(z3, ret_proj.astype(BF16), D)
    yf = _fox(z3, qa, ka, vt)
    out = _tail(a.reshape(T, D), yf, z, x2, g1, sc2, sh2, g2,
                norm2_w.reshape(1, D), norm_out_w.reshape(1, D),
                fox_proj.astype(BF16), w_out.astype(BF16), w_gate.astype(BF16),
                w_up.astype(BF16), w_down.astype(BF16), S)
    return out.reshape(B, S, D)


def kernel(x, c, ada_w, ada_b, norm1_w, w_in, b_f, ret_proj, fox_proj, w_out,
           norm2_w, w_gate, w_up, w_down, norm_f_w):
    depth = ada_w.shape[0]
    assert depth == 1, "the final RMSNorm is fused into the last layer's channel mixer"
    tables = _rope_tables(x.shape[1])
    l = 0
    mod = _ada(c, ada_w[l], ada_b[l])
    return _layer(x, mod, norm1_w[l], w_in[l], b_f[l], ret_proj[l], fox_proj[l], w_out[l],
                  norm2_w[l], w_gate[l], w_up[l], w_down[l], norm_f_w, tables)
```

```python
import functools
import math

import numpy as np
import jax
import jax.numpy as jnp
from jax import lax
from jax.experimental import pallas as pl
from jax.experimental.pallas import tpu as pltpu

F32 = jnp.float32
BF16 = jnp.bfloat16

EPS = 1e-6
ROPE_THETA = 10000.0
RET_HEADS = 4
RET_DK = 128
RET_DV = 256
FOX_HEADS = 8
FOX_DH = 128
RET_CHUNK = 64
LANES = 128
AUG_STRIDE = 16

NEG_BIG = -1e30
LOG2E = math.log2(math.e)

_NT = (((1,), (1,)), ((), ()))
_TN = (((0,), (0,)), ((), ()))


def _cparams(sem, vmem_mb, flags=None):
    return pltpu.CompilerParams(dimension_semantics=sem,
                                vmem_limit_bytes=vmem_mb * 1024 * 1024, flags=flags)


def _ada_kernel(ct_ref, w_ref, b_ref, o_ref):
    ct = ct_ref[...]
    act = ct * jax.nn.sigmoid(ct)
    w = w_ref[...]
    for b in range(o_ref.shape[0]):
        o_ref[b:b + 1, :] = (jnp.sum(act[:, b:b + 1] * w, axis=0, keepdims=True)
                             + b_ref[...])


def _ada(c, w, b):
    B, D = c.shape
    N = w.shape[1]
    tn = 1536
    return pl.pallas_call(
        _ada_kernel,
        out_shape=jax.ShapeDtypeStruct((B, N), F32),
        grid=(N // tn,),
        in_specs=[pl.BlockSpec((D, B), lambda j: (0, 0)),
                  pl.BlockSpec((D, tn), lambda j: (0, j)),
                  pl.BlockSpec((1, tn), lambda j: (0, j))],
        out_specs=pl.BlockSpec((B, tn), lambda j: (0, j)),
        compiler_params=_cparams(("arbitrary",), 40),
        name="ada",
    )(c.T, w, b.reshape(1, N))


INPROJ_CHUNK = 512
Z_COLS = {"rq": 0, "rk": 512, "rv": 1024, "rg": 2048, "fk": 3072, "gr": 4096, "gf": 5120,
          "end": 6144}
WA_COLS = {"rq": 0, "rk": 512, "rv": 1024, "rg": 2048, "fq": 3072, "fk": 4096,
           "fv": 5120}


def _split3(v):
    p0 = v.astype(BF16)
    r1 = v - p0.astype(F32)
    p1 = r1.astype(BF16)
    r2 = r1 - p1.astype(F32)
    p2 = r2.astype(BF16)
    return p0, p1, p2


def _inproj_kernel(x0_ref, xn_ref, nw_ref, sc0_ref, sh0_ref, scn_ref, shn_ref, cos_ref, sin_ref,
                   wa_ref, wb_ref, wff_ref, bf_ref, tri_ref,
                   z_ref, qt_ref, vt_ref, ft_ref, ka_ref, h_scr, carry,
                   *, fox_scale, rk_scale, spb):
    i = pl.program_id(0)

    def normed(x_ref, sc_ref, sh_ref):
        x = x_ref[...]
        ms = jnp.mean(x * x, axis=-1, keepdims=True)
        y = x * lax.rsqrt(ms + EPS) * nw_ref[...]
        return (y * (1.0 + sc_ref[0]) + sh_ref[0]).astype(BF16)

    @pl.when(i == 0)
    def _():
        h_scr[0] = normed(x0_ref, sc0_ref, sh0_ref)

    h = h_scr[i % 2]
    cos = cos_ref[...]
    sin = sin_ref[...]
    cw = INPROJ_CHUNK

    def chunk(w_ref, wc0):
        return jnp.dot(h, w_ref[:, wc0:wc0 + cw], preferred_element_type=F32)

    def rotary(name, scale):
        acc = chunk(wa_ref, WA_COLS[name])
        c0 = Z_COLS[name]
        for hh in range(cw // RET_DK):
            a = acc[:, hh * RET_DK:(hh + 1) * RET_DK]
            o = a * cos + pltpu.roll(a, RET_DK // 2, axis=1) * sin
            if scale is not None:
                o = o * scale
            z_ref[:, c0 + hh * RET_DK:c0 + (hh + 1) * RET_DK] = o.astype(BF16)

    def group(w_ref, wc0, name, width, fn):
        for d in range(0, width, cw):
            c0 = Z_COLS[name] + d
            z_ref[:, c0:c0 + cw] = fn(chunk(w_ref, wc0 + d)).astype(BF16)

    h_next = normed(xn_ref, scn_ref, shn_ref)

    xv = jnp.dot(h, wff_ref[...], preferred_element_type=F32) + bf_ref[...]
    lf = jnp.minimum(xv, 0.0) - jnp.log(1.0 + jnp.exp(-jnp.abs(xv)))
    p0, p1, p2 = _split3(lf)

    rotary("rq", None)
    rotary("rk", rk_scale)

    tri = tri_ref[...]
    cs = (jnp.dot(tri, p0, preferred_element_type=F32)
          + jnp.dot(tri, p1, preferred_element_type=F32)
          + jnp.dot(tri, p2, preferred_element_type=F32))
    first = (i % spb) == 0
    fc = cs + jnp.where(first, 0.0, carry[...])
    tm = fc.shape[0]
    carry[...] = fc[tm - 1:tm, :]

    group(wa_ref, WA_COLS["rv"], "rv", RET_HEADS * RET_DV, lambda a: a)

    f2l = fc * LOG2E
    ft_ref[0] = f2l.T
    f0, f1, f2 = [p.astype(F32) for p in _split3(f2l)]
    c = lax.broadcasted_iota(jnp.int32, fc.shape, 1) % AUG_STRIDE
    ka_ref[0] = jnp.where(c < 3, 1.0, jnp.where(c == 3, -f0, jnp.where(
        c == 4, -f1, jnp.where(c == 5, -f2, 0.0)))).astype(BF16)

    group(wa_ref, WA_COLS["fk"], "fk", FOX_HEADS * FOX_DH, lambda a: a)
    group(wa_ref, WA_COLS["rg"], "rg", RET_HEADS * RET_DV, lambda a: a * jax.nn.sigmoid(a))
    for d in range(0, FOX_HEADS * FOX_DH, cw):
        vt_ref[0, d:d + cw, :] = chunk(wa_ref, WA_COLS["fv"] + d).T.astype(BF16)
    group(wb_ref, 0, "gr", wb_ref.shape[1], jax.nn.sigmoid)
    for d in range(0, FOX_HEADS * FOX_DH, cw):
        qt_ref[0, d:d + cw, :] = (chunk(wa_ref, WA_COLS["fq"] + d) * fox_scale).T.astype(BF16)
    h_scr[(i + 1) % 2] = h_next


def _inproj(x2, nw, sc, sh, cos_t, sin_t, wa, wb, wff, bf_l, B, S):
    T, D = x2.shape
    NA, NB = wa.shape[1], wb.shape[1]
    N = Z_COLS["end"]
    NQ = NV = FOX_HEADS * FOX_DH
    assert NA + NB == N + NQ + NV and NA % INPROJ_CHUNK == 0
    tm = 512
    spb = S // tm
    tri = jnp.asarray(np.tril(np.ones((tm, tm), np.float32)), BF16)
    kern = functools.partial(_inproj_kernel, fox_scale=LOG2E / math.sqrt(FOX_DH),
                             rk_scale=RET_DK ** -0.5, spb=spb)
    const = lambda shape: pl.BlockSpec(shape, lambda i: (0,) * len(shape),
                                       pipeline_mode=pl.Buffered(1))
    nxt = lambda i: jnp.minimum(i + 1, T // tm - 1)
    return pl.pallas_call(
        kern,
        out_shape=(jax.ShapeDtypeStruct((T, N), BF16),
                   jax.ShapeDtypeStruct((B, NQ, S), BF16),
                   jax.ShapeDtypeStruct((B, NV, S), BF16),
                   jax.ShapeDtypeStruct((B, LANES, S), F32),
                   jax.ShapeDtypeStruct((B, S, LANES), BF16)),
        grid=(T // tm,),
        in_specs=[const((tm, D)),
                  pl.BlockSpec((tm, D), lambda i: (nxt(i), 0)),
                  const((1, D)),
                  const((1, 1, D)),
                  const((1, 1, D)),
                  pl.BlockSpec((1, 1, D), lambda i: (nxt(i) // spb, 0, 0)),
                  pl.BlockSpec((1, 1, D), lambda i: (nxt(i) // spb, 0, 0)),
                  pl.BlockSpec((tm, LANES), lambda i: (i % spb, 0)),
                  pl.BlockSpec((tm, LANES), lambda i: (i % spb, 0)),
                  const((D, NA)),
                  const((D, NB)),
                  const((D, LANES)),
                  const((1, LANES)),
                  const((tm, tm))],
        out_specs=(pl.BlockSpec((tm, N), lambda i: (i, 0)),
                   pl.BlockSpec((1, NQ, tm), lambda i: (i // spb, 0, i % spb)),
                   pl.BlockSpec((1, NV, tm), lambda i: (i // spb, 0, i % spb)),
                   pl.BlockSpec((1, LANES, tm), lambda i: (i // spb, 0, i % spb)),
                   pl.BlockSpec((1, tm, LANES), lambda i: (i // spb, i % spb, 0))),
        scratch_shapes=[pltpu.VMEM((2, tm, D), BF16), pltpu.VMEM((1, LANES), F32)],
        compiler_params=_cparams(("arbitrary",), 56),
        name="inproj",
    )(x2, x2, nw, sc, sh, sc, sh, cos_t, sin_t, wa, wb, wff, bf_l, tri)


def _ret_consts(L):
    hs = np.arange(RET_HEADS, dtype=np.float64)
    log_gamma = np.log(1.0 - np.exp2(-5.0 - hs))
    idx = np.arange(L, dtype=np.float64)
    dist = np.abs(idx[:, None] - idx[None, :])
    chunk_ok = (idx[None, :] // RET_CHUNK) <= (idx[:, None] // RET_CHUNK)
    dmask = np.exp(log_gamma[:, None, None] * dist[None]) * chunk_ok[None]
    qdec = np.exp(log_gamma[:, None] * idx[None, :])[..., None]
    kdec = np.exp(log_gamma[:, None] * (L - idx)[None, :])[..., None]
    bdec = [float(v) for v in np.exp(log_gamma * L)]
    return (jnp.asarray(dmask, F32), jnp.asarray(qdec, F32), jnp.asarray(kdec, F32), bdec)


def _ret_kernel(rq_ref, rk_ref, rv_ref, rg_ref, gr_ref, dm_ref, qd_ref, kd_ref, wr_ref,
                o_ref, st_scr, y_scr, *, bdec):
    @pl.when(pl.program_id(1) == 0)
    def _():
        st_scr[...] = jnp.zeros_like(st_scr)

    for h in range(RET_HEADS):
        q = rq_ref[0, :, h * RET_DK:(h + 1) * RET_DK]
        k = rk_ref[0, :, h * RET_DK:(h + 1) * RET_DK]
        v = rv_ref[0, :, h * RET_DV:(h + 1) * RET_DV]
        s = lax.dot_general(q, k, _NT, preferred_element_type=F32) * dm_ref[h]
        intra = jnp.dot(s.astype(BF16), v, preferred_element_type=F32)
        st = st_scr[h]
        inter = jnp.dot(q, st.astype(BF16), preferred_element_type=F32) * qd_ref[h]
        ks = (k.astype(F32) * kd_ref[h]).astype(BF16)
        st_scr[h] = bdec[h] * st + lax.dot_general(ks, v, _TN, preferred_element_type=F32)
        ro = intra + inter
        ron = ro * lax.rsqrt(jnp.mean(ro * ro, axis=-1, keepdims=True) + EPS)
        g = rg_ref[0, :, h * RET_DV:(h + 1) * RET_DV].astype(F32)
        y_scr[:, h * RET_DV:(h + 1) * RET_DV] = (g * ron).astype(BF16)

    proj = jnp.dot(y_scr[...], wr_ref[...], preferred_element_type=F32)
    o_ref[0] = (gr_ref[0].astype(F32) * proj).astype(BF16)


def _retention(z3, wr, D):
    B, S, _ = z3.shape
    L = 512
    dmask, qdec, kdec, bdec = _ret_consts(L)
    QK = RET_HEADS * RET_DK
    V = RET_HEADS * RET_DV
    kern = functools.partial(_ret_kernel, bdec=bdec)
    full3 = lambda b, i: (0, 0, 0)
    return pl.pallas_call(
        kern,
        out_shape=jax.ShapeDtypeStruct((B, S, D), BF16),
        grid=(B, S // L),
        in_specs=[pl.BlockSpec((1, L, QK), lambda b, i: (b, i, 0)),
                  pl.BlockSpec((1, L, QK), lambda b, i: (b, i, 1)),
                  pl.BlockSpec((1, L, V), lambda b, i: (b, i, 1)),
                  pl.BlockSpec((1, L, V), lambda b, i: (b, i, 2)),
                  pl.BlockSpec((1, L, D), lambda b, i: (b, i, Z_COLS["gr"] // D)),
                  pl.BlockSpec((RET_HEADS, L, L), full3),
                  pl.BlockSpec((RET_HEADS, L, 1), full3),
                  pl.BlockSpec((RET_HEADS, L, 1), full3),
                  pl.BlockSpec((V, D), lambda b, i: (0, 0))],
        out_specs=pl.BlockSpec((1, L, D), lambda b, i: (b, i, 0)),
        scratch_shapes=[pltpu.VMEM((RET_HEADS, RET_DK, RET_DV), F32),
                        pltpu.VMEM((L, V), BF16)],
        compiler_params=_cparams(("arbitrary", "arbitrary"), 40),
        name="ret",
    )(z3, z3, z3, z3, z3, dmask, qdec, kdec, wr)


FOX_CHUNK = 256
FOX_MAX_EXCESS = 64.0


def _fox_kernel(qt_ref, ft_ref, k_ref, ka_ref, vt_ref, o_ref,
                kx_scr, vt_scr, qxt_scr, sa_scr, sb_scr, mba_scr, mbb_scr, pa_scr, pb_scr,
                m_scr, al_scr, acc_scr, ex_scr, *, tq, tk):
    h = pl.program_id(1)
    i = pl.program_id(2)
    nkb = vt_scr.shape[0]

    @pl.when(i == 0)
    def _():
        kx_scr[:, :FOX_DH] = k_ref[0]
        kx_scr[:, FOX_DH:] = ka_ref[0]
        qxt_scr[FOX_DH:, :] = jnp.zeros((qxt_scr.shape[0] - FOX_DH, tq), BF16)
        for n in range(nkb):
            vt_scr[n, :FOX_DH, :] = vt_ref[0, :, n * tk:(n + 1) * tk]
            vt_scr[n, FOX_DH:, :] = jnp.ones((vt_scr.shape[1] - FOX_DH, tk), BF16)

    qxt_scr[:FOX_DH, :] = qt_ref[0]
    f_row = ft_ref[0, pl.ds(pl.multiple_of(AUG_STRIDE * h, AUG_STRIDE), 8), :][0:1, :]
    f0, f1, f2 = [p.astype(F32) for p in _split3(f_row)]
    sub = lax.broadcasted_iota(jnp.int32, (AUG_STRIDE, tq), 0)
    aug = jnp.where(sub == 0, f0, jnp.where(sub == 1, f1, jnp.where(
        sub == 2, f2, jnp.where(sub < 6, 1.0, 0.0))))
    qxt_scr[pl.ds(pl.multiple_of(FOX_DH + AUG_STRIDE * h, AUG_STRIDE), AUG_STRIDE), :] = (
        aug.astype(BF16))

    chunks = [slice(c0, c0 + FOX_CHUNK) for c0 in range(0, tq, FOX_CHUNK)]

    def qk(blk, cs):
        off = pl.multiple_of(blk * tk, tk)
        return jnp.dot(kx_scr[pl.ds(off, tk), :], qxt_scr[:, cs],
                       preferred_element_type=F32)

    def causal(s, cs, key0):
        key = lax.broadcasted_iota(jnp.int32, s.shape, 0) + key0
        qry = lax.broadcasted_iota(jnp.int32, s.shape, 1) + cs.start
        return jnp.where(key <= qry, s, NEG_BIG)

    def emit():
        acc = acc_scr[...]
        o_ref[0] = (acc[:FOX_DH] / acc[FOX_DH:FOX_DH + 1]).astype(BF16)

    def pv_acc(blk_prev, p_prv, cs):
        acc_scr[:, cs] = al_scr[:, cs] * (acc_scr[:, cs] + jnp.dot(
            vt_scr[blk_prev], p_prv[:, cs], preferred_element_type=F32))

    def lazy_chunk(blk, cs, p_cur, key0):
        s = qk(blk, cs)
        if key0 is not None:
            s = causal(s, cs, key0)
        mb = jnp.max(s, axis=0, keepdims=True)
        r_old = m_scr[:, cs]
        p_cur[:, cs] = jnp.exp2((s - r_old).astype(BF16))
        r_new = jnp.maximum(r_old, mb)
        m_scr[:, cs] = r_new
        ex_scr[:, cs] = jnp.maximum(ex_scr[:, cs], mb - r_old)
        return jnp.exp2(r_old - r_new)

    def lazy_step(blk, blk_prev, p_cur, p_prv):
        for cs in chunks:
            alpha = lazy_chunk(blk, cs, p_cur, None)
            pv_acc(blk_prev, p_prv, cs)
            al_scr[:, cs] = alpha

    def own_block(cs):
        return (2 * i, 0) if cs.start < tk else (2 * i + 1, tk)

    for cs in chunks:
        blk, key0 = own_block(cs)
        s = causal(qk(blk, cs), cs, key0)
        mb = jnp.max(s, axis=0, keepdims=True)
        m_scr[:, cs] = mb
        pa_scr[:, cs] = jnp.exp2((s - mb).astype(BF16))
        al_scr[:, cs] = jnp.ones((1, FOX_CHUNK), F32)
        acc_scr[:, cs] = jnp.zeros((acc_scr.shape[0], FOX_CHUNK), F32)
        ex_scr[:, cs] = jnp.zeros((1, FOX_CHUNK), F32)
    for cs in chunks:
        blk, _ = own_block(cs)
        if cs.start < tk:
            pb_scr[:, cs] = jnp.zeros((tk, FOX_CHUNK), BF16)
            pv_acc(blk, pa_scr, cs)
        else:
            alpha = lazy_chunk(2 * i, cs, pb_scr, None)
            pv_acc(blk, pa_scr, cs)
            al_scr[:, cs] = alpha

    def lazy_body(t, carry):
        jj = i - 1 - t
        lazy_step(2 * jj + 1, 2 * jj + 2, pa_scr, pb_scr)
        lazy_step(2 * jj, 2 * jj + 1, pb_scr, pa_scr)
        return carry

    lax.fori_loop(0, i, lazy_body, 0)
    for cs in chunks:
        pv_acc(0, pb_scr, cs)
    emit()

    def scores(blk, s_out, mb_out, cs):
        s = qk(blk, cs)
        s_out[:, cs] = s
        if mb_out is not None:
            mb_out[:, cs] = jnp.max(s, axis=0, keepdims=True)

    def softmax(s_in, mb_in, p_out, key0, cs):
        s = s_in[:, cs]
        if key0 is None:
            mb = mb_in[:, cs]
        else:
            s = causal(s, cs, key0)
            mb = jnp.max(s, axis=0, keepdims=True)
        m_prev = m_scr[:, cs]
        m_new = jnp.maximum(m_prev, mb)
        p_out[:, cs] = jnp.exp2((s - m_new).astype(BF16))
        m_scr[:, cs] = m_new
        return jnp.exp2(m_prev - m_new)

    def pv(blk_prev, p_in, cs):
        acc_scr[:, cs] = al_scr[:, cs] * acc_scr[:, cs] + jnp.dot(
            vt_scr[blk_prev], p_in[:, cs], preferred_element_type=F32)

    def step(n, cur, nxt, p_prv, key0):
        s_cur, mb_cur, p_cur = cur
        for cs in chunks:
            if nxt is not None:
                scores(n + 1, nxt[0], nxt[1], cs)
            alpha = softmax(s_cur, mb_cur, p_cur, key0, cs)
            pv(jnp.maximum(n - 1, 0), p_prv, cs)
            al_scr[:, cs] = alpha

    buf_a = (sa_scr, mba_scr, pa_scr)
    buf_b = (sb_scr, mbb_scr, pb_scr)

    def body(jj, carry):
        step(2 * jj, buf_a, buf_b, pb_scr, None)
        step(2 * jj + 1, buf_b, buf_a, pa_scr, None)
        return carry

    @pl.when(jnp.max(ex_scr[...]) > FOX_MAX_EXCESS)
    def _():
        m_scr[...] = jnp.full(m_scr.shape, NEG_BIG, F32)
        al_scr[...] = jnp.ones_like(al_scr)
        acc_scr[...] = jnp.zeros_like(acc_scr)
        pb_scr[...] = jnp.zeros_like(pb_scr)
        for cs in chunks:
            scores(0, sa_scr, mba_scr, cs)
        lax.fori_loop(0, i, body, 0)
        step(2 * i, buf_a, (sb_scr, None), pb_scr, 0)
        step(2 * i + 1, buf_b, None, pa_scr, tk)
        for cs in chunks:
            pv(2 * i + 1, pb_scr, cs)
        emit()


def _fox(z3, qt, ft, ka, vt):
    B, S, _ = z3.shape
    tq, tk = 1024, 512
    assert tq == 2 * tk
    ones_rows = 16
    W = FOX_HEADS * FOX_DH
    kb = Z_COLS["fk"] // FOX_DH
    kern = functools.partial(_fox_kernel, tq=tq, tk=tk)
    return pl.pallas_call(
        kern,
        out_shape=jax.ShapeDtypeStruct((B, W, S), BF16),
        grid=(B, FOX_HEADS, S // tq),
        in_specs=[pl.BlockSpec((1, FOX_DH, tq), lambda b, h, i: (b, h, i)),
                  pl.BlockSpec((1, LANES, tq), lambda b, h, i: (b, 0, i)),
                  pl.BlockSpec((1, S, FOX_DH), lambda b, h, i: (b, 0, kb + h)),
                  pl.BlockSpec((1, S, LANES), lambda b, h, i: (b, 0, 0)),
                  pl.BlockSpec((1, FOX_DH, S), lambda b, h, i: (b, h, 0))],
        out_specs=pl.BlockSpec((1, FOX_DH, tq), lambda b, h, i: (b, h, i)),
        scratch_shapes=[pltpu.VMEM((S, 2 * FOX_DH), BF16),
                        pltpu.VMEM((S // tk, FOX_DH + ones_rows, tk), BF16),
                        pltpu.VMEM((2 * FOX_DH, tq), BF16),
                        pltpu.VMEM((tk, tq), F32),
                        pltpu.VMEM((tk, tq), F32),
                        pltpu.VMEM((1, tq), F32),
                        pltpu.VMEM((1, tq), F32),
                        pltpu.VMEM((tk, tq), BF16),
                        pltpu.VMEM((tk, tq), BF16),
                        pltpu.VMEM((1, tq), F32),
                        pltpu.VMEM((1, tq), F32),
                        pltpu.VMEM((FOX_DH + ones_rows, tq), F32),
                        pltpu.VMEM((1, tq), F32)],
        compiler_params=_cparams(("arbitrary", "arbitrary", "arbitrary"), 48),
        name="fox",
    )(qt, ft, z3, ka, vt)


FFN_CHUNK = 256


def _tail_kernel(a_ref, yft_ref, gf_ref, x_ref, g1_ref, sc_ref, sh_ref, g2_ref, n2_ref, nf_ref,
                 wf_ref, wo_ref, wg_ref, wu_ref, wd_ref, o_ref, act_scr):
    tm = x_ref.shape[0]
    halves = [slice(r0, r0 + tm // 2) for r0 in range(0, tm, tm // 2)]
    x1s, hs = [], []
    for rs in halves:
        fo = lax.dot_general(yft_ref[0, :, rs], wf_ref[...], _TN, preferred_element_type=F32)
        merged = a_ref[rs, :].astype(F32) + gf_ref[rs, :].astype(F32) * fo
        out = jnp.dot(merged.astype(BF16), wo_ref[...], preferred_element_type=F32)
        x1s.append(x_ref[rs, :] + g1_ref[0] * out)
    for x1 in x1s:
        ms = jnp.mean(x1 * x1, axis=-1, keepdims=True)
        y = x1 * lax.rsqrt(ms + EPS) * n2_ref[...]
        hs.append((y * (1.0 + sc_ref[0]) + sh_ref[0]).astype(BF16))
    for c0 in range(0, wg_ref.shape[1], FFN_CHUNK):
        for rs, h in zip(halves, hs):
            g = jnp.dot(h, wg_ref[:, c0:c0 + FFN_CHUNK], preferred_element_type=F32)
            u = jnp.dot(h, wu_ref[:, c0:c0 + FFN_CHUNK], preferred_element_type=F32)
            act_scr[rs, c0:c0 + FFN_CHUNK] = (g * jax.nn.sigmoid(g) * u).astype(BF16)
    for rs, x1 in zip(halves, x1s):
        ffn = jnp.dot(act_scr[rs, :], wd_ref[...], preferred_element_type=F32)
        x2 = x1 + g2_ref[0] * ffn
        ms2 = jnp.mean(x2 * x2, axis=-1, keepdims=True)
        o_ref[rs, :] = x2 * lax.rsqrt(ms2 + EPS) * nf_ref[...]


def _tail(a2, yft, z2, x2, g1, sc2, sh2, g2, n2, nf, wf, wo, wg, wu, wd, S):
    T, D = x2.shape
    FF = wg.shape[1]
    assert FF % FFN_CHUNK == 0
    tm = 512
    spb = S // tm
    row = lambda i: (i, 0)
    bidx = lambda i: (i // spb, 0, 0)
    const = lambda shape: pl.BlockSpec(shape, lambda i: (0,) * len(shape),
                                       pipeline_mode=pl.Buffered(1))
    return pl.pallas_call(
        _tail_kernel,
        out_shape=jax.ShapeDtypeStruct((T, D), F32),
        grid=(T // tm,),
        in_specs=[pl.BlockSpec((tm, D), row),
                  pl.BlockSpec((1, D, tm), lambda i: (i // spb, 0, i % spb)),
                  pl.BlockSpec((tm, D), lambda i: (i, Z_COLS["gf"] // D)),
                  pl.BlockSpec((tm, D), row),
                  pl.BlockSpec((1, 1, D), bidx),
                  pl.BlockSpec((1, 1, D), bidx),
                  pl.BlockSpec((1, 1, D), bidx),
                  pl.BlockSpec((1, 1, D), bidx),
                  const((1, D)),
                  const((1, D)),
                  const((D, D)),
                  const((D, D)),
                  const((D, FF)),
                  const((D, FF)),
                  const((FF, D))],
        out_specs=pl.BlockSpec((tm, D), row),
        scratch_shapes=[pltpu.VMEM((tm, FF), BF16)],
        compiler_params=_cparams(("arbitrary",), 58),
        name="tail",
    )(a2, yft, z2, x2, g1, sc2, sh2, g2, n2, nf, wf, wo, wg, wu, wd)


def _rope_tables(S):
    half = RET_DK // 2
    freqs = ROPE_THETA ** (-np.arange(half, dtype=np.float64) / half)
    ang = np.arange(S, dtype=np.float64)[:, None] * freqs[None, :]
    cos, sin = np.cos(ang), np.sin(ang)
    return (jnp.asarray(np.concatenate([cos, cos], axis=1), F32),
            jnp.asarray(np.concatenate([-sin, sin], axis=1), F32))


def _aug_lanes(v8):
    rep = jnp.repeat(v8[..., None], 6, axis=-1)
    pad = jnp.zeros(v8.shape + (AUG_STRIDE - 6,), v8.dtype)
    return jnp.concatenate([rep, pad], axis=-1).reshape(v8.shape[:-1] + (LANES,))


def _layer(x, mod, norm1_w, w_in, b_f, ret_proj, fox_proj, w_out, norm2_w,
           w_gate, w_up, w_down, norm_out_w, tables):
    B, S, D = x.shape
    T = B * S
    sh1, sc1, g1, sh2, sc2, g2 = [m.reshape(B, 1, D) for m in jnp.split(mod, 6, axis=-1)]
    offs = np.cumsum([0, RET_HEADS * RET_DK, RET_HEADS * RET_DK, RET_HEADS * RET_DV,
                      RET_HEADS * RET_DV, FOX_HEADS * FOX_DH, FOX_HEADS * FOX_DH,
                      FOX_HEADS * FOX_DH, FOX_HEADS, D, D])
    o_ff, o_gr = int(offs[7]), int(offs[8])
    w_a = w_in[:, :o_ff].astype(BF16)
    w_b = w_in[:, o_gr:].astype(BF16)
    w_ff = _aug_lanes(w_in[:, o_ff:o_gr]).astype(BF16)
    bf_l = _aug_lanes(b_f).reshape(1, LANES)

    x2 = x.reshape(T, D)
    z, qt, vt, ft, ka = _inproj(x2, norm1_w.reshape(1, D), sc1, sh1, tables[0], tables[1],
                            w_a, w_b, w_ff, bf_l, B, S)
    z3 = z.reshape(B, S, z.shape[1])
    a = _retention(z3, ret_proj.astype(BF16), D)
    yf = _fox(z3, qt, ft, ka, vt)
    out = _tail(a.reshape(T, D), yf, z, x2, g1, sc2, sh2, g2,
                norm2_w.reshape(1, D), norm_out_w.reshape(1, D),
                fox_proj.astype(BF16), w_out.astype(BF16), w_gate.astype(BF16),
                w_up.astype(BF16), w_down.astype(BF16), S)
    return out.reshape(B, S, D)


def kernel(x, c, ada_w, ada_b, norm1_w, w_in, b_f, ret_proj, fox_proj, w_out,
           norm2_w, w_gate, w_up, w_down, norm_f_w):
    depth = ada_w.shape[0]
    assert depth == 1, "the final RMSNorm is fused into the last layer's channel mixer"
    tables = _rope_tables(x.shape[1])
    l = 0
    mod = _ada(c, ada_w[l], ada_b[l])
    return _layer(x, mod, norm1_w[l], w_in[l], b_f[l], ret_proj[l], fox_proj[l], w_out[l],
                  norm2_w[l], w_gate[l], w_up[l], w_down[l], norm_f_w, tables)
```

```python
import functools
import math

import numpy as np
import jax
import jax.numpy as jnp
from jax import lax
from jax.experimental import pallas as pl
from jax.experimental.pallas import tpu as pltpu

F32 = jnp.float32
BF16 = jnp.bfloat16

EPS = 1e-6
ROPE_THETA = 10000.0
RET_HEADS = 4
RET_DK = 128
RET_DV = 256
FOX_HEADS = 8
FOX_DH = 128
RET_CHUNK = 64
LANES = 128
AUG_STRIDE = 16

NEG_BIG = -1e30
LOG2E = math.log2(math.e)

_NT = (((1,), (1,)), ((), ()))
_TN = (((0,), (0,)), ((), ()))


def _cparams(sem, vmem_mb, flags=None):
    return pltpu.CompilerParams(dimension_semantics=sem,
                                vmem_limit_bytes=vmem_mb * 1024 * 1024, flags=flags)


def _ada_kernel(ct_ref, w_ref, b_ref, o_ref):
    ct = ct_ref[...]
    act = ct * jax.nn.sigmoid(ct)
    w = w_ref[...]
    for b in range(o_ref.shape[0]):
        o_ref[b:b + 1, :] = (jnp.sum(act[:, b:b + 1] * w, axis=0, keepdims=True)
                             + b_ref[...])


def _ada(c, w, b):
    B, D = c.shape
    N = w.shape[1]
    tn = 1536
    return pl.pallas_call(
        _ada_kernel,
        out_shape=jax.ShapeDtypeStruct((B, N), F32),
        grid=(N // tn,),
        in_specs=[pl.BlockSpec((D, B), lambda j: (0, 0)),
                  pl.BlockSpec((D, tn), lambda j: (0, j)),
                  pl.BlockSpec((1, tn), lambda j: (0, j))],
        out_specs=pl.BlockSpec((B, tn), lambda j: (0, j)),
        compiler_params=_cparams(("arbitrary",), 40),
        name="ada",
    )(c.T, w, b.reshape(1, N))


INPROJ_CHUNK = 512
Z_COLS = {"rq": 0, "rk": 512, "rv": 1024, "rg": 2048, "fk": 3072, "gr": 4096, "gf": 5120,
          "end": 6144}
WA_COLS = {"rq": 0, "rk": 512, "rv": 1024, "rg": 2048, "fq": 3072, "fk": 4096,
           "fv": 5120}


def _split3(v):
    p0 = v.astype(BF16)
    r1 = v - p0.astype(F32)
    p1 = r1.astype(BF16)
    r2 = r1 - p1.astype(F32)
    p2 = r2.astype(BF16)
    return p0, p1, p2


def _inproj_kernel(x0_ref, xn_ref, nw_ref, sc0_ref, sh0_ref, scn_ref, shn_ref, cos_ref, sin_ref,
                   wa_ref, wb_ref, wff_ref, bf_ref, tri_ref,
                   z_ref, qt_ref, vt_ref, ft_ref, ka_ref, h_scr, carry,
                   *, fox_scale, rk_scale, spb):
    i = pl.program_id(0)

    def normed(x_ref, sc_ref, sh_ref):
        x = x_ref[...]
        ms = jnp.mean(x * x, axis=-1, keepdims=True)
        y = x * lax.rsqrt(ms + EPS) * nw_ref[...]
        return (y * (1.0 + sc_ref[0]) + sh_ref[0]).astype(BF16)

    @pl.when(i == 0)
    def _():
        h_scr[0] = normed(x0_ref, sc0_ref, sh0_ref)

    h = h_scr[i % 2]
    cos = cos_ref[...]
    sin = sin_ref[...]
    cw = INPROJ_CHUNK

    def chunk(w_ref, wc0):
        return jnp.dot(h, w_ref[:, wc0:wc0 + cw], preferred_element_type=F32)

    def rotary(name, scale):
        acc = chunk(wa_ref, WA_COLS[name])
        c0 = Z_COLS[name]
        for hh in range(cw // RET_DK):
            a = acc[:, hh * RET_DK:(hh + 1) * RET_DK]
            o = a * cos + pltpu.roll(a, RET_DK // 2, axis=1) * sin
            if scale is not None:
                o = o * scale
            z_ref[:, c0 + hh * RET_DK:c0 + (hh + 1) * RET_DK] = o.astype(BF16)

    def group(w_ref, wc0, name, width, fn):
        for d in range(0, width, cw):
            c0 = Z_COLS[name] + d
            z_ref[:, c0:c0 + cw] = fn(chunk(w_ref, wc0 + d)).astype(BF16)

    h_next = normed(xn_ref, scn_ref, shn_ref)

    xv = jnp.dot(h, wff_ref[...], preferred_element_type=F32) + bf_ref[...]
    lf = jnp.minimum(xv, 0.0) - jnp.log(1.0 + jnp.exp(-jnp.abs(xv)))
    p0, p1, p2 = _split3(lf)

    rotary("rq", None)
    rotary("rk", rk_scale)

    tri = tri_ref[...]
    cs = (jnp.dot(tri, p0, preferred_element_type=F32)
          + jnp.dot(tri, p1, preferred_element_type=F32)
          + jnp.dot(tri, p2, preferred_element_type=F32))
    first = (i % spb) == 0
    fc = cs + jnp.where(first, 0.0, carry[...])
    tm = fc.shape[0]
    carry[...] = fc[tm - 1:tm, :]

    group(wa_ref, WA_COLS["rv"], "rv", RET_HEADS * RET_DV, lambda a: a)

    f2l = fc * LOG2E
    ft_ref[0] = f2l.T
    f0, f1, f2 = [p.astype(F32) for p in _split3(f2l)]
    c = lax.broadcasted_iota(jnp.int32, fc.shape, 1) % AUG_STRIDE
    ka_ref[0] = jnp.where(c < 3, 1.0, jnp.where(c == 3, -f0, jnp.where(
        c == 4, -f1, jnp.where(c == 5, -f2, 0.0)))).astype(BF16)

    group(wa_ref, WA_COLS["fk"], "fk", FOX_HEADS * FOX_DH, lambda a: a)
    group(wa_ref, WA_COLS["rg"], "rg", RET_HEADS * RET_DV, lambda a: a * jax.nn.sigmoid(a))
    for d in range(0, FOX_HEADS * FOX_DH, cw):
        vt_ref[0, d:d + cw, :] = chunk(wa_ref, WA_COLS["fv"] + d).T.astype(BF16)
    group(wb_ref, 0, "gr", wb_ref.shape[1], jax.nn.sigmoid)
    for d in range(0, FOX_HEADS * FOX_DH, cw):
        qt_ref[0, d:d + cw, :] = (chunk(wa_ref, WA_COLS["fq"] + d) * fox_scale).T.astype(BF16)
    h_scr[(i + 1) % 2] = h_next


def _inproj(x2, nw, sc, sh, cos_t, sin_t, wa, wb, wff, bf_l, B, S):
    T, D = x2.shape
    NA, NB = wa.shape[1], wb.shape[1]
    N = Z_COLS["end"]
    NQ = NV = FOX_HEADS * FOX_DH
    assert NA + NB == N + NQ + NV and NA % INPROJ_CHUNK == 0
    tm = 512
    spb = S // tm
    tri = jnp.asarray(np.tril(np.ones((tm, tm), np.float32)), BF16)
    kern = functools.partial(_inproj_kernel, fox_scale=LOG2E / math.sqrt(FOX_DH),
                             rk_scale=RET_DK ** -0.5, spb=spb)
    const = lambda shape: pl.BlockSpec(shape, lambda i: (0,) * len(shape),
                                       pipeline_mode=pl.Buffered(1))
    nxt = lambda i: jnp.minimum(i + 1, T // tm - 1)
    return pl.pallas_call(
        kern,
        out_shape=(jax.ShapeDtypeStruct((T, N), BF16),
                   jax.ShapeDtypeStruct((B, NQ, S), BF16),
                   jax.ShapeDtypeStruct((B, NV, S), BF16),
                   jax.ShapeDtypeStruct((B, LANES, S), F32),
                   jax.ShapeDtypeStruct((B, S, LANES), BF16)),
        grid=(T // tm,),
        in_specs=[const((tm, D)),
                  pl.BlockSpec((tm, D), lambda i: (nxt(i), 0)),
                  const((1, D)),
                  const((1, 1, D)),
                  const((1, 1, D)),
                  pl.BlockSpec((1, 1, D), lambda i: (nxt(i) // spb, 0, 0)),
                  pl.BlockSpec((1, 1, D), lambda i: (nxt(i) // spb, 0, 0)),
                  pl.BlockSpec((tm, LANES), lambda i: (i % spb, 0)),
                  pl.BlockSpec((tm, LANES), lambda i: (i % spb, 0)),
                  const((D, NA)),
                  const((D, NB)),
                  const((D, LANES)),
                  const((1, LANES)),
                  const((tm, tm))],
        out_specs=(pl.BlockSpec((tm, N), lambda i: (i, 0)),
                   pl.BlockSpec((1, NQ, tm), lambda i: (i // spb, 0, i % spb)),
                   pl.BlockSpec((1, NV, tm), lambda i: (i // spb, 0, i % spb)),
                   pl.BlockSpec((1, LANES, tm), lambda i: (i // spb, 0, i % spb)),
                   pl.BlockSpec((1, tm, LANES), lambda i: (i // spb, i % spb, 0))),
        scratch_shapes=[pltpu.VMEM((2, tm, D), BF16), pltpu.VMEM((1, LANES), F32)],
        compiler_params=_cparams(("arbitrary",), 56),
        name="inproj",
    )(x2, x2, nw, sc, sh, sc, sh, cos_t, sin_t, wa, wb, wff, bf_l, tri)


def _ret_consts(L):
    hs = np.arange(RET_HEADS, dtype=np.float64)
    log_gamma = np.log(1.0 - np.exp2(-5.0 - hs))
    idx = np.arange(L, dtype=np.float64)
    dist = np.abs(idx[:, None] - idx[None, :])
    chunk_ok = (idx[None, :] // RET_CHUNK) <= (idx[:, None] // RET_CHUNK)
    dmask = np.exp(log_gamma[:, None, None] * dist[None]) * chunk_ok[None]
    qdec = np.exp(log_gamma[:, None] * idx[None, :])[..., None]
    kdec = np.exp(log_gamma[:, None] * (L - idx)[None, :])[..., None]
    bdec = [float(v) for v in np.exp(log_gamma * L)]
    return (jnp.asarray(dmask, F32), jnp.asarray(qdec, F32), jnp.asarray(kdec, F32), bdec)


def _ret_kernel(rq_ref, rk_ref, rv_ref, rg_ref, gr_ref, dm_ref, qd_ref, kd_ref, wr_ref,
                o_ref, st_scr, y_scr, *, bdec):
    @pl.when(pl.program_id(1) == 0)
    def _():
        st_scr[...] = jnp.zeros_like(st_scr)

    for h in range(RET_HEADS):
        q = rq_ref[0, :, h * RET_DK:(h + 1) * RET_DK]
        k = rk_ref[0, :, h * RET_DK:(h + 1) * RET_DK]
        v = rv_ref[0, :, h * RET_DV:(h + 1) * RET_DV]
        s = lax.dot_general(q, k, _NT, preferred_element_type=F32) * dm_ref[h]
        intra = jnp.dot(s.astype(BF16), v, preferred_element_type=F32)
        st = st_scr[h]
        inter = jnp.dot(q, st.astype(BF16), preferred_element_type=F32) * qd_ref[h]
        ks = (k.astype(F32) * kd_ref[h]).astype(BF16)
        st_scr[h] = bdec[h] * st + lax.dot_general(ks, v, _TN, preferred_element_type=F32)
        ro = intra + inter
        ron = ro * lax.rsqrt(jnp.mean(ro * ro, axis=-1, keepdims=True) + EPS)
        g = rg_ref[0, :, h * RET_DV:(h + 1) * RET_DV].astype(F32)
        y_scr[:, h * RET_DV:(h + 1) * RET_DV] = (g * ron).astype(BF16)

    proj = jnp.dot(y_scr[...], wr_ref[...], preferred_element_type=F32)
    o_ref[0] = (gr_ref[0].astype(F32) * proj).astype(BF16)


def _retention(z3, wr, D):
    B, S, _ = z3.shape
    L = 512
    dmask, qdec, kdec, bdec = _ret_consts(L)
    QK = RET_HEADS * RET_DK
    V = RET_HEADS * RET_DV
    kern = functools.partial(_ret_kernel, bdec=bdec)
    full3 = lambda b, i: (0, 0, 0)
    return pl.pallas_call(
        kern,
        out_shape=jax.ShapeDtypeStruct((B, S, D), BF16),
        grid=(B, S // L),
        in_specs=[pl.BlockSpec((1, L, QK), lambda b, i: (b, i, 0)),
                  pl.BlockSpec((1, L, QK), lambda b, i: (b, i, 1)),
                  pl.BlockSpec((1, L, V), lambda b, i: (b, i, 1)),
                  pl.BlockSpec((1, L, V), lambda b, i: (b, i, 2)),
                  pl.BlockSpec((1, L, D), lambda b, i: (b, i, Z_COLS["gr"] // D)),
                  pl.BlockSpec((RET_HEADS, L, L), full3),
                  pl.BlockSpec((RET_HEADS, L, 1), full3),
                  pl.BlockSpec((RET_HEADS, L, 1), full3),
                  pl.BlockSpec((V, D), lambda b, i: (0, 0))],
        out_specs=pl.BlockSpec((1, L, D), lambda b, i: (b, i, 0)),
        scratch_shapes=[pltpu.VMEM((RET_HEADS, RET_DK, RET_DV), F32),
                        pltpu.VMEM((L, V), BF16)],
        compiler_params=_cparams(("arbitrary", "arbitrary"), 40),
        name="ret",
    )(z3, z3, z3, z3, z3, dmask, qdec, kdec, wr)


FOX_CHUNK = 256
FOX_KSUB = 256
FOX_MAX_EXCESS = 64.0


def _fox_kernel(qt_ref, ft_ref, k_ref, ka_ref, vt_ref, o_ref,
                kx_scr, vt_scr, qxt_scr, sa_scr, sb_scr, mba_scr, mbb_scr, pa_scr, pb_scr,
                m_scr, al_scr, acc_scr, ex_scr, *, tq, tk):
    h = pl.program_id(1)
    i = pl.program_id(2)
    nkb, nks = vt_scr.shape[0], vt_scr.shape[1]

    @pl.when(i == 0)
    def _():
        kx_scr[:, :FOX_DH] = k_ref[0]
        kx_scr[:, FOX_DH:] = ka_ref[0]
        qxt_scr[:, FOX_DH:, :] = jnp.zeros(
            (qxt_scr.shape[0], qxt_scr.shape[1] - FOX_DH, FOX_CHUNK), BF16)
        for n in range(nkb):
            for kk in range(nks):
                k0 = n * tk + kk * FOX_KSUB
                vt_scr[n, kk, :FOX_DH, :] = vt_ref[0, :, k0:k0 + FOX_KSUB]
                vt_scr[n, kk, FOX_DH:, :] = jnp.ones(
                    (vt_scr.shape[2] - FOX_DH, FOX_KSUB), BF16)

    f_row = ft_ref[0, pl.ds(pl.multiple_of(AUG_STRIDE * h, AUG_STRIDE), 8), :][0:1, :]
    f0, f1, f2 = [p.astype(F32) for p in _split3(f_row)]
    sub = lax.broadcasted_iota(jnp.int32, (AUG_STRIDE, tq), 0)
    aug = jnp.where(sub == 0, f0, jnp.where(sub == 1, f1, jnp.where(
        sub == 2, f2, jnp.where(sub < 6, 1.0, 0.0))))
    aug = aug.astype(BF16)
    aug_rows = pl.ds(pl.multiple_of(FOX_DH + AUG_STRIDE * h, AUG_STRIDE), AUG_STRIDE)

    chunks = [slice(c0, c0 + FOX_CHUNK) for c0 in range(0, tq, FOX_CHUNK)]
    ci = lambda cs: cs.start // FOX_CHUNK
    for cs in chunks:
        qxt_scr[ci(cs), :FOX_DH, :] = qt_ref[0, :, cs]
        qxt_scr[ci(cs), aug_rows, :] = aug[:, cs]

    def qk(blk, cs):
        off = pl.multiple_of(blk * tk, tk)
        return jnp.dot(kx_scr[pl.ds(off, tk), :], qxt_scr[ci(cs)],
                       preferred_element_type=F32)

    def causal(s, cs, key0):
        key = lax.broadcasted_iota(jnp.int32, s.shape, 0) + key0
        qry = lax.broadcasted_iota(jnp.int32, s.shape, 1) + cs.start
        return jnp.where(key <= qry, s, NEG_BIG)

    def emit():
        for cs in chunks:
            acc = acc_scr[ci(cs)]
            o_ref[0, :, cs] = (acc[:FOX_DH] / acc[FOX_DH:FOX_DH + 1]).astype(BF16)

    def pv_dot(blk, p, cs):
        return sum(jnp.dot(vt_scr[blk, kk], p[ci(cs), kk * FOX_KSUB:(kk + 1) * FOX_KSUB, :],
                           preferred_element_type=F32) for kk in range(nks))

    def pv_acc(blk_prev, p_prv, cs):
        acc_scr[ci(cs)] = al_scr[ci(cs)] * (acc_scr[ci(cs)] + pv_dot(blk_prev, p_prv, cs))

    def lazy_chunk(blk, cs, p_cur, key0):
        s = qk(blk, cs)
        if key0 is not None:
            s = causal(s, cs, key0)
        mb = jnp.max(s, axis=0, keepdims=True)
        r_old = m_scr[ci(cs)]
        p_cur[ci(cs)] = jnp.exp2((s - r_old).astype(BF16))
        r_new = jnp.maximum(r_old, mb)
        m_scr[ci(cs)] = r_new
        ex_scr[ci(cs)] = jnp.maximum(ex_scr[ci(cs)], mb - r_old)
        return jnp.exp2(r_old - r_new)

    def lazy_step(blk, blk_prev, p_cur, p_prv):
        for cs in chunks:
            alpha = lazy_chunk(blk, cs, p_cur, None)
            pv_acc(blk_prev, p_prv, cs)
            al_scr[ci(cs)] = alpha

    def own_block(cs):
        return (2 * i, 0) if cs.start < tk else (2 * i + 1, tk)

    for cs in chunks:
        blk, key0 = own_block(cs)
        s = causal(qk(blk, cs), cs, key0)
        mb = jnp.max(s, axis=0, keepdims=True)
        m_scr[ci(cs)] = mb
        pa_scr[ci(cs)] = jnp.exp2((s - mb).astype(BF16))
        al_scr[ci(cs)] = jnp.ones((1, FOX_CHUNK), F32)
        acc_scr[ci(cs)] = jnp.zeros((acc_scr.shape[1], FOX_CHUNK), F32)
        ex_scr[ci(cs)] = jnp.zeros((1, FOX_CHUNK), F32)
    for cs in chunks:
        blk, _ = own_block(cs)
        if cs.start < tk:
            pb_scr[ci(cs)] = jnp.zeros((tk, FOX_CHUNK), BF16)
            pv_acc(blk, pa_scr, cs)
        else:
            alpha = lazy_chunk(2 * i, cs, pb_scr, None)
            pv_acc(blk, pa_scr, cs)
            al_scr[ci(cs)] = alpha

    def lazy_body(t, carry):
        jj = i - 1 - t
        lazy_step(2 * jj + 1, 2 * jj + 2, pa_scr, pb_scr)
        lazy_step(2 * jj, 2 * jj + 1, pb_scr, pa_scr)
        return carry

    lax.fori_loop(0, i, lazy_body, 0)
    for cs in chunks:
        pv_acc(0, pb_scr, cs)
    emit()

    def scores(blk, s_out, mb_out, cs):
        s = qk(blk, cs)
        s_out[ci(cs)] = s
        if mb_out is not None:
            mb_out[ci(cs)] = jnp.max(s, axis=0, keepdims=True)

    def softmax(s_in, mb_in, p_out, key0, cs):
        s = s_in[ci(cs)]
        if key0 is None:
            mb = mb_in[ci(cs)]
        else:
            s = causal(s, cs, key0)
            mb = jnp.max(s, axis=0, keepdims=True)
        m_prev = m_scr[ci(cs)]
        m_new = jnp.maximum(m_prev, mb)
        p_out[ci(cs)] = jnp.exp2((s - m_new).astype(BF16))
        m_scr[ci(cs)] = m_new
        return jnp.exp2(m_prev - m_new)

    def pv(blk_prev, p_in, cs):
        acc_scr[ci(cs)] = al_scr[ci(cs)] * acc_scr[ci(cs)] + pv_dot(blk_prev, p_in, cs)

    def step(n, cur, nxt, p_prv, key0):
        s_cur, mb_cur, p_cur = cur
        for cs in chunks:
            if nxt is not None:
                scores(n + 1, nxt[0], nxt[1], cs)
            alpha = softmax(s_cur, mb_cur, p_cur, key0, cs)
            pv(jnp.maximum(n - 1, 0), p_prv, cs)
            al_scr[ci(cs)] = alpha

    buf_a = (sa_scr, mba_scr, pa_scr)
    buf_b = (sb_scr, mbb_scr, pb_scr)

    def body(jj, carry):
        step(2 * jj, buf_a, buf_b, pb_scr, None)
        step(2 * jj + 1, buf_b, buf_a, pa_scr, None)
        return carry

    @pl.when(jnp.max(ex_scr[...]) > FOX_MAX_EXCESS)
    def _():
        m_scr[...] = jnp.full(m_scr.shape, NEG_BIG, F32)
        al_scr[...] = jnp.ones_like(al_scr)
        acc_scr[...] = jnp.zeros_like(acc_scr)
        pb_scr[...] = jnp.zeros_like(pb_scr)
        for cs in chunks:
            scores(0, sa_scr, mba_scr, cs)
        lax.fori_loop(0, i, body, 0)
        step(2 * i, buf_a, (sb_scr, None), pb_scr, 0)
        step(2 * i + 1, buf_b, None, pa_scr, tk)
        for cs in chunks:
            pv(2 * i + 1, pb_scr, cs)
        emit()


def _fox(z3, qt, ft, ka, vt):
    B, S, _ = z3.shape
    tq, tk = 1024, 512
    assert tq == 2 * tk
    ones_rows = 16
    nch = tq // FOX_CHUNK
    W = FOX_HEADS * FOX_DH
    kb = Z_COLS["fk"] // FOX_DH
    kern = functools.partial(_fox_kernel, tq=tq, tk=tk)
    return pl.pallas_call(
        kern,
        out_shape=jax.ShapeDtypeStruct((B, W, S), BF16),
        grid=(B, FOX_HEADS, S // tq),
        in_specs=[pl.BlockSpec((1, FOX_DH, tq), lambda b, h, i: (b, h, i)),
                  pl.BlockSpec((1, LANES, tq), lambda b, h, i: (b, 0, i)),
                  pl.BlockSpec((1, S, FOX_DH), lambda b, h, i: (b, 0, kb + h)),
                  pl.BlockSpec((1, S, LANES), lambda b, h, i: (b, 0, 0)),
                  pl.BlockSpec((1, FOX_DH, S), lambda b, h, i: (b, h, 0))],
        out_specs=pl.BlockSpec((1, FOX_DH, tq), lambda b, h, i: (b, h, i)),
        scratch_shapes=[pltpu.VMEM((S, 2 * FOX_DH), BF16),
                        pltpu.VMEM((S // tk, tk // FOX_KSUB, FOX_DH + ones_rows, FOX_KSUB), BF16),
                        pltpu.VMEM((nch, 2 * FOX_DH, FOX_CHUNK), BF16),
                        pltpu.VMEM((nch, tk, FOX_CHUNK), F32),
                        pltpu.VMEM((nch, tk, FOX_CHUNK), F32),
                        pltpu.VMEM((nch, 1, FOX_CHUNK), F32),
                        pltpu.VMEM((nch, 1, FOX_CHUNK), F32),
                        pltpu.VMEM((nch, tk, FOX_CHUNK), BF16),
                        pltpu.VMEM((nch, tk, FOX_CHUNK), BF16),
                        pltpu.VMEM((nch, 1, FOX_CHUNK), F32),
                        pltpu.VMEM((nch, 1, FOX_CHUNK), F32),
                        pltpu.VMEM((nch, FOX_DH + ones_rows, FOX_CHUNK), F32),
                        pltpu.VMEM((nch, 1, FOX_CHUNK), F32)],
        compiler_params=_cparams(("arbitrary", "arbitrary", "arbitrary"), 48),
        name="fox",
    )(qt, ft, z3, ka, vt)


FFN_CHUNK = 256


def _tail_kernel(a_ref, yft_ref, gf_ref, x_ref, g1_ref, sc_ref, sh_ref, g2_ref, n2_ref, nf_ref,
                 wf_ref, wo_ref, wg_ref, wu_ref, wd_ref, o_ref, act_scr):
    tm = x_ref.shape[0]
    halves = [slice(r0, r0 + tm // 2) for r0 in range(0, tm, tm // 2)]
    x1s, hs = [], []
    for rs in halves:
        fo = lax.dot_general(yft_ref[0, :, rs], wf_ref[...], _TN, preferred_element_type=F32)
        merged = a_ref[rs, :].astype(F32) + gf_ref[rs, :].astype(F32) * fo
        out = jnp.dot(merged.astype(BF16), wo_ref[...], preferred_element_type=F32)
        x1s.append(x_ref[rs, :] + g1_ref[0] * out)
    for x1 in x1s:
        ms = jnp.mean(x1 * x1, axis=-1, keepdims=True)
        y = x1 * lax.rsqrt(ms + EPS) * n2_ref[...]
        hs.append((y * (1.0 + sc_ref[0]) + sh_ref[0]).astype(BF16))
    for c0 in range(0, wg_ref.shape[1], FFN_CHUNK):
        for rs, h in zip(halves, hs):
            g = jnp.dot(h, wg_ref[:, c0:c0 + FFN_CHUNK], preferred_element_type=F32)
            u = jnp.dot(h, wu_ref[:, c0:c0 + FFN_CHUNK], preferred_element_type=F32)
            act_scr[rs, c0:c0 + FFN_CHUNK] = (g * jax.nn.sigmoid(g) * u).astype(BF16)
    for rs, x1 in zip(halves, x1s):
        ffn = jnp.dot(act_scr[rs, :], wd_ref[...], preferred_element_type=F32)
        x2 = x1 + g2_ref[0] * ffn
        ms2 = jnp.mean(x2 * x2, axis=-1, keepdims=True)
        o_ref[rs, :] = x2 * lax.rsqrt(ms2 + EPS) * nf_ref[...]


def _tail(a2, yft, z2, x2, g1, sc2, sh2, g2, n2, nf, wf, wo, wg, wu, wd, S):
    T, D = x2.shape
    FF = wg.shape[1]
    assert FF % FFN_CHUNK == 0
    tm = 512
    spb = S // tm
    row = lambda i: (i, 0)
    bidx = lambda i: (i // spb, 0, 0)
    const = lambda shape: pl.BlockSpec(shape, lambda i: (0,) * len(shape),
                                       pipeline_mode=pl.Buffered(1))
    return pl.pallas_call(
        _tail_kernel,
        out_shape=jax.ShapeDtypeStruct((T, D), F32),
        grid=(T // tm,),
        in_specs=[pl.BlockSpec((tm, D), row),
                  pl.BlockSpec((1, D, tm), lambda i: (i // spb, 0, i % spb)),
                  pl.BlockSpec((tm, D), lambda i: (i, Z_COLS["gf"] // D)),
                  pl.BlockSpec((tm, D), row),
                  pl.BlockSpec((1, 1, D), bidx),
                  pl.BlockSpec((1, 1, D), bidx),
                  pl.BlockSpec((1, 1, D), bidx),
                  pl.BlockSpec((1, 1, D), bidx),
                  const((1, D)),
                  const((1, D)),
                  const((D, D)),
                  const((D, D)),
                  const((D, FF)),
                  const((D, FF)),
                  const((FF, D))],
        out_specs=pl.BlockSpec((tm, D), row),
        scratch_shapes=[pltpu.VMEM((tm, FF), BF16)],
        compiler_params=_cparams(("arbitrary",), 58),
        name="tail",
    )(a2, yft, z2, x2, g1, sc2, sh2, g2, n2, nf, wf, wo, wg, wu, wd)


def _rope_tables(S):
    half = RET_DK // 2
    freqs = ROPE_THETA ** (-np.arange(half, dtype=np.float64) / half)
    ang = np.arange(S, dtype=np.float64)[:, None] * freqs[None, :]
    cos, sin = np.cos(ang), np.sin(ang)
    return (jnp.asarray(np.concatenate([cos, cos], axis=1), F32),
            jnp.asarray(np.concatenate([-sin, sin], axis=1), F32))


def _aug_lanes(v8):
    rep = jnp.repeat(v8[..., None], 6, axis=-1)
    pad = jnp.zeros(v8.shape + (AUG_STRIDE - 6,), v8.dtype)
    return jnp.concatenate([rep, pad], axis=-1).reshape(v8.shape[:-1] + (LANES,))


def _layer(x, mod, norm1_w, w_in, b_f, ret_proj, fox_proj, w_out, norm2_w,
           w_gate, w_up, w_down, norm_out_w, tables):
    B, S, D = x.shape
    T = B * S
    sh1, sc1, g1, sh2, sc2, g2 = [m.reshape(B, 1, D) for m in jnp.split(mod, 6, axis=-1)]
    offs = np.cumsum([0, RET_HEADS * RET_DK, RET_HEADS * RET_DK, RET_HEADS * RET_DV,
                      RET_HEADS * RET_DV, FOX_HEADS * FOX_DH, FOX_HEADS * FOX_DH,
                      FOX_HEADS * FOX_DH, FOX_HEADS, D, D])
    o_ff, o_gr = int(offs[7]), int(offs[8])
    w_a = w_in[:, :o_ff].astype(BF16)
    w_b = w_in[:, o_gr:].astype(BF16)
    w_ff = _aug_lanes(w_in[:, o_ff:o_gr]).astype(BF16)
    bf_l = _aug_lanes(b_f).reshape(1, LANES)

    x2 = x.reshape(T, D)
    z, qt, vt, ft, ka = _inproj(x2, norm1_w.reshape(1, D), sc1, sh1, tables[0], tables[1],
                            w_a, w_b, w_ff, bf_l, B, S)
    z3 = z.reshape(B, S, z.shape[1])
    a = _retention(z3, ret_proj.astype(BF16), D)
    yf = _fox(z3, qt, ft, ka, vt)
    out = _tail(a.reshape(T, D), yf, z, x2, g1, sc2, sh2, g2,
                norm2_w.reshape(1, D), norm_out_w.reshape(1, D),
                fox_proj.astype(BF16), w_out.astype(BF16), w_gate.astype(BF16),
                w_up.astype(BF16), w_down.astype(BF16), S)
    return out.reshape(B, S, D)


def kernel(x, c, ada_w, ada_b, norm1_w, w_in, b_f, ret_proj, fox_proj, w_out,
           norm2_w, w_gate, w_up, w_down, norm_f_w):
    depth = ada_w.shape[0]
    assert depth == 1, "the final RMSNorm is fused into the last layer's channel mixer"
    tables = _rope_tables(x.shape[1])
    l = 0
    mod = _ada(c, ada_w[l], ada_b[l])
    return _layer(x, mod, norm1_w[l], w_in[l], b_f[l], ret_proj[l], fox_proj[l], w_out[l],
                  norm2_w[l], w_gate[l], w_up[l], w_down[l], norm_f_w, tables)
```

```python
import functools
import math

import numpy as np
import jax
import jax.numpy as jnp
from jax import lax
from jax.experimental import pallas as pl
from jax.experimental.pallas import tpu as pltpu

F32 = jnp.float32
BF16 = jnp.bfloat16

EPS = 1e-6
ROPE_THETA = 10000.0
RET_HEADS = 4
RET_DK = 128
RET_DV = 256
FOX_HEADS = 8
FOX_DH = 128
RET_CHUNK = 64
LANES = 128
AUG_STRIDE = 16

NEG_BIG = -1e30
LOG2E = math.log2(math.e)

_NT = (((1,), (1,)), ((), ()))
_TN = (((0,), (0,)), ((), ()))


def _cparams(sem, vmem_mb, flags=None):
    return pltpu.CompilerParams(dimension_semantics=sem,
                                vmem_limit_bytes=vmem_mb * 1024 * 1024, flags=flags)


def _ada_kernel(ct_ref, w_ref, b_ref, o_ref):
    ct = ct_ref[...]
    act = ct * jax.nn.sigmoid(ct)
    w = w_ref[...]
    for b in range(o_ref.shape[0]):
        o_ref[b:b + 1, :] = (jnp.sum(act[:, b:b + 1] * w, axis=0, keepdims=True)
                             + b_ref[...])


def _ada(c, w, b):
    B, D = c.shape
    N = w.shape[1]
    tn = 1536
    return pl.pallas_call(
        _ada_kernel,
        out_shape=jax.ShapeDtypeStruct((B, N), F32),
        grid=(N // tn,),
        in_specs=[pl.BlockSpec((D, B), lambda j: (0, 0)),
                  pl.BlockSpec((D, tn), lambda j: (0, j)),
                  pl.BlockSpec((1, tn), lambda j: (0, j))],
        out_specs=pl.BlockSpec((B, tn), lambda j: (0, j)),
        compiler_params=_cparams(("arbitrary",), 40),
        name="ada",
    )(c.T, w, b.reshape(1, N))


INPROJ_CHUNK = 512
Z_COLS = {"rq": 0, "rk": 512, "rv": 1024, "rg": 2048, "fk": 3072, "gr": 4096, "gf": 5120,
          "end": 6144}
W_COLS = {"rq": 0, "rk": 512, "rv": 1024, "rg": 2048, "fq": 3072, "fk": 4096, "fv": 5120,
          "gr": 6144, "gf": 7168, "end": 8192}


def _split3(v):
    p0 = v.astype(BF16)
    r1 = v - p0.astype(F32)
    p1 = r1.astype(BF16)
    r2 = r1 - p1.astype(F32)
    p2 = r2.astype(BF16)
    return p0, p1, p2


def _inproj_kernel(x0_ref, xn_ref, nw_ref, sc0_ref, sh0_ref, scn_ref, shn_ref, cos_ref, sin_ref,
                   w_ref, wff_ref, bf_ref, tri_ref,
                   z_ref, qt_ref, vt_ref, ft_ref, ka_ref, h_scr, carry,
                   *, fox_scale, rk_scale, spb):
    i = pl.program_id(0)

    def normed(x_ref, sc_ref, sh_ref):
        x = x_ref[...]
        ms = jnp.mean(x * x, axis=-1, keepdims=True)
        y = x * lax.rsqrt(ms + EPS) * nw_ref[...]
        return (y * (1.0 + sc_ref[0]) + sh_ref[0]).astype(BF16)

    @pl.when(i == 0)
    def _():
        h_scr[0] = normed(x0_ref, sc0_ref, sh0_ref)

    h = h_scr[i % 2]
    cos = cos_ref[...]
    sin = sin_ref[...]
    cw = INPROJ_CHUNK

    def chunk(w_ref, wc0):
        return jnp.dot(h, w_ref[:, wc0:wc0 + cw], preferred_element_type=F32)

    def rotary(name, scale):
        acc = chunk(w_ref, W_COLS[name])
        c0 = Z_COLS[name]
        for hh in range(cw // RET_DK):
            a = acc[:, hh * RET_DK:(hh + 1) * RET_DK]
            o = a * cos + pltpu.roll(a, RET_DK // 2, axis=1) * sin
            if scale is not None:
                o = o * scale
            z_ref[:, c0 + hh * RET_DK:c0 + (hh + 1) * RET_DK] = o.astype(BF16)

    def group(w_ref, wc0, name, width, fn):
        for d in range(0, width, cw):
            c0 = Z_COLS[name] + d
            z_ref[:, c0:c0 + cw] = fn(chunk(w_ref, wc0 + d)).astype(BF16)

    h_next = normed(xn_ref, scn_ref, shn_ref)

    xv = jnp.dot(h, wff_ref[...], preferred_element_type=F32) + bf_ref[...]
    lf = jnp.minimum(xv, 0.0) - jnp.log(1.0 + jnp.exp(-jnp.abs(xv)))
    p0, p1, p2 = _split3(lf)

    rotary("rq", None)
    rotary("rk", rk_scale)

    tri = tri_ref[...]
    cs = (jnp.dot(tri, p0, preferred_element_type=F32)
          + jnp.dot(tri, p1, preferred_element_type=F32)
          + jnp.dot(tri, p2, preferred_element_type=F32))
    first = (i % spb) == 0
    fc = cs + jnp.where(first, 0.0, carry[...])
    tm = fc.shape[0]
    carry[...] = fc[tm - 1:tm, :]

    group(w_ref, W_COLS["rv"], "rv", RET_HEADS * RET_DV, lambda a: a)

    f2l = fc * LOG2E
    ft_ref[0] = f2l.T
    f0, f1, f2 = [p.astype(F32) for p in _split3(f2l)]
    c = lax.broadcasted_iota(jnp.int32, fc.shape, 1) % AUG_STRIDE
    ka_ref[0] = jnp.where(c < 3, 1.0, jnp.where(c == 3, -f0, jnp.where(
        c == 4, -f1, jnp.where(c == 5, -f2, 0.0)))).astype(BF16)

    group(w_ref, W_COLS["fk"], "fk", FOX_HEADS * FOX_DH, lambda a: a)
    group(w_ref, W_COLS["rg"], "rg", RET_HEADS * RET_DV, lambda a: a * jax.nn.sigmoid(a))
    for d in range(0, FOX_HEADS * FOX_DH, cw):
        vt_ref[0, d:d + cw, :] = chunk(w_ref, W_COLS["fv"] + d).T.astype(BF16)
    group(w_ref, W_COLS["gr"], "gr", W_COLS["end"] - W_COLS["gr"], jax.nn.sigmoid)
    for d in range(0, FOX_HEADS * FOX_DH, cw):
        qt_ref[0, d:d + cw, :] = (chunk(w_ref, W_COLS["fq"] + d) * fox_scale).T.astype(BF16)
    h_scr[(i + 1) % 2] = h_next


def _inproj(x2, nw, sc, sh, cos_t, sin_t, w, wff, bf_l, B, S):
    T, D = x2.shape
    N = Z_COLS["end"]
    NQ = NV = FOX_HEADS * FOX_DH
    assert w.shape[1] == W_COLS["end"] == N + NQ + NV
    tm = 512
    spb = S // tm
    tri = jnp.asarray(np.tril(np.ones((tm, tm), np.float32)), BF16)
    kern = functools.partial(_inproj_kernel, fox_scale=LOG2E / math.sqrt(FOX_DH),
                             rk_scale=RET_DK ** -0.5, spb=spb)
    const = lambda shape: pl.BlockSpec(shape, lambda i: (0,) * len(shape),
                                       pipeline_mode=pl.Buffered(1))
    nxt = lambda i: jnp.minimum(i + 1, T // tm - 1)
    return pl.pallas_call(
        kern,
        out_shape=(jax.ShapeDtypeStruct((T, N), BF16),
                   jax.ShapeDtypeStruct((B, NQ, S), BF16),
                   jax.ShapeDtypeStruct((B, NV, S), BF16),
                   jax.ShapeDtypeStruct((B, LANES, S), F32),
                   jax.ShapeDtypeStruct((B, S, LANES), BF16)),
        grid=(T // tm,),
        in_specs=[const((tm, D)),
                  pl.BlockSpec((tm, D), lambda i: (nxt(i), 0)),
                  const((1, D)),
                  const((1, 1, D)),
                  const((1, 1, D)),
                  pl.BlockSpec((1, 1, D), lambda i: (nxt(i) // spb, 0, 0)),
                  pl.BlockSpec((1, 1, D), lambda i: (nxt(i) // spb, 0, 0)),
                  pl.BlockSpec((tm, LANES), lambda i: (i % spb, 0)),
                  pl.BlockSpec((tm, LANES), lambda i: (i % spb, 0)),
                  const((D, W_COLS["end"])),
                  const((D, LANES)),
                  const((1, LANES)),
                  const((tm, tm))],
        out_specs=(pl.BlockSpec((tm, N), lambda i: (i, 0)),
                   pl.BlockSpec((1, NQ, tm), lambda i: (i // spb, 0, i % spb)),
                   pl.BlockSpec((1, NV, tm), lambda i: (i // spb, 0, i % spb)),
                   pl.BlockSpec((1, LANES, tm), lambda i: (i // spb, 0, i % spb)),
                   pl.BlockSpec((1, tm, LANES), lambda i: (i // spb, i % spb, 0))),
        scratch_shapes=[pltpu.VMEM((2, tm, D), BF16), pltpu.VMEM((1, LANES), F32)],
        compiler_params=_cparams(("arbitrary",), 56),
        name="inproj",
    )(x2, x2, nw, sc, sh, sc, sh, cos_t, sin_t, w, wff, bf_l, tri)


def _ret_consts(L):
    hs = np.arange(RET_HEADS, dtype=np.float64)
    log_gamma = np.log(1.0 - np.exp2(-5.0 - hs))
    idx = np.arange(L, dtype=np.float64)
    dist = np.abs(idx[:, None] - idx[None, :])
    chunk_ok = (idx[None, :] // RET_CHUNK) <= (idx[:, None] // RET_CHUNK)
    dmask = np.exp(log_gamma[:, None, None] * dist[None]) * chunk_ok[None]
    qdec = np.exp(log_gamma[:, None] * idx[None, :])[..., None]
    kdec = np.exp(log_gamma[:, None] * (L - idx)[None, :])[..., None]
    bdec = [float(v) for v in np.exp(log_gamma * L)]
    return (jnp.asarray(dmask, F32), jnp.asarray(qdec, F32), jnp.asarray(kdec, F32), bdec)


def _ret_kernel(rq_ref, rk_ref, rv_ref, rg_ref, gr_ref, dm_ref, qd_ref, kd_ref, wr_ref,
                o_ref, st_scr, y_scr, *, bdec):
    @pl.when(pl.program_id(1) == 0)
    def _():
        st_scr[...] = jnp.zeros_like(st_scr)

    for h in range(RET_HEADS):
        q = rq_ref[0, :, h * RET_DK:(h + 1) * RET_DK]
        k = rk_ref[0, :, h * RET_DK:(h + 1) * RET_DK]
        v = rv_ref[0, :, h * RET_DV:(h + 1) * RET_DV]
        s = lax.dot_general(q, k, _NT, preferred_element_type=F32) * dm_ref[h]
        intra = jnp.dot(s.astype(BF16), v, preferred_element_type=F32)
        st = st_scr[h]
        inter = jnp.dot(q, st.astype(BF16), preferred_element_type=F32) * qd_ref[h]
        ks = (k.astype(F32) * kd_ref[h]).astype(BF16)
        st_scr[h] = bdec[h] * st + lax.dot_general(ks, v, _TN, preferred_element_type=F32)
        ro = intra + inter
        ron = ro * lax.rsqrt(jnp.mean(ro * ro, axis=-1, keepdims=True) + EPS)
        g = rg_ref[0, :, h * RET_DV:(h + 1) * RET_DV].astype(F32)
        y_scr[:, h * RET_DV:(h + 1) * RET_DV] = (g * ron).astype(BF16)

    proj = jnp.dot(y_scr[...], wr_ref[...], preferred_element_type=F32)
    o_ref[0] = (gr_ref[0].astype(F32) * proj).astype(BF16)


def _retention(z3, wr, D):
    B, S, _ = z3.shape
    L = 512
    dmask, qdec, kdec, bdec = _ret_consts(L)
    QK = RET_HEADS * RET_DK
    V = RET_HEADS * RET_DV
    kern = functools.partial(_ret_kernel, bdec=bdec)
    full3 = lambda b, i: (0, 0, 0)
    return pl.pallas_call(
        kern,
        out_shape=jax.ShapeDtypeStruct((B, S, D), BF16),
        grid=(B, S // L),
        in_specs=[pl.BlockSpec((1, L, QK), lambda b, i: (b, i, 0)),
                  pl.BlockSpec((1, L, QK), lambda b, i: (b, i, 1)),
                  pl.BlockSpec((1, L, V), lambda b, i: (b, i, 1)),
                  pl.BlockSpec((1, L, V), lambda b, i: (b, i, 2)),
                  pl.BlockSpec((1, L, D), lambda b, i: (b, i, Z_COLS["gr"] // D)),
                  pl.BlockSpec((RET_HEADS, L, L), full3),
                  pl.BlockSpec((RET_HEADS, L, 1), full3),
                  pl.BlockSpec((RET_HEADS, L, 1), full3),
                  pl.BlockSpec((V, D), lambda b, i: (0, 0))],
        out_specs=pl.BlockSpec((1, L, D), lambda b, i: (b, i, 0)),
        scratch_shapes=[pltpu.VMEM((RET_HEADS, RET_DK, RET_DV), F32),
                        pltpu.VMEM((L, V), BF16)],
        compiler_params=_cparams(("arbitrary", "arbitrary"), 40),
        name="ret",
    )(z3, z3, z3, z3, z3, dmask, qdec, kdec, wr)


FOX_CHUNK = 256
FOX_KSUB = 256
FOX_MAX_EXCESS = 64.0


def _fox_kernel(qt_ref, ft_ref, k_ref, ka_ref, vt_ref, o_ref,
                kx_scr, vt_scr, qxt_scr, sa_scr, sb_scr, mba_scr, mbb_scr, pa_scr, pb_scr,
                m_scr, al_scr, acc_scr, ex_scr, *, tq, tk):
    h = pl.program_id(1)
    i = pl.program_id(2)
    nkb, nks = vt_scr.shape[0], vt_scr.shape[1]

    @pl.when(i == 0)
    def _():
        kx_scr[:, :FOX_DH] = k_ref[0]
        kx_scr[:, FOX_DH:] = ka_ref[0]
        qxt_scr[:, FOX_DH:, :] = jnp.zeros(
            (qxt_scr.shape[0], qxt_scr.shape[1] - FOX_DH, FOX_CHUNK), BF16)
        for n in range(nkb):
            for kk in range(nks):
                k0 = n * tk + kk * FOX_KSUB
                vt_scr[n, kk, :FOX_DH, :] = vt_ref[0, :, k0:k0 + FOX_KSUB]
                vt_scr[n, kk, FOX_DH:, :] = jnp.ones(
                    (vt_scr.shape[2] - FOX_DH, FOX_KSUB), BF16)

    f_row = ft_ref[0, pl.ds(pl.multiple_of(AUG_STRIDE * h, AUG_STRIDE), 8), :][0:1, :]
    f0, f1, f2 = [p.astype(F32) for p in _split3(f_row)]
    sub = lax.broadcasted_iota(jnp.int32, (AUG_STRIDE, tq), 0)
    aug = jnp.where(sub == 0, f0, jnp.where(sub == 1, f1, jnp.where(
        sub == 2, f2, jnp.where(sub < 6, 1.0, 0.0))))
    aug = aug.astype(BF16)
    aug_rows = pl.ds(pl.multiple_of(FOX_DH + AUG_STRIDE * h, AUG_STRIDE), AUG_STRIDE)

    chunks = [slice(c0, c0 + FOX_CHUNK) for c0 in range(0, tq, FOX_CHUNK)]
    ci = lambda cs: cs.start // FOX_CHUNK
    for cs in chunks:
        qxt_scr[ci(cs), :FOX_DH, :] = qt_ref[0, :, cs]
        qxt_scr[ci(cs), aug_rows, :] = aug[:, cs]

    def qk(blk, cs):
        off = pl.multiple_of(blk * tk, tk)
        return jnp.dot(kx_scr[pl.ds(off, tk), :], qxt_scr[ci(cs)],
                       preferred_element_type=F32)

    def causal(s, cs, key0):
        key = lax.broadcasted_iota(jnp.int32, s.shape, 0) + key0
        qry = lax.broadcasted_iota(jnp.int32, s.shape, 1) + cs.start
        return jnp.where(key <= qry, s, NEG_BIG)

    def emit():
        for cs in chunks:
            acc = acc_scr[ci(cs)]
            o_ref[0, :, cs] = (acc[:FOX_DH] / acc[FOX_DH:FOX_DH + 1]).astype(BF16)

    def pv_dot(blk, p, cs):
        return sum(jnp.dot(vt_scr[blk, kk], p[ci(cs), kk * FOX_KSUB:(kk + 1) * FOX_KSUB, :],
                           preferred_element_type=F32) for kk in range(nks))

    def pv_acc(blk_prev, p_prv, cs):
        acc_scr[ci(cs)] = al_scr[ci(cs)] * (acc_scr[ci(cs)] + pv_dot(blk_prev, p_prv, cs))

    def lazy_chunk(blk, cs, p_cur, key0):
        s = qk(blk, cs)
        if key0 is not None:
            s = causal(s, cs, key0)
        mb = jnp.max(s, axis=0, keepdims=True)
        r_old = m_scr[ci(cs)]
        p_cur[ci(cs)] = jnp.exp2((s - r_old).astype(BF16))
        r_new = jnp.maximum(r_old, mb)
        m_scr[ci(cs)] = r_new
        ex_scr[ci(cs)] = jnp.maximum(ex_scr[ci(cs)], mb - r_old)
        return jnp.exp2(r_old - r_new)

    def lazy_step(blk, blk_prev, p_cur, p_prv):
        for cs in chunks:
            alpha = lazy_chunk(blk, cs, p_cur, None)
            pv_acc(blk_prev, p_prv, cs)
            al_scr[ci(cs)] = alpha

    def own_block(cs):
        return (2 * i, 0) if cs.start < tk else (2 * i + 1, tk)

    for cs in chunks:
        blk, key0 = own_block(cs)
        s = causal(qk(blk, cs), cs, key0)
        mb = jnp.max(s, axis=0, keepdims=True)
        m_scr[ci(cs)] = mb
        pa_scr[ci(cs)] = jnp.exp2((s - mb).astype(BF16))
        al_scr[ci(cs)] = jnp.ones((1, FOX_CHUNK), F32)
        acc_scr[ci(cs)] = jnp.zeros((acc_scr.shape[1], FOX_CHUNK), F32)
        ex_scr[ci(cs)] = jnp.zeros((1, FOX_CHUNK), F32)
    for cs in chunks:
        blk, _ = own_block(cs)
        if cs.start < tk:
            pb_scr[ci(cs)] = jnp.zeros((tk, FOX_CHUNK), BF16)
            pv_acc(blk, pa_scr, cs)
        else:
            alpha = lazy_chunk(2 * i, cs, pb_scr, None)
            pv_acc(blk, pa_scr, cs)
            al_scr[ci(cs)] = alpha

    def lazy_body(t, carry):
        jj = i - 1 - t
        lazy_step(2 * jj + 1, 2 * jj + 2, pa_scr, pb_scr)
        lazy_step(2 * jj, 2 * jj + 1, pb_scr, pa_scr)
        return carry

    lax.fori_loop(0, i, lazy_body, 0)
    for cs in chunks:
        pv_acc(0, pb_scr, cs)
    emit()

    def scores(blk, s_out, mb_out, cs):
        s = qk(blk, cs)
        s_out[ci(cs)] = s
        if mb_out is not None:
            mb_out[ci(cs)] = jnp.max(s, axis=0, keepdims=True)

    def softmax(s_in, mb_in, p_out, key0, cs):
        s = s_in[ci(cs)]
        if key0 is None:
            mb = mb_in[ci(cs)]
        else:
            s = causal(s, cs, key0)
            mb = jnp.max(s, axis=0, keepdims=True)
        m_prev = m_scr[ci(cs)]
        m_new = jnp.maximum(m_prev, mb)
        p_out[ci(cs)] = jnp.exp2((s - m_new).astype(BF16))
        m_scr[ci(cs)] = m_new
        return jnp.exp2(m_prev - m_new)

    def pv(blk_prev, p_in, cs):
        acc_scr[ci(cs)] = al_scr[ci(cs)] * acc_scr[ci(cs)] + pv_dot(blk_prev, p_in, cs)

    def step(n, cur, nxt, p_prv, key0):
        s_cur, mb_cur, p_cur = cur
        for cs in chunks:
            if nxt is not None:
                scores(n + 1, nxt[0], nxt[1], cs)
            alpha = softmax(s_cur, mb_cur, p_cur, key0, cs)
            pv(jnp.maximum(n - 1, 0), p_prv, cs)
            al_scr[ci(cs)] = alpha

    buf_a = (sa_scr, mba_scr, pa_scr)
    buf_b = (sb_scr, mbb_scr, pb_scr)

    def body(jj, carry):
        step(2 * jj, buf_a, buf_b, pb_scr, None)
        step(2 * jj + 1, buf_b, buf_a, pa_scr, None)
        return carry

    @pl.when(jnp.max(ex_scr[...]) > FOX_MAX_EXCESS)
    def _():
        m_scr[...] = jnp.full(m_scr.shape, NEG_BIG, F32)
        al_scr[...] = jnp.ones_like(al_scr)
        acc_scr[...] = jnp.zeros_like(acc_scr)
        pb_scr[...] = jnp.zeros_like(pb_scr)
        for cs in chunks:
            scores(0, sa_scr, mba_scr, cs)
        lax.fori_loop(0, i, body, 0)
        step(2 * i, buf_a, (sb_scr, None), pb_scr, 0)
        step(2 * i + 1, buf_b, None, pa_scr, tk)
        for cs in chunks:
            pv(2 * i + 1, pb_scr, cs)
        emit()


def _fox(z3, qt, ft, ka, vt):
    B, S, _ = z3.shape
    tq, tk = 1024, 512
    assert tq == 2 * tk
    ones_rows = 16
    nch = tq // FOX_CHUNK
    W = FOX_HEADS * FOX_DH
    kb = Z_COLS["fk"] // FOX_DH
    kern = functools.partial(_fox_kernel, tq=tq, tk=tk)
    return pl.pallas_call(
        kern,
        out_shape=jax.ShapeDtypeStruct((B, W, S), BF16),
        grid=(B, FOX_HEADS, S // tq),
        in_specs=[pl.BlockSpec((1, FOX_DH, tq), lambda b, h, i: (b, h, i)),
                  pl.BlockSpec((1, LANES, tq), lambda b, h, i: (b, 0, i)),
                  pl.BlockSpec((1, S, FOX_DH), lambda b, h, i: (b, 0, kb + h)),
                  pl.BlockSpec((1, S, LANES), lambda b, h, i: (b, 0, 0)),
                  pl.BlockSpec((1, FOX_DH, S), lambda b, h, i: (b, h, 0))],
        out_specs=pl.BlockSpec((1, FOX_DH, tq), lambda b, h, i: (b, h, i)),
        scratch_shapes=[pltpu.VMEM((S, 2 * FOX_DH), BF16),
                        pltpu.VMEM((S // tk, tk // FOX_KSUB, FOX_DH + ones_rows, FOX_KSUB), BF16),
                        pltpu.VMEM((nch, 2 * FOX_DH, FOX_CHUNK), BF16),
                        pltpu.VMEM((nch, tk, FOX_CHUNK), F32),
                        pltpu.VMEM((nch, tk, FOX_CHUNK), F32),
                        pltpu.VMEM((nch, 1, FOX_CHUNK), F32),
                        pltpu.VMEM((nch, 1, FOX_CHUNK), F32),
                        pltpu.VMEM((nch, tk, FOX_CHUNK), BF16),
                        pltpu.VMEM((nch, tk, FOX_CHUNK), BF16),
                        pltpu.VMEM((nch, 1, FOX_CHUNK), F32),
                        pltpu.VMEM((nch, 1, FOX_CHUNK), F32),
                        pltpu.VMEM((nch, FOX_DH + ones_rows, FOX_CHUNK), F32),
                        pltpu.VMEM((nch, 1, FOX_CHUNK), F32)],
        compiler_params=_cparams(("arbitrary", "arbitrary", "arbitrary"), 48),
        name="fox",
    )(qt, ft, z3, ka, vt)


FFN_CHUNK = 256


def _tail_kernel(a_ref, yft_ref, gf_ref, x_ref, g1_ref, sc_ref, sh_ref, g2_ref, n2_ref, nf_ref,
                 wf_ref, wo_ref, wg_ref, wu_ref, wd_ref, o_ref, act_scr):
    tm = x_ref.shape[0]
    halves = [slice(r0, r0 + tm // 2) for r0 in range(0, tm, tm // 2)]
    x1s, hs = [], []
    for rs in halves:
        fo = lax.dot_general(yft_ref[0, :, rs], wf_ref[...], _TN, preferred_element_type=F32)
        merged = a_ref[rs, :].astype(F32) + gf_ref[rs, :].astype(F32) * fo
        out = jnp.dot(merged.astype(BF16), wo_ref[...], preferred_element_type=F32)
        x1s.append(x_ref[rs, :] + g1_ref[0] * out)
    for x1 in x1s:
        ms = jnp.mean(x1 * x1, axis=-1, keepdims=True)
        y = x1 * lax.rsqrt(ms + EPS) * n2_ref[...]
        hs.append((y * (1.0 + sc_ref[0]) + sh_ref[0]).astype(BF16))
    for c0 in range(0, wg_ref.shape[1], FFN_CHUNK):
        for rs, h in zip(halves, hs):
            g = jnp.dot(h, wg_ref[:, c0:c0 + FFN_CHUNK], preferred_element_type=F32)
            u = jnp.dot(h, wu_ref[:, c0:c0 + FFN_CHUNK], preferred_element_type=F32)
            act_scr[rs, c0:c0 + FFN_CHUNK] = (g * jax.nn.sigmoid(g) * u).astype(BF16)
    for rs, x1 in zip(halves, x1s):
        ffn = jnp.dot(act_scr[rs, :], wd_ref[...], preferred_element_type=F32)
        x2 = x1 + g2_ref[0] * ffn
        ms2 = jnp.mean(x2 * x2, axis=-1, keepdims=True)
        o_ref[rs, :] = x2 * lax.rsqrt(ms2 + EPS) * nf_ref[...]


def _tail(a2, yft, z2, x2, g1, sc2, sh2, g2, n2, nf, wf, wo, wg, wu, wd, S):
    T, D = x2.shape
    FF = wg.shape[1]
    assert FF % FFN_CHUNK == 0
    tm = 512
    spb = S // tm
    row = lambda i: (i, 0)
    bidx = lambda i: (i // spb, 0, 0)
    const = lambda shape: pl.BlockSpec(shape, lambda i: (0,) * len(shape),
                                       pipeline_mode=pl.Buffered(1))
    return pl.pallas_call(
        _tail_kernel,
        out_shape=jax.ShapeDtypeStruct((T, D), F32),
        grid=(T // tm,),
        in_specs=[pl.BlockSpec((tm, D), row),
                  pl.BlockSpec((1, D, tm), lambda i: (i // spb, 0, i % spb)),
                  pl.BlockSpec((tm, D), lambda i: (i, Z_COLS["gf"] // D)),
                  pl.BlockSpec((tm, D), row),
                  pl.BlockSpec((1, 1, D), bidx),
                  pl.BlockSpec((1, 1, D), bidx),
                  pl.BlockSpec((1, 1, D), bidx),
                  pl.BlockSpec((1, 1, D), bidx),
                  const((1, D)),
                  const((1, D)),
                  const((D, D)),
                  const((D, D)),
                  const((D, FF)),
                  const((D, FF)),
                  const((FF, D))],
        out_specs=pl.BlockSpec((tm, D), row),
        scratch_shapes=[pltpu.VMEM((tm, FF), BF16)],
        compiler_params=_cparams(("arbitrary",), 58),
        name="tail",
    )(a2, yft, z2, x2, g1, sc2, sh2, g2, n2, nf, wf, wo, wg, wu, wd)


def _castw_kernel(wt_ref, o_ref):
    o_ref[...] = wt_ref[...].T.astype(BF16)


def _cast_inproj_weights(w_t, n_lead, gap):
    NT, D = w_t.shape
    tn = 1024
    n_out = NT - gap
    sub = 8
    assert n_lead % tn == 0 and n_out % tn == 0 and gap % sub == 0
    row0 = lambda j: sub * jnp.where(j < n_lead // tn, j * (tn // sub), j * (tn // sub) + gap // sub)
    return pl.pallas_call(
        _castw_kernel,
        out_shape=jax.ShapeDtypeStruct((D, n_out), BF16),
        grid=(n_out // tn,),
        in_specs=[pl.BlockSpec((pl.Element(tn), pl.Element(D)), lambda j: (row0(j), 0))],
        out_specs=pl.BlockSpec((D, tn), lambda j: (0, j)),
        compiler_params=_cparams(("arbitrary",), 32),
        name="castw",
    )(w_t)


def _rope_tables(S):
    half = RET_DK // 2
    freqs = ROPE_THETA ** (-np.arange(half, dtype=np.float64) / half)
    ang = np.arange(S, dtype=np.float64)[:, None] * freqs[None, :]
    cos, sin = np.cos(ang), np.sin(ang)
    return (jnp.asarray(np.concatenate([cos, cos], axis=1), F32),
            jnp.asarray(np.concatenate([-sin, sin], axis=1), F32))


def _aug_lanes(v8):
    rep = jnp.repeat(v8[..., None], 6, axis=-1)
    pad = jnp.zeros(v8.shape + (AUG_STRIDE - 6,), v8.dtype)
    return jnp.concatenate([rep, pad], axis=-1).reshape(v8.shape[:-1] + (LANES,))


def _layer(x, mod, norm1_w, w_in, b_f, ret_proj, fox_proj, w_out, norm2_w,
           w_gate, w_up, w_down, norm_out_w, tables):
    B, S, D = x.shape
    T = B * S
    sh1, sc1, g1, sh2, sc2, g2 = [m.reshape(B, 1, D) for m in jnp.split(mod, 6, axis=-1)]
    offs = np.cumsum([0, RET_HEADS * RET_DK, RET_HEADS * RET_DK, RET_HEADS * RET_DV,
                      RET_HEADS * RET_DV, FOX_HEADS * FOX_DH, FOX_HEADS * FOX_DH,
                      FOX_HEADS * FOX_DH, FOX_HEADS, D, D])
    o_ff, o_gr = int(offs[7]), int(offs[8])
    w_main = _cast_inproj_weights(jnp.swapaxes(w_in, 0, 1), o_ff, o_gr - o_ff)
    w_ff = _aug_lanes(w_in[:, o_ff:o_gr]).astype(BF16)
    bf_l = _aug_lanes(b_f).reshape(1, LANES)

    x2 = x.reshape(T, D)
    z, qt, vt, ft, ka = _inproj(x2, norm1_w.reshape(1, D), sc1, sh1, tables[0], tables[1],
                            w_main, w_ff, bf_l, B, S)
    z3 = z.reshape(B, S, z.shape[1])
    a = _retention(z3, ret_proj.astype(BF16), D)
    yf = _fox(z3, qt, ft, ka, vt)
    out = _tail(a.reshape(T, D), yf, z, x2, g1, sc2, sh2, g2,
                norm2_w.reshape(1, D), norm_out_w.reshape(1, D),
                fox_proj.astype(BF16), w_out.astype(BF16), w_gate.astype(BF16),
                w_up.astype(BF16), w_down.astype(BF16), S)
    return out.reshape(B, S, D)


def kernel(x, c, ada_w, ada_b, norm1_w, w_in, b_f, ret_proj, fox_proj, w_out,
           norm2_w, w_gate, w_up, w_down, norm_f_w):
    depth = ada_w.shape[0]
    assert depth == 1, "the final RMSNorm is fused into the last layer's channel mixer"
    tables = _rope_tables(x.shape[1])
    l = 0
    mod = _ada(c, ada_w[l], ada_b[l])
    return _layer(x, mod, norm1_w[l], w_in[l], b_f[l], ret_proj[l], fox_proj[l], w_out[l],
                  norm2_w[l], w_gate[l], w_up[l], w_down[l], norm_f_w, tables)
```

```python
import functools
import math

import numpy as np
import jax
import jax.numpy as jnp
from jax import lax
from jax.experimental import pallas as pl
from jax.experimental.pallas import tpu as pltpu

F32 = jnp.float32
BF16 = jnp.bfloat16

EPS = 1e-6
ROPE_THETA = 10000.0
RET_HEADS = 4
RET_DK = 128
RET_DV = 256
FOX_HEADS = 8
FOX_DH = 128
RET_CHUNK = 64
LANES = 128
AUG_STRIDE = 16

NEG_BIG = -1e30
LOG2E = math.log2(math.e)

_NT = (((1,), (1,)), ((), ()))
_TN = (((0,), (0,)), ((), ()))


def _cparams(sem, vmem_mb, flags=None):
    return pltpu.CompilerParams(dimension_semantics=sem,
                                vmem_limit_bytes=vmem_mb * 1024 * 1024, flags=flags)


def _ada_kernel(ct_ref, w_ref, b_ref, o_ref):
    ct = ct_ref[...]
    act = ct * jax.nn.sigmoid(ct)
    w = w_ref[...]
    for b in range(o_ref.shape[0]):
        o_ref[b:b + 1, :] = (jnp.sum(act[:, b:b + 1] * w, axis=0, keepdims=True)
                             + b_ref[...])


def _ada(c, w, b):
    B, D = c.shape
    N = w.shape[1]
    tn = 1536
    return pl.pallas_call(
        _ada_kernel,
        out_shape=jax.ShapeDtypeStruct((B, N), F32),
        grid=(N // tn,),
        in_specs=[pl.BlockSpec((D, B), lambda j: (0, 0)),
                  pl.BlockSpec((D, tn), lambda j: (0, j)),
                  pl.BlockSpec((1, tn), lambda j: (0, j))],
        out_specs=pl.BlockSpec((B, tn), lambda j: (0, j)),
        compiler_params=_cparams(("arbitrary",), 40),
        name="ada",
    )(c.T, w, b.reshape(1, N))


INPROJ_CHUNK = 512
FOX_TQ = 1024
Z_COLS = {"rq": 0, "rk": 512, "rv": 1024, "rg": 2048, "fk": 3072, "gr": 4096, "gf": 5120,
          "end": 6144}
W_COLS = {"rq": 0, "rk": 512, "rv": 1024, "rg": 2048, "fq": 3072, "fk": 4096, "fv": 5120,
          "gr": 6144, "gf": 7168, "end": 8192}


def _split3(v):
    p0 = v.astype(BF16)
    r1 = v - p0.astype(F32)
    p1 = r1.astype(BF16)
    r2 = r1 - p1.astype(F32)
    p2 = r2.astype(BF16)
    return p0, p1, p2


def _inproj_kernel(x0_ref, xn_ref, nw_ref, sc0_ref, sh0_ref, scn_ref, shn_ref, cos_ref, sin_ref,
                   w_ref, wff_ref, bf_ref, tri_ref,
                   z_ref, qt_ref, vt_ref, ft_ref, ka_ref, h_scr, carry,
                   *, fox_scale, rk_scale, spb):
    i = pl.program_id(0)

    def normed(x_ref, sc_ref, sh_ref):
        x = x_ref[...]
        ms = jnp.mean(x * x, axis=-1, keepdims=True)
        y = x * lax.rsqrt(ms + EPS) * nw_ref[...]
        return (y * (1.0 + sc_ref[0]) + sh_ref[0]).astype(BF16)

    @pl.when(i == 0)
    def _():
        h_scr[0] = normed(x0_ref, sc0_ref, sh0_ref)

    h = h_scr[i % 2]
    cos = cos_ref[...]
    sin = sin_ref[...]
    cw = INPROJ_CHUNK

    def chunk(w_ref, wc0):
        return jnp.dot(h, w_ref[:, wc0:wc0 + cw], preferred_element_type=F32)

    def rotary(name, scale):
        acc = chunk(w_ref, W_COLS[name])
        c0 = Z_COLS[name]
        for hh in range(cw // RET_DK):
            a = acc[:, hh * RET_DK:(hh + 1) * RET_DK]
            o = a * cos + pltpu.roll(a, RET_DK // 2, axis=1) * sin
            if scale is not None:
                o = o * scale
            z_ref[:, c0 + hh * RET_DK:c0 + (hh + 1) * RET_DK] = o.astype(BF16)

    def group(w_ref, wc0, name, width, fn):
        for d in range(0, width, cw):
            c0 = Z_COLS[name] + d
            z_ref[:, c0:c0 + cw] = fn(chunk(w_ref, wc0 + d)).astype(BF16)

    h_next = normed(xn_ref, scn_ref, shn_ref)

    xv = jnp.dot(h, wff_ref[...], preferred_element_type=F32) + bf_ref[...]
    lf = jnp.minimum(xv, 0.0) - jnp.log(1.0 + jnp.exp(-jnp.abs(xv)))
    p0, p1, p2 = _split3(lf)

    rotary("rq", None)
    rotary("rk", rk_scale)

    tri = tri_ref[...]
    cs = (jnp.dot(tri, p0, preferred_element_type=F32)
          + jnp.dot(tri, p1, preferred_element_type=F32)
          + jnp.dot(tri, p2, preferred_element_type=F32))
    first = (i % spb) == 0
    fc = cs + jnp.where(first, 0.0, carry[...])
    tm = fc.shape[0]
    carry[...] = fc[tm - 1:tm, :]

    group(w_ref, W_COLS["rv"], "rv", RET_HEADS * RET_DV, lambda a: a)

    f2l = fc * LOG2E
    ft_ref[0, 0] = f2l.T
    f0, f1, f2 = [p.astype(F32) for p in _split3(f2l)]
    c = lax.broadcasted_iota(jnp.int32, fc.shape, 1) % AUG_STRIDE
    ka_ref[0] = jnp.where(c < 3, 1.0, jnp.where(c == 3, -f0, jnp.where(
        c == 4, -f1, jnp.where(c == 5, -f2, 0.0)))).astype(BF16)

    group(w_ref, W_COLS["fk"], "fk", FOX_HEADS * FOX_DH, lambda a: a)
    group(w_ref, W_COLS["rg"], "rg", RET_HEADS * RET_DV, lambda a: a * jax.nn.sigmoid(a))
    for d in range(0, FOX_HEADS * FOX_DH, cw):
        vt_ref[0, d:d + cw, :] = chunk(w_ref, W_COLS["fv"] + d).T.astype(BF16)
    group(w_ref, W_COLS["gr"], "gr", W_COLS["end"] - W_COLS["gr"], jax.nn.sigmoid)
    for d in range(0, FOX_HEADS * FOX_DH, cw):
        qt_ref[0, 0, d:d + cw, :] = (chunk(w_ref, W_COLS["fq"] + d) * fox_scale).T.astype(BF16)
    h_scr[(i + 1) % 2] = h_next


def _inproj(x2, nw, sc, sh, cos_t, sin_t, w, wff, bf_l, B, S):
    T, D = x2.shape
    N = Z_COLS["end"]
    NQ = NV = FOX_HEADS * FOX_DH
    assert w.shape[1] == W_COLS["end"] == N + NQ + NV
    tm = 512
    spb = S // tm
    tpq = FOX_TQ // tm
    assert spb % tpq == 0
    tri = jnp.asarray(np.tril(np.ones((tm, tm), np.float32)), BF16)
    kern = functools.partial(_inproj_kernel, fox_scale=LOG2E / math.sqrt(FOX_DH),
                             rk_scale=RET_DK ** -0.5, spb=spb)
    const = lambda shape: pl.BlockSpec(shape, lambda i: (0,) * len(shape),
                                       pipeline_mode=pl.Buffered(1))
    nxt = lambda i: jnp.minimum(i + 1, T // tm - 1)
    return pl.pallas_call(
        kern,
        out_shape=(jax.ShapeDtypeStruct((T, N), BF16),
                   jax.ShapeDtypeStruct((B, S // FOX_TQ, NQ, FOX_TQ), BF16),
                   jax.ShapeDtypeStruct((B, NV, S), BF16),
                   jax.ShapeDtypeStruct((B, S // FOX_TQ, LANES, FOX_TQ), F32),
                   jax.ShapeDtypeStruct((B, S, LANES), BF16)),
        grid=(T // tm,),
        in_specs=[const((tm, D)),
                  pl.BlockSpec((tm, D), lambda i: (nxt(i), 0)),
                  const((1, D)),
                  const((1, 1, D)),
                  const((1, 1, D)),
                  pl.BlockSpec((1, 1, D), lambda i: (nxt(i) // spb, 0, 0)),
                  pl.BlockSpec((1, 1, D), lambda i: (nxt(i) // spb, 0, 0)),
                  pl.BlockSpec((tm, LANES), lambda i: (i % spb, 0)),
                  pl.BlockSpec((tm, LANES), lambda i: (i % spb, 0)),
                  const((D, W_COLS["end"])),
                  const((D, LANES)),
                  const((1, LANES)),
                  const((tm, tm))],
        out_specs=(pl.BlockSpec((tm, N), lambda i: (i, 0)),
                   pl.BlockSpec((1, 1, NQ, tm), lambda i: (i // spb, (i % spb) // tpq, 0, i % tpq)),
                   pl.BlockSpec((1, NV, tm), lambda i: (i // spb, 0, i % spb)),
                   pl.BlockSpec((1, 1, LANES, tm),
                                lambda i: (i // spb, (i % spb) // tpq, 0, i % tpq)),
                   pl.BlockSpec((1, tm, LANES), lambda i: (i // spb, i % spb, 0))),
        scratch_shapes=[pltpu.VMEM((2, tm, D), BF16), pltpu.VMEM((1, LANES), F32)],
        compiler_params=_cparams(("arbitrary",), 56),
        name="inproj",
    )(x2, x2, nw, sc, sh, sc, sh, cos_t, sin_t, w, wff, bf_l, tri)


def _ret_consts(L):
    hs = np.arange(RET_HEADS, dtype=np.float64)
    log_gamma = np.log(1.0 - np.exp2(-5.0 - hs))
    idx = np.arange(L, dtype=np.float64)
    dist = np.abs(idx[:, None] - idx[None, :])
    chunk_ok = (idx[None, :] // RET_CHUNK) <= (idx[:, None] // RET_CHUNK)
    dmask = np.exp(log_gamma[:, None, None] * dist[None]) * chunk_ok[None]
    qdec = np.exp(log_gamma[:, None] * idx[None, :])[..., None]
    kdec = np.exp(log_gamma[:, None] * (L - idx)[None, :])[..., None]
    bdec = [float(v) for v in np.exp(log_gamma * L)]
    return (jnp.asarray(dmask, F32), jnp.asarray(qdec, F32), jnp.asarray(kdec, F32), bdec)


def _ret_kernel(rq_ref, rk_ref, rv_ref, rg_ref, gr_ref, dm_ref, qd_ref, kd_ref, wr_ref,
                o_ref, st_scr, y_scr, *, bdec):
    @pl.when(pl.program_id(1) == 0)
    def _():
        st_scr[...] = jnp.zeros_like(st_scr)

    for h in range(RET_HEADS):
        q = rq_ref[0, :, h * RET_DK:(h + 1) * RET_DK]
        k = rk_ref[0, :, h * RET_DK:(h + 1) * RET_DK]
        v = rv_ref[0, :, h * RET_DV:(h + 1) * RET_DV]
        s = lax.dot_general(q, k, _NT, preferred_element_type=F32) * dm_ref[h]
        intra = jnp.dot(s.astype(BF16), v, preferred_element_type=F32)
        st = st_scr[h]
        inter = jnp.dot(q, st.astype(BF16), preferred_element_type=F32) * qd_ref[h]
        ks = (k.astype(F32) * kd_ref[h]).astype(BF16)
        st_scr[h] = bdec[h] * st + lax.dot_general(ks, v, _TN, preferred_element_type=F32)
        ro = intra + inter
        ron = ro * lax.rsqrt(jnp.mean(ro * ro, axis=-1, keepdims=True) + EPS)
        g = rg_ref[0, :, h * RET_DV:(h + 1) * RET_DV].astype(F32)
        y_scr[:, h * RET_DV:(h + 1) * RET_DV] = (g * ron).astype(BF16)

    proj = jnp.dot(y_scr[...], wr_ref[...], preferred_element_type=F32)
    o_ref[0] = (gr_ref[0].astype(F32) * proj).astype(BF16)


def _retention(z3, wr, D):
    B, S, _ = z3.shape
    L = 512
    dmask, qdec, kdec, bdec = _ret_consts(L)
    QK = RET_HEADS * RET_DK
    V = RET_HEADS * RET_DV
    kern = functools.partial(_ret_kernel, bdec=bdec)
    full3 = lambda b, i: (0, 0, 0)
    return pl.pallas_call(
        kern,
        out_shape=jax.ShapeDtypeStruct((B, S, D), BF16),
        grid=(B, S // L),
        in_specs=[pl.BlockSpec((1, L, QK), lambda b, i: (b, i, 0)),
                  pl.BlockSpec((1, L, QK), lambda b, i: (b, i, 1)),
                  pl.BlockSpec((1, L, V), lambda b, i: (b, i, 1)),
                  pl.BlockSpec((1, L, V), lambda b, i: (b, i, 2)),
                  pl.BlockSpec((1, L, D), lambda b, i: (b, i, Z_COLS["gr"] // D)),
                  pl.BlockSpec((RET_HEADS, L, L), full3),
                  pl.BlockSpec((RET_HEADS, L, 1), full3),
                  pl.BlockSpec((RET_HEADS, L, 1), full3),
                  pl.BlockSpec((V, D), lambda b, i: (0, 0))],
        out_specs=pl.BlockSpec((1, L, D), lambda b, i: (b, i, 0)),
        scratch_shapes=[pltpu.VMEM((RET_HEADS, RET_DK, RET_DV), F32),
                        pltpu.VMEM((L, V), BF16)],
        compiler_params=_cparams(("arbitrary", "arbitrary"), 40),
        name="ret",
    )(z3, z3, z3, z3, z3, dmask, qdec, kdec, wr)


FOX_CHUNK = 256
FOX_KSUB = 256
FOX_MAX_EXCESS = 64.0


def _fox_kernel(qt_ref, ft_ref, k_ref, ka_ref, vt_ref, o_ref,
                kx_scr, vt_scr, qxt_scr, sa_scr, sb_scr, mba_scr, mbb_scr, pa_scr, pb_scr,
                m_scr, al_scr, acc_scr, ex_scr, *, tq, tk):
    h = pl.program_id(1)
    nkb, nks = vt_scr.shape[0], vt_scr.shape[1]

    kx_scr[:, :FOX_DH] = k_ref[0]
    kx_scr[:, FOX_DH:] = ka_ref[0]
    qxt_scr[:, FOX_DH:, :] = jnp.zeros(
        (qxt_scr.shape[0], qxt_scr.shape[1] - FOX_DH, FOX_CHUNK), BF16)
    for n in range(nkb):
        for kk in range(nks):
            k0 = n * tk + kk * FOX_KSUB
            vt_scr[n, kk, :FOX_DH, :] = vt_ref[0, :, k0:k0 + FOX_KSUB]
            vt_scr[n, kk, FOX_DH:, :] = jnp.ones((vt_scr.shape[2] - FOX_DH, FOX_KSUB), BF16)

    def tile(i, carry):
        _fox_tile(i, h, qt_ref, ft_ref, o_ref, kx_scr, vt_scr, qxt_scr, sa_scr, sb_scr,
                  mba_scr, mbb_scr, pa_scr, pb_scr, m_scr, al_scr, acc_scr, ex_scr, tq=tq, tk=tk)
        return carry

    lax.fori_loop(0, qt_ref.shape[1], tile, 0)


def _fox_tile(i, h, qt_ref, ft_ref, o_ref, kx_scr, vt_scr, qxt_scr, sa_scr, sb_scr,
              mba_scr, mbb_scr, pa_scr, pb_scr, m_scr, al_scr, acc_scr, ex_scr, *, tq, tk):
    nks = vt_scr.shape[1]
    f_row = ft_ref[0, i, pl.ds(pl.multiple_of(AUG_STRIDE * h, AUG_STRIDE), 8), :][0:1, :]
    f0, f1, f2 = [p.astype(F32) for p in _split3(f_row)]
    sub = lax.broadcasted_iota(jnp.int32, (AUG_STRIDE, tq), 0)
    aug = jnp.where(sub == 0, f0, jnp.where(sub == 1, f1, jnp.where(
        sub == 2, f2, jnp.where(sub < 6, 1.0, 0.0))))
    aug = aug.astype(BF16)
    aug_rows = pl.ds(pl.multiple_of(FOX_DH + AUG_STRIDE * h, AUG_STRIDE), AUG_STRIDE)

    chunks = [slice(c0, c0 + FOX_CHUNK) for c0 in range(0, tq, FOX_CHUNK)]
    ci = lambda cs: cs.start // FOX_CHUNK
    for cs in chunks:
        qxt_scr[ci(cs), :FOX_DH, :] = qt_ref[0, i, :, cs]
        qxt_scr[ci(cs), aug_rows, :] = aug[:, cs]

    def qk(blk, cs, nkeys=tk):
        off = pl.multiple_of(blk * tk, tk)
        return jnp.dot(kx_scr[pl.ds(off, nkeys), :], qxt_scr[ci(cs)],
                       preferred_element_type=F32)

    def causal(s, cs, key0):
        key = lax.broadcasted_iota(jnp.int32, s.shape, 0) + key0
        qry = lax.broadcasted_iota(jnp.int32, s.shape, 1) + cs.start
        return jnp.where(key <= qry, s, NEG_BIG)

    def emit():
        for cs in chunks:
            acc = acc_scr[ci(cs)]
            o_ref[0, i, :, cs] = (acc[:FOX_DH] / acc[FOX_DH:FOX_DH + 1]).astype(BF16)

    def pv_dot(blk, p, cs):
        return sum(jnp.dot(vt_scr[blk, kk], p[ci(cs), kk * FOX_KSUB:(kk + 1) * FOX_KSUB, :],
                           preferred_element_type=F32) for kk in range(nks))

    def pv_acc(blk_prev, p_prv, cs):
        acc_scr[ci(cs)] = al_scr[ci(cs)] * (acc_scr[ci(cs)] + pv_dot(blk_prev, p_prv, cs))

    def lazy_chunk(blk, cs, p_cur, key0):
        s = qk(blk, cs)
        if key0 is not None:
            s = causal(s, cs, key0)
        mb = jnp.max(s, axis=0, keepdims=True)
        r_old = m_scr[ci(cs)]
        p_cur[ci(cs)] = jnp.exp2((s - r_old).astype(BF16))
        r_new = jnp.maximum(r_old, mb)
        m_scr[ci(cs)] = r_new
        ex_scr[ci(cs)] = jnp.maximum(ex_scr[ci(cs)], mb - r_old)
        return jnp.exp2(r_old - r_new)

    def lazy_step(blk, blk_prev, p_cur, p_prv):
        for cs in chunks:
            alpha = lazy_chunk(blk, cs, p_cur, None)
            pv_acc(blk_prev, p_prv, cs)
            al_scr[ci(cs)] = alpha

    def own_block(cs):
        return (2 * i, 0) if cs.start < tk else (2 * i + 1, tk)

    for cs in chunks:
        blk, key0 = own_block(cs)
        nkeys = min(tk, cs.start + FOX_CHUNK - key0)
        s = causal(qk(blk, cs, nkeys), cs, key0)
        mb = jnp.max(s, axis=0, keepdims=True)
        m_scr[ci(cs)] = mb
        pa_scr[ci(cs), :nkeys, :] = jnp.exp2((s - mb).astype(BF16))
        if nkeys < tk:
            pa_scr[ci(cs), nkeys:, :] = jnp.zeros((tk - nkeys, FOX_CHUNK), BF16)
        al_scr[ci(cs)] = jnp.ones((1, FOX_CHUNK), F32)
        acc_scr[ci(cs)] = jnp.zeros((acc_scr.shape[1], FOX_CHUNK), F32)
        ex_scr[ci(cs)] = jnp.zeros((1, FOX_CHUNK), F32)
    for cs in chunks:
        blk, _ = own_block(cs)
        if cs.start < tk:
            pb_scr[ci(cs)] = jnp.zeros((tk, FOX_CHUNK), BF16)
            pv_acc(blk, pa_scr, cs)
        else:
            alpha = lazy_chunk(2 * i, cs, pb_scr, None)
            pv_acc(blk, pa_scr, cs)
            al_scr[ci(cs)] = alpha

    def lazy_pair(jj):
        lazy_step(2 * jj + 1, 2 * jj + 2, pa_scr, pb_scr)
        lazy_step(2 * jj, 2 * jj + 1, pb_scr, pa_scr)

    def lazy_body(t, carry):
        lazy_pair(i - 1 - 2 * t)
        lazy_pair(i - 2 - 2 * t)
        return carry

    lax.fori_loop(0, i // 2, lazy_body, 0)

    @pl.when(i % 2 == 1)
    def _():
        lazy_pair(0)

    for cs in chunks:
        pv_acc(0, pb_scr, cs)
    emit()

    def scores(blk, s_out, mb_out, cs):
        s = qk(blk, cs)
        s_out[ci(cs)] = s
        if mb_out is not None:
            mb_out[ci(cs)] = jnp.max(s, axis=0, keepdims=True)

    def softmax(s_in, mb_in, p_out, key0, cs):
        s = s_in[ci(cs)]
        if key0 is None:
            mb = mb_in[ci(cs)]
        else:
            s = causal(s, cs, key0)
            mb = jnp.max(s, axis=0, keepdims=True)
        m_prev = m_scr[ci(cs)]
        m_new = jnp.maximum(m_prev, mb)
        p_out[ci(cs)] = jnp.exp2((s - m_new).astype(BF16))
        m_scr[ci(cs)] = m_new
        return jnp.exp2(m_prev - m_new)

    def pv(blk_prev, p_in, cs):
        acc_scr[ci(cs)] = al_scr[ci(cs)] * acc_scr[ci(cs)] + pv_dot(blk_prev, p_in, cs)

    def step(n, cur, nxt, p_prv, key0):
        s_cur, mb_cur, p_cur = cur
        for cs in chunks:
            if nxt is not None:
                scores(n + 1, nxt[0], nxt[1], cs)
            alpha = softmax(s_cur, mb_cur, p_cur, key0, cs)
            pv(jnp.maximum(n - 1, 0), p_prv, cs)
            al_scr[ci(cs)] = alpha

    buf_a = (sa_scr, mba_scr, pa_scr)
    buf_b = (sb_scr, mbb_scr, pb_scr)

    def body(jj, carry):
        step(2 * jj, buf_a, buf_b, pb_scr, None)
        step(2 * jj + 1, buf_b, buf_a, pa_scr, None)
        return carry

    @pl.when(jnp.max(ex_scr[...]) > FOX_MAX_EXCESS)
    def _():
        m_scr[...] = jnp.full(m_scr.shape, NEG_BIG, F32)
        al_scr[...] = jnp.ones_like(al_scr)
        acc_scr[...] = jnp.zeros_like(acc_scr)
        pb_scr[...] = jnp.zeros_like(pb_scr)
        for cs in chunks:
            scores(0, sa_scr, mba_scr, cs)
        lax.fori_loop(0, i, body, 0)
        step(2 * i, buf_a, (sb_scr, None), pb_scr, 0)
        step(2 * i + 1, buf_b, None, pa_scr, tk)
        for cs in chunks:
            pv(2 * i + 1, pb_scr, cs)
        emit()


def _fox(z3, qt, ft, ka, vt):
    B, S, _ = z3.shape
    tq, tk = FOX_TQ, FOX_TQ // 2
    nq = S // tq
    ones_rows = 16
    nch = tq // FOX_CHUNK
    W = FOX_HEADS * FOX_DH
    kb = Z_COLS["fk"] // FOX_DH
    kern = functools.partial(_fox_kernel, tq=tq, tk=tk)
    return pl.pallas_call(
        kern,
        out_shape=jax.ShapeDtypeStruct((B, nq, W, tq), BF16),
        grid=(B, FOX_HEADS),
        in_specs=[pl.BlockSpec((1, nq, FOX_DH, tq), lambda b, h: (b, 0, h, 0)),
                  pl.BlockSpec((1, nq, LANES, tq), lambda b, h: (b, 0, 0, 0)),
                  pl.BlockSpec((1, S, FOX_DH), lambda b, h: (b, 0, kb + h)),
                  pl.BlockSpec((1, S, LANES), lambda b, h: (b, 0, 0)),
                  pl.BlockSpec((1, FOX_DH, S), lambda b, h: (b, h, 0))],
        out_specs=pl.BlockSpec((1, nq, FOX_DH, tq), lambda b, h: (b, 0, h, 0)),
        scratch_shapes=[pltpu.VMEM((S, 2 * FOX_DH), BF16),
                        pltpu.VMEM((S // tk, tk // FOX_KSUB, FOX_DH + ones_rows, FOX_KSUB), BF16),
                        pltpu.VMEM((nch, 2 * FOX_DH, FOX_CHUNK), BF16),
                        pltpu.VMEM((nch, tk, FOX_CHUNK), F32),
                        pltpu.VMEM((nch, tk, FOX_CHUNK), F32),
                        pltpu.VMEM((nch, 1, FOX_CHUNK), F32),
                        pltpu.VMEM((nch, 1, FOX_CHUNK), F32),
                        pltpu.VMEM((nch, tk, FOX_CHUNK), BF16),
                        pltpu.VMEM((nch, tk, FOX_CHUNK), BF16),
                        pltpu.VMEM((nch, 1, FOX_CHUNK), F32),
                        pltpu.VMEM((nch, 1, FOX_CHUNK), F32),
                        pltpu.VMEM((nch, FOX_DH + ones_rows, FOX_CHUNK), F32),
                        pltpu.VMEM((nch, 1, FOX_CHUNK), F32)],
        compiler_params=_cparams(("arbitrary", "arbitrary"), 56),
        name="fox",
    )(qt, ft, z3, ka, vt)


FFN_CHUNK = 256


def _tail_kernel(a_ref, yft_ref, gf_ref, x_ref, g1_ref, sc_ref, sh_ref, g2_ref, n2_ref, nf_ref,
                 wf_ref, wo_ref, wg_ref, wu_ref, wd_ref, o_ref, act_scr):
    tm = x_ref.shape[0]
    halves = [slice(r0, r0 + tm // 2) for r0 in range(0, tm, tm // 2)]
    x1s, hs = [], []
    for rs in halves:
        fo = lax.dot_general(yft_ref[0, 0, :, rs], wf_ref[...], _TN, preferred_element_type=F32)
        merged = a_ref[rs, :].astype(F32) + gf_ref[rs, :].astype(F32) * fo
        out = jnp.dot(merged.astype(BF16), wo_ref[...], preferred_element_type=F32)
        x1s.append(x_ref[rs, :] + g1_ref[0] * out)
    for x1 in x1s:
        ms = jnp.mean(x1 * x1, axis=-1, keepdims=True)
        y = x1 * lax.rsqrt(ms + EPS) * n2_ref[...]
        hs.append((y * (1.0 + sc_ref[0]) + sh_ref[0]).astype(BF16))
    for c0 in range(0, wg_ref.shape[1], FFN_CHUNK):
        for rs, h in zip(halves, hs):
            g = jnp.dot(h, wg_ref[:, c0:c0 + FFN_CHUNK], preferred_element_type=F32)
            u = jnp.dot(h, wu_ref[:, c0:c0 + FFN_CHUNK], preferred_element_type=F32)
            act_scr[rs, c0:c0 + FFN_CHUNK] = (g * jax.nn.sigmoid(g) * u).astype(BF16)
    for rs, x1 in zip(halves, x1s):
        ffn = jnp.dot(act_scr[rs, :], wd_ref[...], preferred_element_type=F32)
        x2 = x1 + g2_ref[0] * ffn
        ms2 = jnp.mean(x2 * x2, axis=-1, keepdims=True)
        o_ref[rs, :] = x2 * lax.rsqrt(ms2 + EPS) * nf_ref[...]


def _tail(a2, yft, z2, x2, g1, sc2, sh2, g2, n2, nf, wf, wo, wg, wu, wd, S):
    T, D = x2.shape
    FF = wg.shape[1]
    assert FF % FFN_CHUNK == 0
    tm = 512
    spb = S // tm
    tpq = FOX_TQ // tm
    row = lambda i: (i, 0)
    bidx = lambda i: (i // spb, 0, 0)
    const = lambda shape: pl.BlockSpec(shape, lambda i: (0,) * len(shape),
                                       pipeline_mode=pl.Buffered(1))
    return pl.pallas_call(
        _tail_kernel,
        out_shape=jax.ShapeDtypeStruct((T, D), F32),
        grid=(T // tm,),
        in_specs=[pl.BlockSpec((tm, D), row),
                  pl.BlockSpec((1, 1, D, tm),
                               lambda i: (i // spb, (i % spb) // tpq, 0, i % tpq)),
                  pl.BlockSpec((tm, D), lambda i: (i, Z_COLS["gf"] // D)),
                  pl.BlockSpec((tm, D), row),
                  pl.BlockSpec((1, 1, D), bidx),
                  pl.BlockSpec((1, 1, D), bidx),
                  pl.BlockSpec((1, 1, D), bidx),
                  pl.BlockSpec((1, 1, D), bidx),
                  const((1, D)),
                  const((1, D)),
                  const((D, D)),
                  const((D, D)),
                  const((D, FF)),
                  const((D, FF)),
                  const((FF, D))],
        out_specs=pl.BlockSpec((tm, D), row),
        scratch_shapes=[pltpu.VMEM((tm, FF), BF16)],
        compiler_params=_cparams(("arbitrary",), 58),
        name="tail",
    )(a2, yft, z2, x2, g1, sc2, sh2, g2, n2, nf, wf, wo, wg, wu, wd)


def _castw_kernel(wt_ref, o_ref):
    o_ref[...] = wt_ref[...].T.astype(BF16)


def _cast_inproj_weights(w_t, n_lead, gap):
    NT, D = w_t.shape
    tn = 1024
    n_out = NT - gap
    sub = 8
    assert n_lead % tn == 0 and n_out % tn == 0 and gap % sub == 0
    row0 = lambda j: sub * jnp.where(j < n_lead // tn, j * (tn // sub), j * (tn // sub) + gap // sub)
    return pl.pallas_call(
        _castw_kernel,
        out_shape=jax.ShapeDtypeStruct((D, n_out), BF16),
        grid=(n_out // tn,),
        in_specs=[pl.BlockSpec((pl.Element(tn), pl.Element(D)), lambda j: (row0(j), 0))],
        out_specs=pl.BlockSpec((D, tn), lambda j: (0, j)),
        compiler_params=_cparams(("arbitrary",), 32),
        name="castw",
    )(w_t)


def _rope_tables(S):
    half = RET_DK // 2
    freqs = ROPE_THETA ** (-np.arange(half, dtype=np.float64) / half)
    ang = np.arange(S, dtype=np.float64)[:, None] * freqs[None, :]
    cos, sin = np.cos(ang), np.sin(ang)
    return (jnp.asarray(np.concatenate([cos, cos], axis=1), F32),
            jnp.asarray(np.concatenate([-sin, sin], axis=1), F32))


def _aug_lanes(v8):
    rep = jnp.repeat(v8[..., None], 6, axis=-1)
    pad = jnp.zeros(v8.shape + (AUG_STRIDE - 6,), v8.dtype)
    return jnp.concatenate([rep, pad], axis=-1).reshape(v8.shape[:-1] + (LANES,))


def _layer(x, mod, norm1_w, w_in, b_f, ret_proj, fox_proj, w_out, norm2_w,
           w_gate, w_up, w_down, norm_out_w, tables):
    B, S, D = x.shape
    T = B * S
    sh1, sc1, g1, sh2, sc2, g2 = [m.reshape(B, 1, D) for m in jnp.split(mod, 6, axis=-1)]
    offs = np.cumsum([0, RET_HEADS * RET_DK, RET_HEADS * RET_DK, RET_HEADS * RET_DV,
                      RET_HEADS * RET_DV, FOX_HEADS * FOX_DH, FOX_HEADS * FOX_DH,
                      FOX_HEADS * FOX_DH, FOX_HEADS, D, D])
    o_ff, o_gr = int(offs[7]), int(offs[8])
    w_main = _cast_inproj_weights(jnp.swapaxes(w_in, 0, 1), o_ff, o_gr - o_ff)
    w_ff = _aug_lanes(w_in[:, o_ff:o_gr]).astype(BF16)
    bf_l = _aug_lanes(b_f).reshape(1, LANES)

    x2 = x.reshape(T, D)
    z, qt, vt, ft, ka = _inproj(x2, norm1_w.reshape(1, D), sc1, sh1, tables[0], tables[1],
                            w_main, w_ff, bf_l, B, S)
    z3 = z.reshape(B, S, z.shape[1])
    a = _retention(z3, ret_proj.astype(BF16), D)
    yf = _fox(z3, qt, ft, ka, vt)
    out = _tail(a.reshape(T, D), yf, z, x2, g1, sc2, sh2, g2,
                norm2_w.reshape(1, D), norm_out_w.reshape(1, D),
                fox_proj.astype(BF16), w_out.astype(BF16), w_gate.astype(BF16),
                w_up.astype(BF16), w_down.astype(BF16), S)
    return out.reshape(B, S, D)


def kernel(x, c, ada_w, ada_b, norm1_w, w_in, b_f, ret_proj, fox_proj, w_out,
           norm2_w, w_gate, w_up, w_down, norm_f_w):
    depth = ada_w.shape[0]
    assert depth == 1, "the final RMSNorm is fused into the last layer's channel mixer"
    tables = _rope_tables(x.shape[1])
    l = 0
    mod = _ada(c, ada_w[l], ada_b[l])
    return _layer(x, mod, norm1_w[l], w_in[l], b_f[l], ret_proj[l], fox_proj[l], w_out[l],
                  norm2_w[l], w_gate[l], w_up[l], w_down[l], norm_f_w, tables)
```

```python
import functools
import math

import numpy as np
import jax
import jax.numpy as jnp
from jax import lax
from jax.experimental import pallas as pl
from jax.experimental.pallas import tpu as pltpu

F32 = jnp.float32
BF16 = jnp.bfloat16

EPS = 1e-6
ROPE_THETA = 10000.0
RET_HEADS = 4
RET_DK = 128
RET_DV = 256
FOX_HEADS = 8
FOX_DH = 128
RET_CHUNK = 64
LANES = 128
AUG_STRIDE = 16

NEG_BIG = -1e30
LOG2E = math.log2(math.e)

_NT = (((1,), (1,)), ((), ()))
_TN = (((0,), (0,)), ((), ()))


def _cparams(sem, vmem_mb, flags=None):
    return pltpu.CompilerParams(dimension_semantics=sem,
                                vmem_limit_bytes=vmem_mb * 1024 * 1024, flags=flags)


def _ada_kernel(ct_ref, w_ref, b_ref, o_ref):
    ct = ct_ref[...]
    act = ct * jax.nn.sigmoid(ct)
    w = w_ref[...]
    for b in range(o_ref.shape[0]):
        o_ref[b:b + 1, :] = (jnp.sum(act[:, b:b + 1] * w, axis=0, keepdims=True)
                             + b_ref[...])


def _ada(c, w, b):
    B, D = c.shape
    N = w.shape[1]
    tn = 1536
    return pl.pallas_call(
        _ada_kernel,
        out_shape=jax.ShapeDtypeStruct((B, N), F32),
        grid=(N // tn,),
        in_specs=[pl.BlockSpec((D, B), lambda j: (0, 0)),
                  pl.BlockSpec((D, tn), lambda j: (0, j)),
                  pl.BlockSpec((1, tn), lambda j: (0, j))],
        out_specs=pl.BlockSpec((B, tn), lambda j: (0, j)),
        compiler_params=_cparams(("arbitrary",), 40),
        name="ada",
    )(c.T, w, b.reshape(1, N))


INPROJ_CHUNK = 512
FOX_TQ = 2048
FOX_TK = 512
Z_COLS = {"rq": 0, "rk": 512, "rv": 1024, "rg": 2048, "fk": 3072, "gr": 4096, "gf": 5120,
          "end": 6144}
W_COLS = {"rq": 0, "rk": 512, "rv": 1024, "rg": 2048, "fq": 3072, "fk": 4096, "fv": 5120,
          "gr": 6144, "gf": 7168, "end": 8192}


def _split3(v):
    p0 = v.astype(BF16)
    r1 = v - p0.astype(F32)
    p1 = r1.astype(BF16)
    r2 = r1 - p1.astype(F32)
    p2 = r2.astype(BF16)
    return p0, p1, p2


def _inproj_kernel(x0_ref, xn_ref, nw_ref, sc0_ref, sh0_ref, scn_ref, shn_ref, cos_ref, sin_ref,
                   w_ref, wff_ref, bf_ref, tri_ref,
                   z_ref, qt_ref, vt_ref, ft_ref, ka_ref, h_scr, carry,
                   *, fox_scale, rk_scale, spb):
    i = pl.program_id(0)

    def normed(x_ref, sc_ref, sh_ref):
        x = x_ref[...]
        ms = jnp.mean(x * x, axis=-1, keepdims=True)
        y = x * lax.rsqrt(ms + EPS) * nw_ref[...]
        return (y * (1.0 + sc_ref[0]) + sh_ref[0]).astype(BF16)

    @pl.when(i == 0)
    def _():
        h_scr[0] = normed(x0_ref, sc0_ref, sh0_ref)

    h = h_scr[i % 2]
    cos = cos_ref[...]
    sin = sin_ref[...]
    cw = INPROJ_CHUNK

    def chunk(w_ref, wc0):
        return jnp.dot(h, w_ref[:, wc0:wc0 + cw], preferred_element_type=F32)

    def rotary(name, scale):
        acc = chunk(w_ref, W_COLS[name])
        c0 = Z_COLS[name]
        for hh in range(cw // RET_DK):
            a = acc[:, hh * RET_DK:(hh + 1) * RET_DK]
            o = a * cos + pltpu.roll(a, RET_DK // 2, axis=1) * sin
            if scale is not None:
                o = o * scale
            z_ref[:, c0 + hh * RET_DK:c0 + (hh + 1) * RET_DK] = o.astype(BF16)

    def group(w_ref, wc0, name, width, fn):
        for d in range(0, width, cw):
            c0 = Z_COLS[name] + d
            z_ref[:, c0:c0 + cw] = fn(chunk(w_ref, wc0 + d)).astype(BF16)

    h_next = normed(xn_ref, scn_ref, shn_ref)

    xv = jnp.dot(h, wff_ref[...], preferred_element_type=F32) + bf_ref[...]
    lf = jnp.minimum(xv, 0.0) - jnp.log(1.0 + jnp.exp(-jnp.abs(xv)))
    p0, p1, p2 = _split3(lf)

    rotary("rq", None)
    rotary("rk", rk_scale)

    tri = tri_ref[...]
    cs = (jnp.dot(tri, p0, preferred_element_type=F32)
          + jnp.dot(tri, p1, preferred_element_type=F32)
          + jnp.dot(tri, p2, preferred_element_type=F32))
    first = (i % spb) == 0
    fc = cs + jnp.where(first, 0.0, carry[...])
    tm = fc.shape[0]
    carry[...] = fc[tm - 1:tm, :]

    group(w_ref, W_COLS["rv"], "rv", RET_HEADS * RET_DV, lambda a: a)

    f2l = fc * LOG2E
    ft_ref[0, 0] = f2l.T
    f0, f1, f2 = [p.astype(F32) for p in _split3(f2l)]
    c = lax.broadcasted_iota(jnp.int32, fc.shape, 1) % AUG_STRIDE
    ka_ref[0] = jnp.where(c < 3, 1.0, jnp.where(c == 3, -f0, jnp.where(
        c == 4, -f1, jnp.where(c == 5, -f2, 0.0)))).astype(BF16)

    group(w_ref, W_COLS["fk"], "fk", FOX_HEADS * FOX_DH, lambda a: a)
    group(w_ref, W_COLS["rg"], "rg", RET_HEADS * RET_DV, lambda a: a * jax.nn.sigmoid(a))
    for d in range(0, FOX_HEADS * FOX_DH, cw):
        vt_ref[0, d:d + cw, :] = chunk(w_ref, W_COLS["fv"] + d).T.astype(BF16)
    group(w_ref, W_COLS["gr"], "gr", W_COLS["end"] - W_COLS["gr"], jax.nn.sigmoid)
    for d in range(0, FOX_HEADS * FOX_DH, cw):
        qt_ref[0, 0, d:d + cw, :] = (chunk(w_ref, W_COLS["fq"] + d) * fox_scale).T.astype(BF16)
    h_scr[(i + 1) % 2] = h_next


def _inproj(x2, nw, sc, sh, cos_t, sin_t, w, wff, bf_l, B, S):
    T, D = x2.shape
    N = Z_COLS["end"]
    NQ = NV = FOX_HEADS * FOX_DH
    assert w.shape[1] == W_COLS["end"] == N + NQ + NV
    tm = 512
    spb = S // tm
    tpq = FOX_TQ // tm
    assert spb % tpq == 0
    tri = jnp.asarray(np.tril(np.ones((tm, tm), np.float32)), BF16)
    kern = functools.partial(_inproj_kernel, fox_scale=LOG2E / math.sqrt(FOX_DH),
                             rk_scale=RET_DK ** -0.5, spb=spb)
    const = lambda shape: pl.BlockSpec(shape, lambda i: (0,) * len(shape),
                                       pipeline_mode=pl.Buffered(1))
    nxt = lambda i: jnp.minimum(i + 1, T // tm - 1)
    return pl.pallas_call(
        kern,
        out_shape=(jax.ShapeDtypeStruct((T, N), BF16),
                   jax.ShapeDtypeStruct((B, S // FOX_TQ, NQ, FOX_TQ), BF16),
                   jax.ShapeDtypeStruct((B, NV, S), BF16),
                   jax.ShapeDtypeStruct((B, S // FOX_TQ, LANES, FOX_TQ), F32),
                   jax.ShapeDtypeStruct((B, S, LANES), BF16)),
        grid=(T // tm,),
        in_specs=[const((tm, D)),
                  pl.BlockSpec((tm, D), lambda i: (nxt(i), 0)),
                  const((1, D)),
                  const((1, 1, D)),
                  const((1, 1, D)),
                  pl.BlockSpec((1, 1, D), lambda i: (nxt(i) // spb, 0, 0)),
                  pl.BlockSpec((1, 1, D), lambda i: (nxt(i) // spb, 0, 0)),
                  pl.BlockSpec((tm, LANES), lambda i: (i % spb, 0)),
                  pl.BlockSpec((tm, LANES), lambda i: (i % spb, 0)),
                  const((D, W_COLS["end"])),
                  const((D, LANES)),
                  const((1, LANES)),
                  const((tm, tm))],
        out_specs=(pl.BlockSpec((tm, N), lambda i: (i, 0)),
                   pl.BlockSpec((1, 1, NQ, tm), lambda i: (i // spb, (i % spb) // tpq, 0, i % tpq)),
                   pl.BlockSpec((1, NV, tm), lambda i: (i // spb, 0, i % spb)),
                   pl.BlockSpec((1, 1, LANES, tm),
                                lambda i: (i // spb, (i % spb) // tpq, 0, i % tpq)),
                   pl.BlockSpec((1, tm, LANES), lambda i: (i // spb, i % spb, 0))),
        scratch_shapes=[pltpu.VMEM((2, tm, D), BF16), pltpu.VMEM((1, LANES), F32)],
        compiler_params=_cparams(("arbitrary",), 56),
        name="inproj",
    )(x2, x2, nw, sc, sh, sc, sh, cos_t, sin_t, w, wff, bf_l, tri)


def _ret_consts(L):
    hs = np.arange(RET_HEADS, dtype=np.float64)
    log_gamma = np.log(1.0 - np.exp2(-5.0 - hs))
    idx = np.arange(L, dtype=np.float64)
    dist = np.abs(idx[:, None] - idx[None, :])
    chunk_ok = (idx[None, :] // RET_CHUNK) <= (idx[:, None] // RET_CHUNK)
    dmask = np.exp(log_gamma[:, None, None] * dist[None]) * chunk_ok[None]
    qdec = np.exp(log_gamma[:, None] * idx[None, :])[..., None]
    kdec = np.exp(log_gamma[:, None] * (L - idx)[None, :])[..., None]
    bdec = [float(v) for v in np.exp(log_gamma * L)]
    return (jnp.asarray(dmask, F32), jnp.asarray(qdec, F32), jnp.asarray(kdec, F32), bdec)


def _ret_kernel(rq_ref, rk_ref, rv_ref, rg_ref, gr_ref, dm_ref, qd_ref, kd_ref, wr_ref,
                o_ref, st_scr, y_scr, *, bdec):
    @pl.when(pl.program_id(1) == 0)
    def _():
        st_scr[...] = jnp.zeros_like(st_scr)

    for h in range(RET_HEADS):
        q = rq_ref[0, :, h * RET_DK:(h + 1) * RET_DK]
        k = rk_ref[0, :, h * RET_DK:(h + 1) * RET_DK]
        v = rv_ref[0, :, h * RET_DV:(h + 1) * RET_DV]
        s = lax.dot_general(q, k, _NT, preferred_element_type=F32) * dm_ref[h]
        intra = jnp.dot(s.astype(BF16), v, preferred_element_type=F32)
        st = st_scr[h]
        inter = jnp.dot(q, st.astype(BF16), preferred_element_type=F32) * qd_ref[h]
        ks = (k.astype(F32) * kd_ref[h]).astype(BF16)
        st_scr[h] = bdec[h] * st + lax.dot_general(ks, v, _TN, preferred_element_type=F32)
        ro = intra + inter
        ron = ro * lax.rsqrt(jnp.mean(ro * ro, axis=-1, keepdims=True) + EPS)
        g = rg_ref[0, :, h * RET_DV:(h + 1) * RET_DV].astype(F32)
        y_scr[:, h * RET_DV:(h + 1) * RET_DV] = (g * ron).astype(BF16)

    proj = jnp.dot(y_scr[...], wr_ref[...], preferred_element_type=F32)
    o_ref[0] = (gr_ref[0].astype(F32) * proj).astype(BF16)


def _retention(z3, wr, D):
    B, S, _ = z3.shape
    L = 512
    dmask, qdec, kdec, bdec = _ret_consts(L)
    QK = RET_HEADS * RET_DK
    V = RET_HEADS * RET_DV
    kern = functools.partial(_ret_kernel, bdec=bdec)
    full3 = lambda b, i: (0, 0, 0)
    return pl.pallas_call(
        kern,
        out_shape=jax.ShapeDtypeStruct((B, S, D), BF16),
        grid=(B, S // L),
        in_specs=[pl.BlockSpec((1, L, QK), lambda b, i: (b, i, 0)),
                  pl.BlockSpec((1, L, QK), lambda b, i: (b, i, 1)),
                  pl.BlockSpec((1, L, V), lambda b, i: (b, i, 1)),
                  pl.BlockSpec((1, L, V), lambda b, i: (b, i, 2)),
                  pl.BlockSpec((1, L, D), lambda b, i: (b, i, Z_COLS["gr"] // D)),
                  pl.BlockSpec((RET_HEADS, L, L), full3),
                  pl.BlockSpec((RET_HEADS, L, 1), full3),
                  pl.BlockSpec((RET_HEADS, L, 1), full3),
                  pl.BlockSpec((V, D), lambda b, i: (0, 0))],
        out_specs=pl.BlockSpec((1, L, D), lambda b, i: (b, i, 0)),
        scratch_shapes=[pltpu.VMEM((RET_HEADS, RET_DK, RET_DV), F32),
                        pltpu.VMEM((L, V), BF16)],
        compiler_params=_cparams(("arbitrary", "arbitrary"), 40),
        name="ret",
    )(z3, z3, z3, z3, z3, dmask, qdec, kdec, wr)


FOX_CHUNK = 256
FOX_KSUB = 256
FOX_MAX_EXCESS = 64.0


def _fox_kernel(qt_ref, ft_ref, k_ref, ka_ref, vt_ref, o_ref,
                kx_scr, vt_scr, qxt_scr, sa_scr, sb_scr, mba_scr, mbb_scr, pa_scr, pb_scr,
                m_scr, al_scr, acc_scr, ex_scr, *, tq, tk):
    h = pl.program_id(1)
    nkb, nks = vt_scr.shape[0], vt_scr.shape[1]

    kx_scr[:, :FOX_DH] = k_ref[0]
    kx_scr[:, FOX_DH:] = ka_ref[0]
    qxt_scr[:, FOX_DH:, :] = jnp.zeros(
        (qxt_scr.shape[0], qxt_scr.shape[1] - FOX_DH, FOX_CHUNK), BF16)
    for n in range(nkb):
        for kk in range(nks):
            k0 = n * tk + kk * FOX_KSUB
            vt_scr[n, kk, :FOX_DH, :] = vt_ref[0, :, k0:k0 + FOX_KSUB]
            vt_scr[n, kk, FOX_DH:, :] = jnp.ones((vt_scr.shape[2] - FOX_DH, FOX_KSUB), BF16)

    def tile(i, carry):
        _fox_tile(i, h, qt_ref, ft_ref, o_ref, kx_scr, vt_scr, qxt_scr, sa_scr, sb_scr,
                  mba_scr, mbb_scr, pa_scr, pb_scr, m_scr, al_scr, acc_scr, ex_scr, tq=tq, tk=tk)
        return carry

    lax.fori_loop(0, qt_ref.shape[1], tile, 0)


def _fox_tile(i, h, qt_ref, ft_ref, o_ref, kx_scr, vt_scr, qxt_scr, sa_scr, sb_scr,
              mba_scr, mbb_scr, pa_scr, pb_scr, m_scr, al_scr, acc_scr, ex_scr, *, tq, tk):
    nks = vt_scr.shape[1]
    f_row = ft_ref[0, i, pl.ds(pl.multiple_of(AUG_STRIDE * h, AUG_STRIDE), 8), :][0:1, :]
    f0, f1, f2 = [p.astype(F32) for p in _split3(f_row)]
    sub = lax.broadcasted_iota(jnp.int32, (AUG_STRIDE, tq), 0)
    aug = jnp.where(sub == 0, f0, jnp.where(sub == 1, f1, jnp.where(
        sub == 2, f2, jnp.where(sub < 6, 1.0, 0.0))))
    aug = aug.astype(BF16)
    aug_rows = pl.ds(pl.multiple_of(FOX_DH + AUG_STRIDE * h, AUG_STRIDE), AUG_STRIDE)

    chunks = [slice(c0, c0 + FOX_CHUNK) for c0 in range(0, tq, FOX_CHUNK)]
    ci = lambda cs: cs.start // FOX_CHUNK
    for cs in chunks:
        qxt_scr[ci(cs), :FOX_DH, :] = qt_ref[0, i, :, cs]
        qxt_scr[ci(cs), aug_rows, :] = aug[:, cs]

    def qk(blk, cs, nkeys=tk):
        off = pl.multiple_of(blk * tk, tk)
        return jnp.dot(kx_scr[pl.ds(off, nkeys), :], qxt_scr[ci(cs)],
                       preferred_element_type=F32)

    def causal(s, cs, key0):
        key = lax.broadcasted_iota(jnp.int32, s.shape, 0) + key0
        qry = lax.broadcasted_iota(jnp.int32, s.shape, 1) + cs.start
        return jnp.where(key <= qry, s, NEG_BIG)

    def emit():
        for cs in chunks:
            acc = acc_scr[ci(cs)]
            o_ref[0, i, :, cs] = (acc[:FOX_DH] / acc[FOX_DH:FOX_DH + 1]).astype(BF16)

    def pv_dot(blk, p, cs):
        return sum(jnp.dot(vt_scr[blk, kk], p[ci(cs), kk * FOX_KSUB:(kk + 1) * FOX_KSUB, :],
                           preferred_element_type=F32) for kk in range(nks))

    def pv_acc(blk_prev, p_prv, cs):
        acc_scr[ci(cs)] = al_scr[ci(cs)] * (acc_scr[ci(cs)] + pv_dot(blk_prev, p_prv, cs))

    def lazy_chunk(blk, cs, p_cur, key0):
        s = qk(blk, cs)
        if key0 is not None:
            s = causal(s, cs, key0)
        mb = jnp.max(s, axis=0, keepdims=True)
        r_old = m_scr[ci(cs)]
        p_cur[ci(cs)] = jnp.exp2((s - r_old).astype(BF16))
        r_new = jnp.maximum(r_old, mb)
        m_scr[ci(cs)] = r_new
        ex_scr[ci(cs)] = jnp.maximum(ex_scr[ci(cs)], mb - r_old)
        return jnp.exp2(r_old - r_new)

    def lazy_step(blk, blk_prev, p_cur, p_prv):
        for cs in chunks:
            alpha = lazy_chunk(blk, cs, p_cur, None)
            pv_acc(blk_prev, p_prv, cs)
            al_scr[ci(cs)] = alpha

    nb = tq // tk
    first = nb * i
    ones = jnp.ones((1, FOX_CHUNK), F32)

    def own_block(cs):
        return cs.start // tk, (cs.start // tk) * tk

    for cs in chunks:
        bl, key0 = own_block(cs)
        nkeys = min(tk, cs.start + FOX_CHUNK - key0)
        s = causal(qk(first + bl, cs, nkeys), cs, key0)
        mb = jnp.max(s, axis=0, keepdims=True)
        m_scr[ci(cs)] = mb
        pa_scr[ci(cs), :nkeys, :] = jnp.exp2((s - mb).astype(BF16))
        if nkeys < tk:
            pa_scr[ci(cs), nkeys:, :] = jnp.zeros((tk - nkeys, FOX_CHUNK), BF16)
        al_scr[ci(cs)] = ones
        acc_scr[ci(cs)] = jnp.zeros((acc_scr.shape[1], FOX_CHUNK), F32)
        ex_scr[ci(cs)] = jnp.zeros((1, FOX_CHUNK), F32)
    for e in range(1, nb):
        p_cur, p_prv = (pb_scr, pa_scr) if e % 2 == 1 else (pa_scr, pb_scr)
        for cs in chunks:
            bl, _ = own_block(cs)
            prev = first + max(bl - (e - 1), 0)
            if bl >= e:
                alpha = lazy_chunk(first + bl - e, cs, p_cur, None)
                pv_acc(prev, p_prv, cs)
                al_scr[ci(cs)] = alpha
            else:
                p_cur[ci(cs)] = jnp.zeros((tk, FOX_CHUNK), BF16)
                pv_acc(prev, p_prv, cs)
                al_scr[ci(cs)] = ones
    assert nb % 2 == 0

    def lazy_pair(jj):
        lazy_step(2 * jj + 1, 2 * jj + 2, pa_scr, pb_scr)
        lazy_step(2 * jj, 2 * jj + 1, pb_scr, pa_scr)

    npairs = first // 2
    pairs_per_trip = nb // 2

    def lazy_body(t, carry):
        for u in range(pairs_per_trip):
            lazy_pair(npairs - 1 - pairs_per_trip * t - u)
        return carry

    lax.fori_loop(0, i, lazy_body, 0)
    for cs in chunks:
        pv_acc(0, pb_scr, cs)
    emit()

    def scores(blk, s_out, mb_out, cs):
        s = qk(blk, cs)
        s_out[ci(cs)] = s
        if mb_out is not None:
            mb_out[ci(cs)] = jnp.max(s, axis=0, keepdims=True)

    def softmax(s_in, mb_in, p_out, key0, cs):
        s = s_in[ci(cs)]
        if key0 is None:
            mb = mb_in[ci(cs)]
        else:
            s = causal(s, cs, key0)
            mb = jnp.max(s, axis=0, keepdims=True)
        m_prev = m_scr[ci(cs)]
        m_new = jnp.maximum(m_prev, mb)
        p_out[ci(cs)] = jnp.exp2((s - m_new).astype(BF16))
        m_scr[ci(cs)] = m_new
        return jnp.exp2(m_prev - m_new)

    def pv(blk_prev, p_in, cs):
        acc_scr[ci(cs)] = al_scr[ci(cs)] * acc_scr[ci(cs)] + pv_dot(blk_prev, p_in, cs)

    def step(n, cur, nxt, p_prv, key0):
        s_cur, mb_cur, p_cur = cur
        for cs in chunks:
            if nxt is not None:
                scores(n + 1, nxt[0], nxt[1], cs)
            alpha = softmax(s_cur, mb_cur, p_cur, key0, cs)
            pv(jnp.maximum(n - 1, 0), p_prv, cs)
            al_scr[ci(cs)] = alpha

    buf_a = (sa_scr, mba_scr, pa_scr)
    buf_b = (sb_scr, mbb_scr, pb_scr)

    def body(jj, carry):
        step(2 * jj, buf_a, buf_b, pb_scr, None)
        step(2 * jj + 1, buf_b, buf_a, pa_scr, None)
        return carry

    @pl.when(jnp.max(ex_scr[...]) > FOX_MAX_EXCESS)
    def _():
        m_scr[...] = jnp.full(m_scr.shape, NEG_BIG, F32)
        al_scr[...] = jnp.ones_like(al_scr)
        acc_scr[...] = jnp.zeros_like(acc_scr)
        pb_scr[...] = jnp.zeros_like(pb_scr)
        for cs in chunks:
            scores(0, sa_scr, mba_scr, cs)
        lax.fori_loop(0, npairs, body, 0)
        for d in range(nb):
            cur, oth = (buf_a, buf_b) if d % 2 == 0 else (buf_b, buf_a)
            nxt = (oth[0], None) if d + 1 < nb else None
            step(first + d, cur, nxt, oth[2], d * tk)
        for cs in chunks:
            pv(first + nb - 1, pb_scr, cs)
        emit()


def _fox(z3, qt, ft, ka, vt):
    B, S, _ = z3.shape
    tq, tk = FOX_TQ, FOX_TK
    nq = S // tq
    ones_rows = 16
    nch = tq // FOX_CHUNK
    W = FOX_HEADS * FOX_DH
    kb = Z_COLS["fk"] // FOX_DH
    kern = functools.partial(_fox_kernel, tq=tq, tk=tk)
    return pl.pallas_call(
        kern,
        out_shape=jax.ShapeDtypeStruct((B, nq, W, tq), BF16),
        grid=(B, FOX_HEADS),
        in_specs=[pl.BlockSpec((1, nq, FOX_DH, tq), lambda b, h: (b, 0, h, 0)),
                  pl.BlockSpec((1, nq, LANES, tq), lambda b, h: (b, 0, 0, 0)),
                  pl.BlockSpec((1, S, FOX_DH), lambda b, h: (b, 0, kb + h)),
                  pl.BlockSpec((1, S, LANES), lambda b, h: (b, 0, 0)),
                  pl.BlockSpec((1, FOX_DH, S), lambda b, h: (b, h, 0))],
        out_specs=pl.BlockSpec((1, nq, FOX_DH, tq), lambda b, h: (b, 0, h, 0)),
        scratch_shapes=[pltpu.VMEM((S, 2 * FOX_DH), BF16),
                        pltpu.VMEM((S // tk, tk // FOX_KSUB, FOX_DH + ones_rows, FOX_KSUB), BF16),
                        pltpu.VMEM((nch, 2 * FOX_DH, FOX_CHUNK), BF16),
                        pltpu.VMEM((nch, tk, FOX_CHUNK), F32),
                        pltpu.VMEM((nch, tk, FOX_CHUNK), F32),
                        pltpu.VMEM((nch, 1, FOX_CHUNK), F32),
                        pltpu.VMEM((nch, 1, FOX_CHUNK), F32),
                        pltpu.VMEM((nch, tk, FOX_CHUNK), BF16),
                        pltpu.VMEM((nch, tk, FOX_CHUNK), BF16),
                        pltpu.VMEM((nch, 1, FOX_CHUNK), F32),
                        pltpu.VMEM((nch, 1, FOX_CHUNK), F32),
                        pltpu.VMEM((nch, FOX_DH + ones_rows, FOX_CHUNK), F32),
                        pltpu.VMEM((nch, 1, FOX_CHUNK), F32)],
        compiler_params=_cparams(("arbitrary", "arbitrary"), 56),
        name="fox",
    )(qt, ft, z3, ka, vt)


FFN_CHUNK = 256


def _tail_kernel(a_ref, yft_ref, gf_ref, x_ref, g1_ref, sc_ref, sh_ref, g2_ref, n2_ref, nf_ref,
                 wf_ref, wo_ref, wg_ref, wu_ref, wd_ref, o_ref, act_scr):
    tm = x_ref.shape[0]
    halves = [slice(r0, r0 + tm // 2) for r0 in range(0, tm, tm // 2)]
    x1s, hs = [], []
    for rs in halves:
        fo = lax.dot_general(yft_ref[0, 0, :, rs], wf_ref[...], _TN, preferred_element_type=F32)
        merged = a_ref[rs, :].astype(F32) + gf_ref[rs, :].astype(F32) * fo
        out = jnp.dot(merged.astype(BF16), wo_ref[...], preferred_element_type=F32)
        x1s.append(x_ref[rs, :] + g1_ref[0] * out)
    for x1 in x1s:
        ms = jnp.mean(x1 * x1, axis=-1, keepdims=True)
        y = x1 * lax.rsqrt(ms + EPS) * n2_ref[...]
        hs.append((y * (1.0 + sc_ref[0]) + sh_ref[0]).astype(BF16))
    for c0 in range(0, wg_ref.shape[1], FFN_CHUNK):
        for rs, h in zip(halves, hs):
            g = jnp.dot(h, wg_ref[:, c0:c0 + FFN_CHUNK], preferred_element_type=F32)
            u = jnp.dot(h, wu_ref[:, c0:c0 + FFN_CHUNK], preferred_element_type=F32)
            act_scr[rs, c0:c0 + FFN_CHUNK] = (g * jax.nn.sigmoid(g) * u).astype(BF16)
    for rs, x1 in zip(halves, x1s):
        ffn = jnp.dot(act_scr[rs, :], wd_ref[...], preferred_element_type=F32)
        x2 = x1 + g2_ref[0] * ffn
        ms2 = jnp.mean(x2 * x2, axis=-1, keepdims=True)
        o_ref[rs, :] = x2 * lax.rsqrt(ms2 + EPS) * nf_ref[...]


def _tail(a2, yft, z2, x2, g1, sc2, sh2, g2, n2, nf, wf, wo, wg, wu, wd, S):
    T, D = x2.shape
    FF = wg.shape[1]
    assert FF % FFN_CHUNK == 0
    tm = 512
    spb = S // tm
    tpq = FOX_TQ // tm
    row = lambda i: (i, 0)
    bidx = lambda i: (i // spb, 0, 0)
    const = lambda shape: pl.BlockSpec(shape, lambda i: (0,) * len(shape),
                                       pipeline_mode=pl.Buffered(1))
    return pl.pallas_call(
        _tail_kernel,
        out_shape=jax.ShapeDtypeStruct((T, D), F32),
        grid=(T // tm,),
        in_specs=[pl.BlockSpec((tm, D), row),
                  pl.BlockSpec((1, 1, D, tm),
                               lambda i: (i // spb, (i % spb) // tpq, 0, i % tpq)),
                  pl.BlockSpec((tm, D), lambda i: (i, Z_COLS["gf"] // D)),
                  pl.BlockSpec((tm, D), row),
                  pl.BlockSpec((1, 1, D), bidx),
                  pl.BlockSpec((1, 1, D), bidx),
                  pl.BlockSpec((1, 1, D), bidx),
                  pl.BlockSpec((1, 1, D), bidx),
                  const((1, D)),
                  const((1, D)),
                  const((D, D)),
                  const((D, D)),
                  const((D, FF)),
                  const((D, FF)),
                  const((FF, D))],
        out_specs=pl.BlockSpec((tm, D), row),
        scratch_shapes=[pltpu.VMEM((tm, FF), BF16)],
        compiler_params=_cparams(("arbitrary",), 58),
        name="tail",
    )(a2, yft, z2, x2, g1, sc2, sh2, g2, n2, nf, wf, wo, wg, wu, wd)


def _castw_kernel(wt_ref, o_ref):
    o_ref[...] = wt_ref[...].T.astype(BF16)


def _cast_inproj_weights(w_t, n_lead, gap):
    NT, D = w_t.shape
    tn = 1024
    n_out = NT - gap
    sub = 8
    assert n_lead % tn == 0 and n_out % tn == 0 and gap % sub == 0
    row0 = lambda j: sub * jnp.where(j < n_lead // tn, j * (tn // sub), j * (tn // sub) + gap // sub)
    return pl.pallas_call(
        _castw_kernel,
        out_shape=jax.ShapeDtypeStruct((D, n_out), BF16),
        grid=(n_out // tn,),
        in_specs=[pl.BlockSpec((pl.Element(tn), pl.Element(D)), lambda j: (row0(j), 0))],
        out_specs=pl.BlockSpec((D, tn), lambda j: (0, j)),
        compiler_params=_cparams(("arbitrary",), 32),
        name="castw",
    )(w_t)


def _rope_tables(S):
    half = RET_DK // 2
    freqs = ROPE_THETA ** (-np.arange(half, dtype=np.float64) / half)
    ang = np.arange(S, dtype=np.float64)[:, None] * freqs[None, :]
    cos, sin = np.cos(ang), np.sin(ang)
    return (jnp.asarray(np.concatenate([cos, cos], axis=1), F32),
            jnp.asarray(np.concatenate([-sin, sin], axis=1), F32))


def _aug_lanes(v8):
    rep = jnp.repeat(v8[..., None], 6, axis=-1)
    pad = jnp.zeros(v8.shape + (AUG_STRIDE - 6,), v8.dtype)
    return jnp.concatenate([rep, pad], axis=-1).reshape(v8.shape[:-1] + (LANES,))


def _layer(x, mod, norm1_w, w_in, b_f, ret_proj, fox_proj, w_out, norm2_w,
           w_gate, w_up, w_down, norm_out_w, tables):
    B, S, D = x.shape
    T = B * S
    sh1, sc1, g1, sh2, sc2, g2 = [m.reshape(B, 1, D) for m in jnp.split(mod, 6, axis=-1)]
    offs = np.cumsum([0, RET_HEADS * RET_DK, RET_HEADS * RET_DK, RET_HEADS * RET_DV,
                      RET_HEADS * RET_DV, FOX_HEADS * FOX_DH, FOX_HEADS * FOX_DH,
                      FOX_HEADS * FOX_DH, FOX_HEADS, D, D])
    o_ff, o_gr = int(offs[7]), int(offs[8])
    w_main = _cast_inproj_weights(jnp.swapaxes(w_in, 0, 1), o_ff, o_gr - o_ff)
    w_ff = _aug_lanes(w_in[:, o_ff:o_gr]).astype(BF16)
    bf_l = _aug_lanes(b_f).reshape(1, LANES)

    x2 = x.reshape(T, D)
    z, qt, vt, ft, ka = _inproj(x2, norm1_w.reshape(1, D), sc1, sh1, tables[0], tables[1],
                            w_main, w_ff, bf_l, B, S)
    z3 = z.reshape(B, S, z.shape[1])
    a = _retention(z3, ret_proj.astype(BF16), D)
    yf = _fox(z3, qt, ft, ka, vt)
    out = _tail(a.reshape(T, D), yf, z, x2, g1, sc2, sh2, g2,
                norm2_w.reshape(1, D), norm_out_w.reshape(1, D),
                fox_proj.astype(BF16), w_out.astype(BF16), w_gate.astype(BF16),
                w_up.astype(BF16), w_down.astype(BF16), S)
    return out.reshape(B, S, D)


def kernel(x, c, ada_w, ada_b, norm1_w, w_in, b_f, ret_proj, fox_proj, w_out,
           norm2_w, w_gate, w_up, w_down, norm_f_w):
    depth = ada_w.shape[0]
    assert depth == 1, "the final RMSNorm is fused into the last layer's channel mixer"
    tables = _rope_tables(x.shape[1])
    l = 0
    mod = _ada(c, ada_w[l], ada_b[l])
    return _layer(x, mod, norm1_w[l], w_in[l], b_f[l], ret_proj[l], fox_proj[l], w_out[l],
                  norm2_w[l], w_gate[l], w_up[l], w_down[l], norm_f_w, tables)
```

```python
import functools
import math

import numpy as np
import jax
import jax.numpy as jnp
from jax import lax
from jax.experimental import pallas as pl
from jax.experimental.pallas import tpu as pltpu

F32 = jnp.float32
BF16 = jnp.bfloat16

EPS = 1e-6
ROPE_THETA = 10000.0
RET_HEADS = 4
RET_DK = 128
RET_DV = 256
FOX_HEADS = 8
FOX_DH = 128
RET_CHUNK = 64
LANES = 128
SUBLANES = 8
AUG_STRIDE = 16

NEG_BIG = -1e30
LOG2E = math.log2(math.e)

_NT = (((1,), (1,)), ((), ()))
_TN = (((0,), (0,)), ((), ()))


def _cparams(sem, vmem_mb, flags=None):
    return pltpu.CompilerParams(dimension_semantics=sem,
                                vmem_limit_bytes=vmem_mb * 1024 * 1024, flags=flags)


def _ada_kernel(ct_ref, w_ref, b_ref, o_ref):
    ct = ct_ref[...]
    act = ct * jax.nn.sigmoid(ct)
    w = w_ref[...]
    for b in range(o_ref.shape[0]):
        o_ref[b:b + 1, :] = (jnp.sum(act[:, b:b + 1] * w, axis=0, keepdims=True)
                             + b_ref[...])


def _ada(c, w, b):
    B, D = c.shape
    N = w.shape[1]
    tn = 1536
    return pl.pallas_call(
        _ada_kernel,
        out_shape=jax.ShapeDtypeStruct((B, N), F32),
        grid=(N // tn,),
        in_specs=[pl.BlockSpec((D, B), lambda j: (0, 0)),
                  pl.BlockSpec((D, tn), lambda j: (0, j)),
                  pl.BlockSpec((1, tn), lambda j: (0, j))],
        out_specs=pl.BlockSpec((B, tn), lambda j: (0, j)),
        compiler_params=_cparams(("arbitrary",), 40),
        name="ada",
    )(c.T, w, b.reshape(1, N))


INPROJ_CHUNK = 512
FOX_TQ = 2048
FOX_TK = 512
Z_COLS = {"rq": 0, "rk": 512, "rv": 1024, "rg": 2048, "fk": 3072, "gr": 4096, "gf": 5120,
          "end": 6144}
W_COLS = {"rq": 0, "rk": 512, "rv": 1024, "rg": 2048, "fq": 3072, "fk": 4096, "fv": 5120,
          "gr": 6144, "gf": 7168, "end": 8192}


def _split3(v):
    p0 = v.astype(BF16)
    r1 = v - p0.astype(F32)
    p1 = r1.astype(BF16)
    r2 = r1 - p1.astype(F32)
    p2 = r2.astype(BF16)
    return p0, p1, p2


def _inproj_kernel(x0_ref, xn_ref, nw_ref, sc0_ref, sh0_ref, scn_ref, shn_ref, cos_ref, sin_ref,
                   w_ref, wff_ref, bf_ref, tri_ref,
                   z_ref, qt_ref, vt_ref, ft_ref, ka_ref, h_scr, carry,
                   *, fox_scale, rk_scale, spb):
    i = pl.program_id(0)

    def normed(x_ref, sc_ref, sh_ref):
        x = x_ref[...]
        ms = jnp.mean(x * x, axis=-1, keepdims=True)
        y = x * lax.rsqrt(ms + EPS) * nw_ref[...]
        return (y * (1.0 + sc_ref[0]) + sh_ref[0]).astype(BF16)

    @pl.when(i == 0)
    def _():
        h_scr[0] = normed(x0_ref, sc0_ref, sh0_ref)

    h = h_scr[i % 2]
    cos = cos_ref[...]
    sin = sin_ref[...]
    cw = INPROJ_CHUNK

    def chunk(w_ref, wc0):
        return jnp.dot(h, w_ref[:, wc0:wc0 + cw], preferred_element_type=F32)

    def rotary(name, scale):
        acc = chunk(w_ref, W_COLS[name])
        c0 = Z_COLS[name]
        for hh in range(cw // RET_DK):
            a = acc[:, hh * RET_DK:(hh + 1) * RET_DK]
            o = a * cos + pltpu.roll(a, RET_DK // 2, axis=1) * sin
            if scale is not None:
                o = o * scale
            z_ref[:, c0 + hh * RET_DK:c0 + (hh + 1) * RET_DK] = o.astype(BF16)

    def group(w_ref, wc0, name, width, fn):
        for d in range(0, width, cw):
            c0 = Z_COLS[name] + d
            z_ref[:, c0:c0 + cw] = fn(chunk(w_ref, wc0 + d)).astype(BF16)

    h_next = normed(xn_ref, scn_ref, shn_ref)

    xv = jnp.dot(h, wff_ref[...], preferred_element_type=F32) + bf_ref[...]
    lf = jnp.minimum(xv, 0.0) - jnp.log(1.0 + jnp.exp(-jnp.abs(xv)))
    p0, p1, p2 = _split3(lf)

    rotary("rq", None)
    rotary("rk", rk_scale)

    tri = tri_ref[...]
    cs = (jnp.dot(tri, p0, preferred_element_type=F32)
          + jnp.dot(tri, p1, preferred_element_type=F32)
          + jnp.dot(tri, p2, preferred_element_type=F32))
    first = (i % spb) == 0
    fc = cs + jnp.where(first, 0.0, carry[...])
    tm = fc.shape[0]
    carry[...] = fc[tm - 1:tm, :]

    group(w_ref, W_COLS["rv"], "rv", RET_HEADS * RET_DV, lambda a: a)

    f2l = fc * LOG2E
    ft_ref[0, 0] = f2l.T
    f0, f1, f2 = [p.astype(F32) for p in _split3(f2l)]
    c = lax.broadcasted_iota(jnp.int32, fc.shape, 1) % AUG_STRIDE
    ka_ref[0] = jnp.where(c < 3, 1.0, jnp.where(c == 3, -f0, jnp.where(
        c == 4, -f1, jnp.where(c == 5, -f2, 0.0)))).astype(BF16)

    group(w_ref, W_COLS["fk"], "fk", FOX_HEADS * FOX_DH, lambda a: a)
    group(w_ref, W_COLS["rg"], "rg", RET_HEADS * RET_DV, lambda a: a * jax.nn.sigmoid(a))
    for d in range(0, FOX_HEADS * FOX_DH, cw):
        vt_ref[0, d:d + cw, :] = chunk(w_ref, W_COLS["fv"] + d).T.astype(BF16)
    group(w_ref, W_COLS["gr"], "gr", W_COLS["end"] - W_COLS["gr"], jax.nn.sigmoid)
    for d in range(0, FOX_HEADS * FOX_DH, cw):
        qt_ref[0, 0, d:d + cw, :] = (chunk(w_ref, W_COLS["fq"] + d) * fox_scale).T.astype(BF16)
    h_scr[(i + 1) % 2] = h_next


def _inproj(x2, nw, sc, sh, cos_t, sin_t, w, wff, bf_l, B, S):
    T, D = x2.shape
    N = Z_COLS["end"]
    NQ = NV = FOX_HEADS * FOX_DH
    assert w.shape[1] == W_COLS["end"] == N + NQ + NV
    tm = 512
    spb = S // tm
    tpq = FOX_TQ // tm
    assert spb % tpq == 0
    tri = jnp.asarray(np.tril(np.ones((tm, tm), np.float32)), BF16)
    kern = functools.partial(_inproj_kernel, fox_scale=LOG2E / math.sqrt(FOX_DH),
                             rk_scale=RET_DK ** -0.5, spb=spb)
    const = lambda shape: pl.BlockSpec(shape, lambda i: (0,) * len(shape),
                                       pipeline_mode=pl.Buffered(1))
    nxt = lambda i: jnp.minimum(i + 1, T // tm - 1)
    return pl.pallas_call(
        kern,
        out_shape=(jax.ShapeDtypeStruct((T, N), BF16),
                   jax.ShapeDtypeStruct((B, S // FOX_TQ, NQ, FOX_TQ), BF16),
                   jax.ShapeDtypeStruct((B, NV, S), BF16),
                   jax.ShapeDtypeStruct((B, S // FOX_TQ, LANES, FOX_TQ), F32),
                   jax.ShapeDtypeStruct((B, S, LANES), BF16)),
        grid=(T // tm,),
        in_specs=[const((tm, D)),
                  pl.BlockSpec((tm, D), lambda i: (nxt(i), 0)),
                  const((1, D)),
                  const((1, 1, D)),
                  const((1, 1, D)),
                  pl.BlockSpec((1, 1, D), lambda i: (nxt(i) // spb, 0, 0)),
                  pl.BlockSpec((1, 1, D), lambda i: (nxt(i) // spb, 0, 0)),
                  pl.BlockSpec((tm, LANES), lambda i: (i % spb, 0)),
                  pl.BlockSpec((tm, LANES), lambda i: (i % spb, 0)),
                  const((D, W_COLS["end"])),
                  const((D, LANES)),
                  const((1, LANES)),
                  const((tm, tm))],
        out_specs=(pl.BlockSpec((tm, N), lambda i: (i, 0)),
                   pl.BlockSpec((1, 1, NQ, tm), lambda i: (i // spb, (i % spb) // tpq, 0, i % tpq)),
                   pl.BlockSpec((1, NV, tm), lambda i: (i // spb, 0, i % spb)),
                   pl.BlockSpec((1, 1, LANES, tm),
                                lambda i: (i // spb, (i % spb) // tpq, 0, i % tpq)),
                   pl.BlockSpec((1, tm, LANES), lambda i: (i // spb, i % spb, 0))),
        scratch_shapes=[pltpu.VMEM((2, tm, D), BF16), pltpu.VMEM((1, LANES), F32)],
        compiler_params=_cparams(("arbitrary",), 56),
        name="inproj",
    )(x2, x2, nw, sc, sh, sc, sh, cos_t, sin_t, w, wff, bf_l, tri)


def _ret_consts(L):
    hs = np.arange(RET_HEADS, dtype=np.float64)
    log_gamma = np.log(1.0 - np.exp2(-5.0 - hs))
    idx = np.arange(L, dtype=np.float64)
    dist = np.abs(idx[:, None] - idx[None, :])
    chunk_ok = (idx[None, :] // RET_CHUNK) <= (idx[:, None] // RET_CHUNK)
    dmask = np.exp(log_gamma[:, None, None] * dist[None]) * chunk_ok[None]
    qdec = np.exp(log_gamma[:, None] * idx[None, :])[..., None]
    kdec = np.exp(log_gamma[:, None] * (L - idx)[None, :])[..., None]
    bdec = [float(v) for v in np.exp(log_gamma * L)]
    return (jnp.asarray(dmask, F32), jnp.asarray(qdec, F32), jnp.asarray(kdec, F32), bdec)


def _ret_kernel(rq_ref, rk_ref, rv_ref, rg_ref, gr_ref, dm_ref, qd_ref, kd_ref, wr_ref,
                o_ref, st_scr, y_scr, *, bdec):
    @pl.when(pl.program_id(1) == 0)
    def _():
        st_scr[...] = jnp.zeros_like(st_scr)

    for h in range(RET_HEADS):
        q = rq_ref[0, :, h * RET_DK:(h + 1) * RET_DK]
        k = rk_ref[0, :, h * RET_DK:(h + 1) * RET_DK]
        v = rv_ref[0, :, h * RET_DV:(h + 1) * RET_DV]
        s = lax.dot_general(q, k, _NT, preferred_element_type=F32) * dm_ref[h]
        intra = jnp.dot(s.astype(BF16), v, preferred_element_type=F32)
        st = st_scr[h]
        inter = jnp.dot(q, st.astype(BF16), preferred_element_type=F32) * qd_ref[h]
        ks = (k.astype(F32) * kd_ref[h]).astype(BF16)
        st_scr[h] = bdec[h] * st + lax.dot_general(ks, v, _TN, preferred_element_type=F32)
        ro = intra + inter
        ron = ro * lax.rsqrt(jnp.mean(ro * ro, axis=-1, keepdims=True) + EPS)
        g = rg_ref[0, :, h * RET_DV:(h + 1) * RET_DV].astype(F32)
        y_scr[:, h * RET_DV:(h + 1) * RET_DV] = (g * ron).astype(BF16)

    proj = jnp.dot(y_scr[...], wr_ref[...], preferred_element_type=F32)
    o_ref[0] = (gr_ref[0].astype(F32) * proj).astype(BF16)


def _retention(z3, wr, D):
    B, S, _ = z3.shape
    L = 512
    dmask, qdec, kdec, bdec = _ret_consts(L)
    QK = RET_HEADS * RET_DK
    V = RET_HEADS * RET_DV
    kern = functools.partial(_ret_kernel, bdec=bdec)
    full3 = lambda b, i: (0, 0, 0)
    return pl.pallas_call(
        kern,
        out_shape=jax.ShapeDtypeStruct((B, S, D), BF16),
        grid=(B, S // L),
        in_specs=[pl.BlockSpec((1, L, QK), lambda b, i: (b, i, 0)),
                  pl.BlockSpec((1, L, QK), lambda b, i: (b, i, 1)),
                  pl.BlockSpec((1, L, V), lambda b, i: (b, i, 1)),
                  pl.BlockSpec((1, L, V), lambda b, i: (b, i, 2)),
                  pl.BlockSpec((1, L, D), lambda b, i: (b, i, Z_COLS["gr"] // D)),
                  pl.BlockSpec((RET_HEADS, L, L), full3),
                  pl.BlockSpec((RET_HEADS, L, 1), full3),
                  pl.BlockSpec((RET_HEADS, L, 1), full3),
                  pl.BlockSpec((V, D), lambda b, i: (0, 0))],
        out_specs=pl.BlockSpec((1, L, D), lambda b, i: (b, i, 0)),
        scratch_shapes=[pltpu.VMEM((RET_HEADS, RET_DK, RET_DV), F32),
                        pltpu.VMEM((L, V), BF16)],
        compiler_params=_cparams(("arbitrary", "arbitrary"), 40),
        name="ret",
    )(z3, z3, z3, z3, z3, dmask, qdec, kdec, wr)


FOX_CHUNK = 256
FOX_KSUB = 256
FOX_MAX_EXCESS = 64.0


def _fox_kernel(qt_ref, ft_ref, k_ref, ka_ref, vt_ref, o_ref,
                kx_scr, vt_scr, qxt_scr, sa_scr, sb_scr, mba_scr, mbb_scr, pa_scr, pb_scr,
                m_scr, al_scr, acc_scr, ex_scr, *, tq, tk):
    h = pl.program_id(1)
    nkb, nks = vt_scr.shape[0], vt_scr.shape[1]

    kx_scr[:, :FOX_DH] = k_ref[0]
    kx_scr[:, FOX_DH:] = ka_ref[0]
    qxt_scr[:, FOX_DH:, :] = jnp.zeros(
        (qxt_scr.shape[0], qxt_scr.shape[1] - FOX_DH, FOX_CHUNK), BF16)
    for n in range(nkb):
        for kk in range(nks):
            k0 = n * tk + kk * FOX_KSUB
            vt_scr[n, kk, :FOX_DH, :] = vt_ref[0, :, k0:k0 + FOX_KSUB]
            vt_scr[n, kk, FOX_DH:, :] = jnp.ones((vt_scr.shape[2] - FOX_DH, FOX_KSUB), BF16)

    def tile(i, carry):
        _fox_tile(i, h, qt_ref, ft_ref, o_ref, kx_scr, vt_scr, qxt_scr, sa_scr, sb_scr,
                  mba_scr, mbb_scr, pa_scr, pb_scr, m_scr, al_scr, acc_scr, ex_scr, tq=tq, tk=tk)
        return carry

    lax.fori_loop(0, qt_ref.shape[1], tile, 0)


def _fox_tile(i, h, qt_ref, ft_ref, o_ref, kx_scr, vt_scr, qxt_scr, sa_scr, sb_scr,
              mba_scr, mbb_scr, pa_scr, pb_scr, m_scr, al_scr, acc_scr, ex_scr, *, tq, tk):
    nks = vt_scr.shape[1]
    f_row = ft_ref[0, i, pl.ds(pl.multiple_of(AUG_STRIDE * h, AUG_STRIDE), SUBLANES), :][0:1, :]
    f0, f1, f2 = [p.astype(F32) for p in _split3(f_row)]
    sub = lax.broadcasted_iota(jnp.int32, (AUG_STRIDE, tq), 0)
    aug = jnp.where(sub == 0, f0, jnp.where(sub == 1, f1, jnp.where(
        sub == 2, f2, jnp.where(sub < 6, 1.0, 0.0))))
    aug = aug.astype(BF16)
    aug_rows = pl.ds(pl.multiple_of(FOX_DH + AUG_STRIDE * h, AUG_STRIDE), AUG_STRIDE)

    chunks = [slice(c0, c0 + FOX_CHUNK) for c0 in range(0, tq, FOX_CHUNK)]
    ci = lambda cs: cs.start // FOX_CHUNK
    for cs in chunks:
        qxt_scr[ci(cs), :FOX_DH, :] = qt_ref[0, i, :, cs]
        qxt_scr[ci(cs), aug_rows, :] = aug[:, cs]

    def qk(blk, cs, nkeys=tk):
        off = pl.multiple_of(blk * tk, tk)
        return jnp.dot(kx_scr[pl.ds(off, nkeys), :], qxt_scr[ci(cs)],
                       preferred_element_type=F32)

    def causal(s, cs, key0):
        key = lax.broadcasted_iota(jnp.int32, s.shape, 0) + key0
        qry = lax.broadcasted_iota(jnp.int32, s.shape, 1) + cs.start
        return jnp.where(key <= qry, s, NEG_BIG)

    def emit():
        for cs in chunks:
            acc = acc_scr[ci(cs)]
            o_ref[0, i, :, cs] = (acc[:FOX_DH] / acc[FOX_DH:FOX_DH + 1]).astype(BF16)

    def pv_dot(blk, p, cs):
        return sum(jnp.dot(vt_scr[blk, kk], p[ci(cs), kk * FOX_KSUB:(kk + 1) * FOX_KSUB, :],
                           preferred_element_type=F32) for kk in range(nks))

    def pv_acc(blk_prev, p_prv, cs):
        acc_scr[ci(cs)] = al_scr[ci(cs)] * (acc_scr[ci(cs)] + pv_dot(blk_prev, p_prv, cs))

    def lazy_chunk(blk, cs, p_cur, key0):
        s = qk(blk, cs)
        if key0 is not None:
            s = causal(s, cs, key0)
        mb = jnp.max(s, axis=0, keepdims=True)
        r_old = m_scr[ci(cs)]
        p_cur[ci(cs)] = jnp.exp2((s - r_old).astype(BF16))
        r_new = jnp.maximum(r_old, mb)
        m_scr[ci(cs)] = r_new
        ex_scr[ci(cs)] = jnp.maximum(ex_scr[ci(cs)], mb - r_old)
        return jnp.exp2(r_old - r_new)

    def lazy_step(blk, blk_prev, p_cur, p_prv):
        for cs in chunks:
            alpha = lazy_chunk(blk, cs, p_cur, None)
            pv_acc(blk_prev, p_prv, cs)
            al_scr[ci(cs)] = alpha

    nb = tq // tk
    first = nb * i
    ones = jnp.ones((1, FOX_CHUNK), F32)

    def own_block(cs):
        return cs.start // tk, (cs.start // tk) * tk

    for cs in chunks:
        bl, key0 = own_block(cs)
        nkeys = min(tk, cs.start + FOX_CHUNK - key0)
        s = causal(qk(first + bl, cs, nkeys), cs, key0)
        mb = jnp.max(s, axis=0, keepdims=True)
        m_scr[ci(cs)] = mb
        pa_scr[ci(cs), :nkeys, :] = jnp.exp2((s - mb).astype(BF16))
        if nkeys < tk:
            pa_scr[ci(cs), nkeys:, :] = jnp.zeros((tk - nkeys, FOX_CHUNK), BF16)
        al_scr[ci(cs)] = ones
        acc_scr[ci(cs)] = jnp.zeros((acc_scr.shape[1], FOX_CHUNK), F32)
        ex_scr[ci(cs)] = jnp.zeros((1, FOX_CHUNK), F32)
    for e in range(1, nb):
        p_cur, p_prv = (pb_scr, pa_scr) if e % 2 == 1 else (pa_scr, pb_scr)
        for cs in chunks:
            bl, _ = own_block(cs)
            prev = first + max(bl - (e - 1), 0)
            if bl >= e:
                alpha = lazy_chunk(first + bl - e, cs, p_cur, None)
                pv_acc(prev, p_prv, cs)
                al_scr[ci(cs)] = alpha
            else:
                if bl == e - 1:
                    pv_acc(prev, p_prv, cs)
                    al_scr[ci(cs)] = ones
                if p_cur is pb_scr and e in (bl + 1, bl + 2):
                    pb_scr[ci(cs)] = jnp.zeros((tk, FOX_CHUNK), BF16)
    assert nb % 2 == 0

    def lazy_pair(jj):
        lazy_step(2 * jj + 1, 2 * jj + 2, pa_scr, pb_scr)
        lazy_step(2 * jj, 2 * jj + 1, pb_scr, pa_scr)

    npairs = first // 2
    pairs_per_trip = nb // 2

    def lazy_body(t, carry):
        for u in range(pairs_per_trip):
            lazy_pair(npairs - 1 - pairs_per_trip * t - u)
        return carry

    lax.fori_loop(0, i, lazy_body, 0)
    for cs in chunks:
        pv_acc(0, pb_scr, cs)
    emit()

    def scores(blk, s_out, mb_out, cs):
        s = qk(blk, cs)
        s_out[ci(cs)] = s
        if mb_out is not None:
            mb_out[ci(cs)] = jnp.max(s, axis=0, keepdims=True)

    def softmax(s_in, mb_in, p_out, key0, cs):
        s = s_in[ci(cs)]
        if key0 is None:
            mb = mb_in[ci(cs)]
        else:
            s = causal(s, cs, key0)
            mb = jnp.max(s, axis=0, keepdims=True)
        m_prev = m_scr[ci(cs)]
        m_new = jnp.maximum(m_prev, mb)
        p_out[ci(cs)] = jnp.exp2((s - m_new).astype(BF16))
        m_scr[ci(cs)] = m_new
        return jnp.exp2(m_prev - m_new)

    def pv(blk_prev, p_in, cs):
        acc_scr[ci(cs)] = al_scr[ci(cs)] * acc_scr[ci(cs)] + pv_dot(blk_prev, p_in, cs)

    def step(n, cur, nxt, p_prv, key0):
        s_cur, mb_cur, p_cur = cur
        for cs in chunks:
            if nxt is not None:
                scores(n + 1, nxt[0], nxt[1], cs)
            alpha = softmax(s_cur, mb_cur, p_cur, key0, cs)
            pv(jnp.maximum(n - 1, 0), p_prv, cs)
            al_scr[ci(cs)] = alpha

    buf_a = (sa_scr, mba_scr, pa_scr)
    buf_b = (sb_scr, mbb_scr, pb_scr)

    def body(jj, carry):
        step(2 * jj, buf_a, buf_b, pb_scr, None)
        step(2 * jj + 1, buf_b, buf_a, pa_scr, None)
        return carry

    @pl.when(jnp.max(ex_scr[...]) > FOX_MAX_EXCESS)
    def _():
        m_scr[...] = jnp.full(m_scr.shape, NEG_BIG, F32)
        al_scr[...] = jnp.ones_like(al_scr)
        acc_scr[...] = jnp.zeros_like(acc_scr)
        pb_scr[...] = jnp.zeros_like(pb_scr)
        for cs in chunks:
            scores(0, sa_scr, mba_scr, cs)
        lax.fori_loop(0, npairs, body, 0)
        for d in range(nb):
            cur, oth = (buf_a, buf_b) if d % 2 == 0 else (buf_b, buf_a)
            nxt = (oth[0], None) if d + 1 < nb else None
            step(first + d, cur, nxt, oth[2], d * tk)
        for cs in chunks:
            pv(first + nb - 1, pb_scr, cs)
        emit()


def _fox(z3, qt, ft, ka, vt):
    B, S, _ = z3.shape
    tq, tk = FOX_TQ, FOX_TK
    assert tq % (2 * tk) == 0 and tk % FOX_KSUB == 0 and tk % FOX_CHUNK == 0 and S % tq == 0
    nq = S // tq
    ones_rows = 16
    nch = tq // FOX_CHUNK
    W = FOX_HEADS * FOX_DH
    kb = Z_COLS["fk"] // FOX_DH
    kern = functools.partial(_fox_kernel, tq=tq, tk=tk)
    return pl.pallas_call(
        kern,
        out_shape=jax.ShapeDtypeStruct((B, nq, W, tq), BF16),
        grid=(B, FOX_HEADS),
        in_specs=[pl.BlockSpec((1, nq, FOX_DH, tq), lambda b, h: (b, 0, h, 0)),
                  pl.BlockSpec((1, nq, LANES, tq), lambda b, h: (b, 0, 0, 0)),
                  pl.BlockSpec((1, S, FOX_DH), lambda b, h: (b, 0, kb + h)),
                  pl.BlockSpec((1, S, LANES), lambda b, h: (b, 0, 0)),
                  pl.BlockSpec((1, FOX_DH, S), lambda b, h: (b, h, 0))],
        out_specs=pl.BlockSpec((1, nq, FOX_DH, tq), lambda b, h: (b, 0, h, 0)),
        scratch_shapes=[pltpu.VMEM((S, 2 * FOX_DH), BF16),
                        pltpu.VMEM((S // tk, tk // FOX_KSUB, FOX_DH + ones_rows, FOX_KSUB), BF16),
                        pltpu.VMEM((nch, 2 * FOX_DH, FOX_CHUNK), BF16),
                        pltpu.VMEM((nch, tk, FOX_CHUNK), F32),
                        pltpu.VMEM((nch, tk, FOX_CHUNK), F32),
                        pltpu.VMEM((nch, 1, FOX_CHUNK), F32),
                        pltpu.VMEM((nch, 1, FOX_CHUNK), F32),
                        pltpu.VMEM((nch, tk, FOX_CHUNK), BF16),
                        pltpu.VMEM((nch, tk, FOX_CHUNK), BF16),
                        pltpu.VMEM((nch, 1, FOX_CHUNK), F32),
                        pltpu.VMEM((nch, 1, FOX_CHUNK), F32),
                        pltpu.VMEM((nch, FOX_DH + ones_rows, FOX_CHUNK), F32),
                        pltpu.VMEM((nch, 1, FOX_CHUNK), F32)],
        compiler_params=_cparams(("arbitrary", "arbitrary"), 56),
        name="fox",
    )(qt, ft, z3, ka, vt)


FFN_CHUNK = 256


def _tail_kernel(a_ref, yft_ref, gf_ref, x_ref, g1_ref, sc_ref, sh_ref, g2_ref, n2_ref, nf_ref,
                 wf_ref, wo_ref, wg_ref, wu_ref, wd_ref, o_ref, act_scr):
    tm = x_ref.shape[0]
    halves = [slice(r0, r0 + tm // 2) for r0 in range(0, tm, tm // 2)]
    x1s, hs = [], []
    for rs in halves:
        fo = lax.dot_general(yft_ref[0, 0, :, rs], wf_ref[...], _TN, preferred_element_type=F32)
        merged = a_ref[rs, :].astype(F32) + gf_ref[rs, :].astype(F32) * fo
        out = jnp.dot(merged.astype(BF16), wo_ref[...], preferred_element_type=F32)
        x1s.append(x_ref[rs, :] + g1_ref[0] * out)
    for x1 in x1s:
        ms = jnp.mean(x1 * x1, axis=-1, keepdims=True)
        y = x1 * lax.rsqrt(ms + EPS) * n2_ref[...]
        hs.append((y * (1.0 + sc_ref[0]) + sh_ref[0]).astype(BF16))
    for c0 in range(0, wg_ref.shape[1], FFN_CHUNK):
        for rs, h in zip(halves, hs):
            g = jnp.dot(h, wg_ref[:, c0:c0 + FFN_CHUNK], preferred_element_type=F32)
            u = jnp.dot(h, wu_ref[:, c0:c0 + FFN_CHUNK], preferred_element_type=F32)
            act_scr[rs, c0:c0 + FFN_CHUNK] = (g * jax.nn.sigmoid(g) * u).astype(BF16)
    for rs, x1 in zip(halves, x1s):
        ffn = jnp.dot(act_scr[rs, :], wd_ref[...], preferred_element_type=F32)
        x2 = x1 + g2_ref[0] * ffn
        ms2 = jnp.mean(x2 * x2, axis=-1, keepdims=True)
        o_ref[rs, :] = x2 * lax.rsqrt(ms2 + EPS) * nf_ref[...]


def _tail(a2, yft, z2, x2, g1, sc2, sh2, g2, n2, nf, wf, wo, wg, wu, wd, S):
    T, D = x2.shape
    FF = wg.shape[1]
    assert FF % FFN_CHUNK == 0
    tm = 512
    spb = S // tm
    tpq = FOX_TQ // tm
    row = lambda i: (i, 0)
    bidx = lambda i: (i // spb, 0, 0)
    const = lambda shape: pl.BlockSpec(shape, lambda i: (0,) * len(shape),
                                       pipeline_mode=pl.Buffered(1))
    return pl.pallas_call(
        _tail_kernel,
        out_shape=jax.ShapeDtypeStruct((T, D), F32),
        grid=(T // tm,),
        in_specs=[pl.BlockSpec((tm, D), row),
                  pl.BlockSpec((1, 1, D, tm),
                               lambda i: (i // spb, (i % spb) // tpq, 0, i % tpq)),
                  pl.BlockSpec((tm, D), lambda i: (i, Z_COLS["gf"] // D)),
                  pl.BlockSpec((tm, D), row),
                  pl.BlockSpec((1, 1, D), bidx),
                  pl.BlockSpec((1, 1, D), bidx),
                  pl.BlockSpec((1, 1, D), bidx),
                  pl.BlockSpec((1, 1, D), bidx),
                  const((1, D)),
                  const((1, D)),
                  const((D, D)),
                  const((D, D)),
                  const((D, FF)),
                  const((D, FF)),
                  const((FF, D))],
        out_specs=pl.BlockSpec((tm, D), row),
        scratch_shapes=[pltpu.VMEM((tm, FF), BF16)],
        compiler_params=_cparams(("arbitrary",), 58),
        name="tail",
    )(a2, yft, z2, x2, g1, sc2, sh2, g2, n2, nf, wf, wo, wg, wu, wd)


def _castw_kernel(wt_ref, o_ref):
    o_ref[...] = wt_ref[...].T.astype(BF16)


def _cast_inproj_weights(w_t, n_lead, gap):
    NT, D = w_t.shape
    tn = 1024
    n_out = NT - gap
    sub = SUBLANES
    assert n_lead % tn == 0 and n_out % tn == 0 and gap % sub == 0
    row0 = lambda j: sub * jnp.where(j < n_lead // tn, j * (tn // sub), j * (tn // sub) + gap // sub)
    return pl.pallas_call(
        _castw_kernel,
        out_shape=jax.ShapeDtypeStruct((D, n_out), BF16),
        grid=(n_out // tn,),
        in_specs=[pl.BlockSpec((pl.Element(tn), pl.Element(D)), lambda j: (row0(j), 0))],
        out_specs=pl.BlockSpec((D, tn), lambda j: (0, j)),
        compiler_params=_cparams(("arbitrary",), 32),
        name="castw",
    )(w_t)


def _rope_tables(S):
    half = RET_DK // 2
    freqs = ROPE_THETA ** (-np.arange(half, dtype=np.float64) / half)
    ang = np.arange(S, dtype=np.float64)[:, None] * freqs[None, :]
    cos, sin = np.cos(ang), np.sin(ang)
    return (jnp.asarray(np.concatenate([cos, cos], axis=1), F32),
            jnp.asarray(np.concatenate([-sin, sin], axis=1), F32))


def _aug_lanes(v8):
    rep = jnp.repeat(v8[..., None], 6, axis=-1)
    pad = jnp.zeros(v8.shape + (AUG_STRIDE - 6,), v8.dtype)
    return jnp.concatenate([rep, pad], axis=-1).reshape(v8.shape[:-1] + (LANES,))


def _layer(x, mod, norm1_w, w_in, b_f, ret_proj, fox_proj, w_out, norm2_w,
           w_gate, w_up, w_down, norm_out_w, tables):
    B, S, D = x.shape
    T = B * S
    sh1, sc1, g1, sh2, sc2, g2 = [m.reshape(B, 1, D) for m in jnp.split(mod, 6, axis=-1)]
    offs = np.cumsum([0, RET_HEADS * RET_DK, RET_HEADS * RET_DK, RET_HEADS * RET_DV,
                      RET_HEADS * RET_DV, FOX_HEADS * FOX_DH, FOX_HEADS * FOX_DH,
                      FOX_HEADS * FOX_DH, FOX_HEADS, D, D])
    o_ff, o_gr = int(offs[7]), int(offs[8])
    w_main = _cast_inproj_weights(jnp.swapaxes(w_in, 0, 1), o_ff, o_gr - o_ff)
    w_ff = _aug_lanes(w_in[:, o_ff:o_gr]).astype(BF16)
    bf_l = _aug_lanes(b_f).reshape(1, LANES)

    x2 = x.reshape(T, D)
    z, qt, vt, ft, ka = _inproj(x2, norm1_w.reshape(1, D), sc1, sh1, tables[0], tables[1],
                            w_main, w_ff, bf_l, B, S)
    z3 = z.reshape(B, S, z.shape[1])
    a = _retention(z3, ret_proj.astype(BF16), D)
    yf = _fox(z3, qt, ft, ka, vt)
    out = _tail(a.reshape(T, D), yf, z, x2, g1, sc2, sh2, g2,
                norm2_w.reshape(1, D), norm_out_w.reshape(1, D),
                fox_proj.astype(BF16), w_out.astype(BF16), w_gate.astype(BF16),
                w_up.astype(BF16), w_down.astype(BF16), S)
    return out.reshape(B, S, D)


def kernel(x, c, ada_w, ada_b, norm1_w, w_in, b_f, ret_proj, fox_proj, w_out,
           norm2_w, w_gate, w_up, w_down, norm_f_w):
    depth = ada_w.shape[0]
    assert depth == 1, "the final RMSNorm is fused into the last layer's channel mixer"
    tables = _rope_tables(x.shape[1])
    l = 0
    mod = _ada(c, ada_w[l], ada_b[l])
    return _layer(x, mod, norm1_w[l], w_in[l], b_f[l], ret_proj[l], fox_proj[l], w_out[l],
                  norm2_w[l], w_gate[l], w_up[l], w_down[l], norm_f_w, tables)
```

```python
import functools
import math

import numpy as np
import jax
import jax.numpy as jnp
from jax import lax
from jax.experimental import pallas as pl
from jax.experimental.pallas import tpu as pltpu

F32 = jnp.float32
BF16 = jnp.bfloat16

EPS = 1e-6
ROPE_THETA = 10000.0
RET_HEADS = 4
RET_DK = 128
RET_DV = 256
FOX_HEADS = 8
FOX_DH = 128
RET_CHUNK = 64
LANES = 128
SUBLANES = 8
AUG_STRIDE = 16

NEG_BIG = -1e30
LOG2E = math.log2(math.e)

_NT = (((1,), (1,)), ((), ()))
_TN = (((0,), (0,)), ((), ()))


def _sigmoid(v):
    return 0.5 + 0.5 * jnp.tanh(0.5 * v)


def _cparams(sem, vmem_mb, flags=None):
    return pltpu.CompilerParams(dimension_semantics=sem,
                                vmem_limit_bytes=vmem_mb * 1024 * 1024, flags=flags)


def _ada_kernel(ct_ref, w_ref, b_ref, o_ref):
    ct = ct_ref[...]
    act = ct * jax.nn.sigmoid(ct)
    w = w_ref[...]
    for b in range(o_ref.shape[0]):
        o_ref[b:b + 1, :] = (jnp.sum(act[:, b:b + 1] * w, axis=0, keepdims=True)
                             + b_ref[...])


def _ada(c, w, b):
    B, D = c.shape
    N = w.shape[1]
    tn = 1536
    return pl.pallas_call(
        _ada_kernel,
        out_shape=jax.ShapeDtypeStruct((B, N), F32),
        grid=(N // tn,),
        in_specs=[pl.BlockSpec((D, B), lambda j: (0, 0)),
                  pl.BlockSpec((D, tn), lambda j: (0, j)),
                  pl.BlockSpec((1, tn), lambda j: (0, j))],
        out_specs=pl.BlockSpec((B, tn), lambda j: (0, j)),
        compiler_params=_cparams(("arbitrary",), 40),
        name="ada",
    )(c.T, w, b.reshape(1, N))


INPROJ_CHUNK = 512
FOX_TQ = 2048
FOX_TK = 512
Z_COLS = {"rq": 0, "rk": 512, "rv": 1024, "rg": 2048, "fk": 3072, "gr": 4096, "gf": 5120,
          "end": 6144}
W_COLS = {"rq": 0, "rk": 512, "rv": 1024, "rg": 2048, "fq": 3072, "fk": 4096, "fv": 5120,
          "gr": 6144, "gf": 7168, "end": 8192}


def _split3(v):
    p0 = v.astype(BF16)
    r1 = v - p0.astype(F32)
    p1 = r1.astype(BF16)
    r2 = r1 - p1.astype(F32)
    p2 = r2.astype(BF16)
    return p0, p1, p2


def _inproj_kernel(x0_ref, xn_ref, nw_ref, sc0_ref, sh0_ref, scn_ref, shn_ref, cos_ref, sin_ref,
                   w_ref, wff_ref, bf_ref, tri_ref,
                   z_ref, qt_ref, vt_ref, ft_ref, ka_ref, h_scr, carry,
                   *, fox_scale, rk_scale, spb):
    i = pl.program_id(0)

    def normed(x_ref, sc_ref, sh_ref):
        x = x_ref[...]
        ms = jnp.mean(x * x, axis=-1, keepdims=True)
        y = x * lax.rsqrt(ms + EPS) * nw_ref[...]
        return (y * (1.0 + sc_ref[0]) + sh_ref[0]).astype(BF16)

    @pl.when(i == 0)
    def _():
        h_scr[0] = normed(x0_ref, sc0_ref, sh0_ref)

    h = h_scr[i % 2]
    cos = cos_ref[...]
    sin = sin_ref[...]
    cw = INPROJ_CHUNK

    def chunk(w_ref, wc0):
        return jnp.dot(h, w_ref[:, wc0:wc0 + cw], preferred_element_type=F32)

    def rotary(name, scale):
        acc = chunk(w_ref, W_COLS[name])
        c0 = Z_COLS[name]
        for hh in range(cw // RET_DK):
            a = acc[:, hh * RET_DK:(hh + 1) * RET_DK]
            o = a * cos + pltpu.roll(a, RET_DK // 2, axis=1) * sin
            if scale is not None:
                o = o * scale
            z_ref[:, c0 + hh * RET_DK:c0 + (hh + 1) * RET_DK] = o.astype(BF16)

    def group(w_ref, wc0, name, width, fn):
        for d in range(0, width, cw):
            c0 = Z_COLS[name] + d
            z_ref[:, c0:c0 + cw] = fn(chunk(w_ref, wc0 + d)).astype(BF16)

    h_next = normed(xn_ref, scn_ref, shn_ref)

    xv = jnp.dot(h, wff_ref[...], preferred_element_type=F32) + bf_ref[...]
    lf = jnp.minimum(xv, 0.0) - jnp.log(1.0 + jnp.exp(-jnp.abs(xv)))
    p0, p1, p2 = _split3(lf)

    rotary("rq", None)
    rotary("rk", rk_scale)

    tri = tri_ref[...]
    cs = (jnp.dot(tri, p0, preferred_element_type=F32)
          + jnp.dot(tri, p1, preferred_element_type=F32)
          + jnp.dot(tri, p2, preferred_element_type=F32))
    first = (i % spb) == 0
    fc = cs + jnp.where(first, 0.0, carry[...])
    tm = fc.shape[0]
    carry[...] = fc[tm - 1:tm, :]

    group(w_ref, W_COLS["rv"], "rv", RET_HEADS * RET_DV, lambda a: a)

    f2l = fc * LOG2E
    ft_ref[0, 0] = f2l.T
    f0, f1, f2 = [p.astype(F32) for p in _split3(f2l)]
    c = lax.broadcasted_iota(jnp.int32, fc.shape, 1) % AUG_STRIDE
    ka_ref[0] = jnp.where(c < 3, 1.0, jnp.where(c == 3, -f0, jnp.where(
        c == 4, -f1, jnp.where(c == 5, -f2, 0.0)))).astype(BF16)

    group(w_ref, W_COLS["fk"], "fk", FOX_HEADS * FOX_DH, lambda a: a)
    group(w_ref, W_COLS["rg"], "rg", RET_HEADS * RET_DV, lambda a: a * _sigmoid(a))
    for d in range(0, FOX_HEADS * FOX_DH, cw):
        vt_ref[0, d:d + cw, :] = chunk(w_ref, W_COLS["fv"] + d).T.astype(BF16)
    group(w_ref, W_COLS["gr"], "gr", W_COLS["end"] - W_COLS["gr"], _sigmoid)
    for d in range(0, FOX_HEADS * FOX_DH, cw):
        qt_ref[0, 0, d:d + cw, :] = (chunk(w_ref, W_COLS["fq"] + d) * fox_scale).T.astype(BF16)
    h_scr[(i + 1) % 2] = h_next


def _inproj(x2, nw, sc, sh, cos_t, sin_t, w, wff, bf_l, B, S):
    T, D = x2.shape
    N = Z_COLS["end"]
    NQ = NV = FOX_HEADS * FOX_DH
    assert w.shape[1] == W_COLS["end"] == N + NQ + NV
    tm = 512
    spb = S // tm
    tpq = FOX_TQ // tm
    assert spb % tpq == 0
    tri = jnp.asarray(np.tril(np.ones((tm, tm), np.float32)), BF16)
    kern = functools.partial(_inproj_kernel, fox_scale=LOG2E / math.sqrt(FOX_DH),
                             rk_scale=RET_DK ** -0.5, spb=spb)
    const = lambda shape: pl.BlockSpec(shape, lambda i: (0,) * len(shape),
                                       pipeline_mode=pl.Buffered(1))
    nxt = lambda i: jnp.minimum(i + 1, T // tm - 1)
    return pl.pallas_call(
        kern,
        out_shape=(jax.ShapeDtypeStruct((T, N), BF16),
                   jax.ShapeDtypeStruct((B, S // FOX_TQ, NQ, FOX_TQ), BF16),
                   jax.ShapeDtypeStruct((B, NV, S), BF16),
                   jax.ShapeDtypeStruct((B, S // FOX_TQ, LANES, FOX_TQ), F32),
                   jax.ShapeDtypeStruct((B, S, LANES), BF16)),
        grid=(T // tm,),
        in_specs=[const((tm, D)),
                  pl.BlockSpec((tm, D), lambda i: (nxt(i), 0)),
                  const((1, D)),
                  const((1, 1, D)),
                  const((1, 1, D)),
                  pl.BlockSpec((1, 1, D), lambda i: (nxt(i) // spb, 0, 0)),
                  pl.BlockSpec((1, 1, D), lambda i: (nxt(i) // spb, 0, 0)),
                  pl.BlockSpec((tm, LANES), lambda i: (i % spb, 0)),
                  pl.BlockSpec((tm, LANES), lambda i: (i % spb, 0)),
                  const((D, W_COLS["end"])),
                  const((D, LANES)),
                  const((1, LANES)),
                  const((tm, tm))],
        out_specs=(pl.BlockSpec((tm, N), lambda i: (i, 0)),
                   pl.BlockSpec((1, 1, NQ, tm), lambda i: (i // spb, (i % spb) // tpq, 0, i % tpq)),
                   pl.BlockSpec((1, NV, tm), lambda i: (i // spb, 0, i % spb)),
                   pl.BlockSpec((1, 1, LANES, tm),
                                lambda i: (i // spb, (i % spb) // tpq, 0, i % tpq)),
                   pl.BlockSpec((1, tm, LANES), lambda i: (i // spb, i % spb, 0))),
        scratch_shapes=[pltpu.VMEM((2, tm, D), BF16), pltpu.VMEM((1, LANES), F32)],
        compiler_params=_cparams(("arbitrary",), 56),
        name="inproj",
    )(x2, x2, nw, sc, sh, sc, sh, cos_t, sin_t, w, wff, bf_l, tri)


def _ret_consts(L):
    hs = np.arange(RET_HEADS, dtype=np.float64)
    log_gamma = np.log(1.0 - np.exp2(-5.0 - hs))
    idx = np.arange(L, dtype=np.float64)
    dist = np.abs(idx[:, None] - idx[None, :])
    chunk_ok = (idx[None, :] // RET_CHUNK) <= (idx[:, None] // RET_CHUNK)
    dmask = np.exp(log_gamma[:, None, None] * dist[None]) * chunk_ok[None]
    qdec = np.exp(log_gamma[:, None] * idx[None, :])[..., None]
    kdec = np.exp(log_gamma[:, None] * (L - idx)[None, :])[..., None]
    bdec = [float(v) for v in np.exp(log_gamma * L)]
    return (jnp.asarray(dmask, F32), jnp.asarray(qdec, F32), jnp.asarray(kdec, F32), bdec)


def _ret_kernel(rq_ref, rk_ref, rv_ref, rg_ref, gr_ref, dm_ref, qd_ref, kd_ref, wr_ref,
                o_ref, st_scr, y_scr, *, bdec):
    @pl.when(pl.program_id(1) == 0)
    def _():
        st_scr[...] = jnp.zeros_like(st_scr)

    for h in range(RET_HEADS):
        q = rq_ref[0, :, h * RET_DK:(h + 1) * RET_DK]
        k = rk_ref[0, :, h * RET_DK:(h + 1) * RET_DK]
        v = rv_ref[0, :, h * RET_DV:(h + 1) * RET_DV]
        s = lax.dot_general(q, k, _NT, preferred_element_type=F32) * dm_ref[h]
        intra = jnp.dot(s.astype(BF16), v, preferred_element_type=F32)
        st = st_scr[h]
        inter = jnp.dot(q, st.astype(BF16), preferred_element_type=F32) * qd_ref[h]
        ks = (k.astype(F32) * kd_ref[h]).astype(BF16)
        st_scr[h] = bdec[h] * st + lax.dot_general(ks, v, _TN, preferred_element_type=F32)
        ro = intra + inter
        ron = ro * lax.rsqrt(jnp.mean(ro * ro, axis=-1, keepdims=True) + EPS)
        g = rg_ref[0, :, h * RET_DV:(h + 1) * RET_DV].astype(F32)
        y_scr[:, h * RET_DV:(h + 1) * RET_DV] = (g * ron).astype(BF16)

    proj = jnp.dot(y_scr[...], wr_ref[...], preferred_element_type=F32)
    o_ref[0] = (gr_ref[0].astype(F32) * proj).astype(BF16)


def _retention(z3, wr, D):
    B, S, _ = z3.shape
    L = 512
    dmask, qdec, kdec, bdec = _ret_consts(L)
    QK = RET_HEADS * RET_DK
    V = RET_HEADS * RET_DV
    kern = functools.partial(_ret_kernel, bdec=bdec)
    full3 = lambda b, i: (0, 0, 0)
    return pl.pallas_call(
        kern,
        out_shape=jax.ShapeDtypeStruct((B, S, D), BF16),
        grid=(B, S // L),
        in_specs=[pl.BlockSpec((1, L, QK), lambda b, i: (b, i, 0)),
                  pl.BlockSpec((1, L, QK), lambda b, i: (b, i, 1)),
                  pl.BlockSpec((1, L, V), lambda b, i: (b, i, 1)),
                  pl.BlockSpec((1, L, V), lambda b, i: (b, i, 2)),
                  pl.BlockSpec((1, L, D), lambda b, i: (b, i, Z_COLS["gr"] // D)),
                  pl.BlockSpec((RET_HEADS, L, L), full3),
                  pl.BlockSpec((RET_HEADS, L, 1), full3),
                  pl.BlockSpec((RET_HEADS, L, 1), full3),
                  pl.BlockSpec((V, D), lambda b, i: (0, 0))],
        out_specs=pl.BlockSpec((1, L, D), lambda b, i: (b, i, 0)),
        scratch_shapes=[pltpu.VMEM((RET_HEADS, RET_DK, RET_DV), F32),
                        pltpu.VMEM((L, V), BF16)],
        compiler_params=_cparams(("arbitrary", "arbitrary"), 40),
        name="ret",
    )(z3, z3, z3, z3, z3, dmask, qdec, kdec, wr)


FOX_CHUNK = 256
FOX_KSUB = 256
FOX_MAX_EXCESS = 64.0


def _fox_kernel(qt_ref, ft_ref, k_ref, ka_ref, vt_ref, o_ref,
                kx_scr, vt_scr, qxt_scr, sa_scr, sb_scr, mba_scr, mbb_scr, pa_scr, pb_scr,
                m_scr, al_scr, acc_scr, ex_scr, *, tq, tk):
    h = pl.program_id(1)
    nkb, nks = vt_scr.shape[0], vt_scr.shape[1]

    kx_scr[:, :FOX_DH] = k_ref[0]
    kx_scr[:, FOX_DH:] = ka_ref[0]
    qxt_scr[:, FOX_DH:, :] = jnp.zeros(
        (qxt_scr.shape[0], qxt_scr.shape[1] - FOX_DH, FOX_CHUNK), BF16)
    for n in range(nkb):
        for kk in range(nks):
            k0 = n * tk + kk * FOX_KSUB
            vt_scr[n, kk, :FOX_DH, :] = vt_ref[0, :, k0:k0 + FOX_KSUB]
            vt_scr[n, kk, FOX_DH:, :] = jnp.ones((vt_scr.shape[2] - FOX_DH, FOX_KSUB), BF16)

    def tile(i, carry):
        _fox_tile(i, h, qt_ref, ft_ref, o_ref, kx_scr, vt_scr, qxt_scr, sa_scr, sb_scr,
                  mba_scr, mbb_scr, pa_scr, pb_scr, m_scr, al_scr, acc_scr, ex_scr, tq=tq, tk=tk)
        return carry

    lax.fori_loop(0, qt_ref.shape[1], tile, 0)


def _fox_tile(i, h, qt_ref, ft_ref, o_ref, kx_scr, vt_scr, qxt_scr, sa_scr, sb_scr,
              mba_scr, mbb_scr, pa_scr, pb_scr, m_scr, al_scr, acc_scr, ex_scr, *, tq, tk):
    nks = vt_scr.shape[1]
    f_row = ft_ref[0, i, pl.ds(pl.multiple_of(AUG_STRIDE * h, AUG_STRIDE), SUBLANES), :][0:1, :]
    f0, f1, f2 = [p.astype(F32) for p in _split3(f_row)]
    sub = lax.broadcasted_iota(jnp.int32, (AUG_STRIDE, tq), 0)
    aug = jnp.where(sub == 0, f0, jnp.where(sub == 1, f1, jnp.where(
        sub == 2, f2, jnp.where(sub < 6, 1.0, 0.0))))
    aug = aug.astype(BF16)
    aug_rows = pl.ds(pl.multiple_of(FOX_DH + AUG_STRIDE * h, AUG_STRIDE), AUG_STRIDE)

    chunks = [slice(c0, c0 + FOX_CHUNK) for c0 in range(0, tq, FOX_CHUNK)]
    ci = lambda cs: cs.start // FOX_CHUNK
    for cs in chunks:
        qxt_scr[ci(cs), :FOX_DH, :] = qt_ref[0, i, :, cs]
        qxt_scr[ci(cs), aug_rows, :] = aug[:, cs]

    def qk(blk, cs, nkeys=tk):
        off = pl.multiple_of(blk * tk, tk)
        return jnp.dot(kx_scr[pl.ds(off, nkeys), :], qxt_scr[ci(cs)],
                       preferred_element_type=F32)

    def causal(s, cs, key0):
        key = lax.broadcasted_iota(jnp.int32, s.shape, 0) + key0
        qry = lax.broadcasted_iota(jnp.int32, s.shape, 1) + cs.start
        return jnp.where(key <= qry, s, NEG_BIG)

    def emit():
        for cs in chunks:
            acc = acc_scr[ci(cs)]
            o_ref[0, i, :, cs] = (acc[:FOX_DH] / acc[FOX_DH:FOX_DH + 1]).astype(BF16)

    def pv_dot(blk, p, cs):
        return sum(jnp.dot(vt_scr[blk, kk], p[ci(cs), kk * FOX_KSUB:(kk + 1) * FOX_KSUB, :],
                           preferred_element_type=F32) for kk in range(nks))

    def pv_acc(blk_prev, p_prv, cs):
        acc_scr[ci(cs)] = al_scr[ci(cs)] * (acc_scr[ci(cs)] + pv_dot(blk_prev, p_prv, cs))

    def lazy_chunk(blk, cs, p_cur, key0):
        s = qk(blk, cs)
        if key0 is not None:
            s = causal(s, cs, key0)
        mb = jnp.max(s, axis=0, keepdims=True)
        r_old = m_scr[ci(cs)]
        p_cur[ci(cs)] = jnp.exp2((s - r_old).astype(BF16))
        r_new = jnp.maximum(r_old, mb)
        m_scr[ci(cs)] = r_new
        ex_scr[ci(cs)] = jnp.maximum(ex_scr[ci(cs)], mb - r_old)
        return jnp.exp2(r_old - r_new)

    def lazy_step(blk, blk_prev, p_cur, p_prv):
        for cs in chunks:
            alpha = lazy_chunk(blk, cs, p_cur, None)
            pv_acc(blk_prev, p_prv, cs)
            al_scr[ci(cs)] = alpha

    nb = tq // tk
    first = nb * i
    ones = jnp.ones((1, FOX_CHUNK), F32)

    def own_block(cs):
        return cs.start // tk, (cs.start // tk) * tk

    for cs in chunks:
        bl, key0 = own_block(cs)
        nkeys = min(tk, cs.start + FOX_CHUNK - key0)
        s = causal(qk(first + bl, cs, nkeys), cs, key0)
        mb = jnp.max(s, axis=0, keepdims=True)
        m_scr[ci(cs)] = mb
        pa_scr[ci(cs), :nkeys, :] = jnp.exp2((s - mb).astype(BF16))
        if nkeys < tk:
            pa_scr[ci(cs), nkeys:, :] = jnp.zeros((tk - nkeys, FOX_CHUNK), BF16)
        al_scr[ci(cs)] = ones
        acc_scr[ci(cs)] = jnp.zeros((acc_scr.shape[1], FOX_CHUNK), F32)
        ex_scr[ci(cs)] = jnp.zeros((1, FOX_CHUNK), F32)
    for e in range(1, nb):
        p_cur, p_prv = (pb_scr, pa_scr) if e % 2 == 1 else (pa_scr, pb_scr)
        for cs in chunks:
            bl, _ = own_block(cs)
            prev = first + max(bl - (e - 1), 0)
            if bl >= e:
                alpha = lazy_chunk(first + bl - e, cs, p_cur, None)
                pv_acc(prev, p_prv, cs)
                al_scr[ci(cs)] = alpha
            else:
                if bl == e - 1:
                    pv_acc(prev, p_prv, cs)
                    al_scr[ci(cs)] = ones
                if p_cur is pb_scr and e in (bl + 1, bl + 2):
                    pb_scr[ci(cs)] = jnp.zeros((tk, FOX_CHUNK), BF16)
    assert nb % 2 == 0

    def lazy_pair(jj):
        lazy_step(2 * jj + 1, 2 * jj + 2, pa_scr, pb_scr)
        lazy_step(2 * jj, 2 * jj + 1, pb_scr, pa_scr)

    npairs = first // 2
    pairs_per_trip = nb // 2

    def lazy_body(t, carry):
        for u in range(pairs_per_trip):
            lazy_pair(npairs - 1 - pairs_per_trip * t - u)
        return carry

    lax.fori_loop(0, i, lazy_body, 0)
    for cs in chunks:
        pv_acc(0, pb_scr, cs)
    emit()

    def scores(blk, s_out, mb_out, cs):
        s = qk(blk, cs)
        s_out[ci(cs)] = s
        if mb_out is not None:
            mb_out[ci(cs)] = jnp.max(s, axis=0, keepdims=True)

    def softmax(s_in, mb_in, p_out, key0, cs):
        s = s_in[ci(cs)]
        if key0 is None:
            mb = mb_in[ci(cs)]
        else:
            s = causal(s, cs, key0)
            mb = jnp.max(s, axis=0, keepdims=True)
        m_prev = m_scr[ci(cs)]
        m_new = jnp.maximum(m_prev, mb)
        p_out[ci(cs)] = jnp.exp2((s - m_new).astype(BF16))
        m_scr[ci(cs)] = m_new
        return jnp.exp2(m_prev - m_new)

    def pv(blk_prev, p_in, cs):
        acc_scr[ci(cs)] = al_scr[ci(cs)] * acc_scr[ci(cs)] + pv_dot(blk_prev, p_in, cs)

    def step(n, cur, nxt, p_prv, key0):
        s_cur, mb_cur, p_cur = cur
        for cs in chunks:
            if nxt is not None:
                scores(n + 1, nxt[0], nxt[1], cs)
            alpha = softmax(s_cur, mb_cur, p_cur, key0, cs)
            pv(jnp.maximum(n - 1, 0), p_prv, cs)
            al_scr[ci(cs)] = alpha

    buf_a = (sa_scr, mba_scr, pa_scr)
    buf_b = (sb_scr, mbb_scr, pb_scr)

    def body(jj, carry):
        step(2 * jj, buf_a, buf_b, pb_scr, None)
        step(2 * jj + 1, buf_b, buf_a, pa_scr, None)
        return carry

    @pl.when(jnp.max(ex_scr[...]) > FOX_MAX_EXCESS)
    def _():
        m_scr[...] = jnp.full(m_scr.shape, NEG_BIG, F32)
        al_scr[...] = jnp.ones_like(al_scr)
        acc_scr[...] = jnp.zeros_like(acc_scr)
        pb_scr[...] = jnp.zeros_like(pb_scr)
        for cs in chunks:
            scores(0, sa_scr, mba_scr, cs)
        lax.fori_loop(0, npairs, body, 0)
        for d in range(nb):
            cur, oth = (buf_a, buf_b) if d % 2 == 0 else (buf_b, buf_a)
            nxt = (oth[0], None) if d + 1 < nb else None
            step(first + d, cur, nxt, oth[2], d * tk)
        for cs in chunks:
            pv(first + nb - 1, pb_scr, cs)
        emit()


def _fox(z3, qt, ft, ka, vt):
    B, S, _ = z3.shape
    tq, tk = FOX_TQ, FOX_TK
    assert tq % (2 * tk) == 0 and tk % FOX_KSUB == 0 and tk % FOX_CHUNK == 0 and S % tq == 0
    nq = S // tq
    ones_rows = 16
    nch = tq // FOX_CHUNK
    W = FOX_HEADS * FOX_DH
    kb = Z_COLS["fk"] // FOX_DH
    kern = functools.partial(_fox_kernel, tq=tq, tk=tk)
    return pl.pallas_call(
        kern,
        out_shape=jax.ShapeDtypeStruct((B, nq, W, tq), BF16),
        grid=(B, FOX_HEADS),
        in_specs=[pl.BlockSpec((1, nq, FOX_DH, tq), lambda b, h: (b, 0, h, 0)),
                  pl.BlockSpec((1, nq, LANES, tq), lambda b, h: (b, 0, 0, 0)),
                  pl.BlockSpec((1, S, FOX_DH), lambda b, h: (b, 0, kb + h)),
                  pl.BlockSpec((1, S, LANES), lambda b, h: (b, 0, 0)),
                  pl.BlockSpec((1, FOX_DH, S), lambda b, h: (b, h, 0))],
        out_specs=pl.BlockSpec((1, nq, FOX_DH, tq), lambda b, h: (b, 0, h, 0)),
        scratch_shapes=[pltpu.VMEM((S, 2 * FOX_DH), BF16),
                        pltpu.VMEM((S // tk, tk // FOX_KSUB, FOX_DH + ones_rows, FOX_KSUB), BF16),
                        pltpu.VMEM((nch, 2 * FOX_DH, FOX_CHUNK), BF16),
                        pltpu.VMEM((nch, tk, FOX_CHUNK), F32),
                        pltpu.VMEM((nch, tk, FOX_CHUNK), F32),
                        pltpu.VMEM((nch, 1, FOX_CHUNK), F32),
                        pltpu.VMEM((nch, 1, FOX_CHUNK), F32),
                        pltpu.VMEM((nch, tk, FOX_CHUNK), BF16),
                        pltpu.VMEM((nch, tk, FOX_CHUNK), BF16),
                        pltpu.VMEM((nch, 1, FOX_CHUNK), F32),
                        pltpu.VMEM((nch, 1, FOX_CHUNK), F32),
                        pltpu.VMEM((nch, FOX_DH + ones_rows, FOX_CHUNK), F32),
                        pltpu.VMEM((nch, 1, FOX_CHUNK), F32)],
        compiler_params=_cparams(("arbitrary", "arbitrary"), 56),
        name="fox",
    )(qt, ft, z3, ka, vt)


FFN_CHUNK = 256


def _tail_kernel(a_ref, yft_ref, gf_ref, x_ref, g1_ref, sc_ref, sh_ref, g2_ref, n2_ref, nf_ref,
                 wf_ref, wo_ref, wg_ref, wu_ref, wd_ref, o_ref, act_scr):
    tm = x_ref.shape[0]
    halves = [slice(r0, r0 + tm // 2) for r0 in range(0, tm, tm // 2)]
    x1s, hs = [], []
    for rs in halves:
        fo = lax.dot_general(yft_ref[0, 0, :, rs], wf_ref[...], _TN, preferred_element_type=F32)
        merged = a_ref[rs, :].astype(F32) + gf_ref[rs, :].astype(F32) * fo
        out = jnp.dot(merged.astype(BF16), wo_ref[...], preferred_element_type=F32)
        x1s.append(x_ref[rs, :] + g1_ref[0] * out)
    for x1 in x1s:
        ms = jnp.mean(x1 * x1, axis=-1, keepdims=True)
        y = x1 * lax.rsqrt(ms + EPS) * n2_ref[...]
        hs.append((y * (1.0 + sc_ref[0]) + sh_ref[0]).astype(BF16))
    for c0 in range(0, wg_ref.shape[1], FFN_CHUNK):
        for rs, h in zip(halves, hs):
            g = jnp.dot(h, wg_ref[:, c0:c0 + FFN_CHUNK], preferred_element_type=F32)
            u = jnp.dot(h, wu_ref[:, c0:c0 + FFN_CHUNK], preferred_element_type=F32)
            act_scr[rs, c0:c0 + FFN_CHUNK] = (g * _sigmoid(g) * u).astype(BF16)
    for rs, x1 in zip(halves, x1s):
        ffn = jnp.dot(act_scr[rs, :], wd_ref[...], preferred_element_type=F32)
        x2 = x1 + g2_ref[0] * ffn
        ms2 = jnp.mean(x2 * x2, axis=-1, keepdims=True)
        o_ref[rs, :] = x2 * lax.rsqrt(ms2 + EPS) * nf_ref[...]


def _tail(a2, yft, z2, x2, g1, sc2, sh2, g2, n2, nf, wf, wo, wg, wu, wd, S):
    T, D = x2.shape
    FF = wg.shape[1]
    assert FF % FFN_CHUNK == 0
    tm = 512
    spb = S // tm
    tpq = FOX_TQ // tm
    row = lambda i: (i, 0)
    bidx = lambda i: (i // spb, 0, 0)
    const = lambda shape: pl.BlockSpec(shape, lambda i: (0,) * len(shape),
                                       pipeline_mode=pl.Buffered(1))
    return pl.pallas_call(
        _tail_kernel,
        out_shape=jax.ShapeDtypeStruct((T, D), F32),
        grid=(T // tm,),
        in_specs=[pl.BlockSpec((tm, D), row),
                  pl.BlockSpec((1, 1, D, tm),
                               lambda i: (i // spb, (i % spb) // tpq, 0, i % tpq)),
                  pl.BlockSpec((tm, D), lambda i: (i, Z_COLS["gf"] // D)),
                  pl.BlockSpec((tm, D), row),
                  pl.BlockSpec((1, 1, D), bidx),
                  pl.BlockSpec((1, 1, D), bidx),
                  pl.BlockSpec((1, 1, D), bidx),
                  pl.BlockSpec((1, 1, D), bidx),
                  const((1, D)),
                  const((1, D)),
                  const((D, D)),
                  const((D, D)),
                  const((D, FF)),
                  const((D, FF)),
                  const((FF, D))],
        out_specs=pl.BlockSpec((tm, D), row),
        scratch_shapes=[pltpu.VMEM((tm, FF), BF16)],
        compiler_params=_cparams(("arbitrary",), 58),
        name="tail",
    )(a2, yft, z2, x2, g1, sc2, sh2, g2, n2, nf, wf, wo, wg, wu, wd)


def _castw_kernel(wt_ref, o_ref):
    o_ref[...] = wt_ref[...].T.astype(BF16)


def _cast_inproj_weights(w_t, n_lead, gap):
    NT, D = w_t.shape
    tn = 1024
    n_out = NT - gap
    sub = SUBLANES
    assert n_lead % tn == 0 and n_out % tn == 0 and gap % sub == 0
    row0 = lambda j: sub * jnp.where(j < n_lead // tn, j * (tn // sub), j * (tn // sub) + gap // sub)
    return pl.pallas_call(
        _castw_kernel,
        out_shape=jax.ShapeDtypeStruct((D, n_out), BF16),
        grid=(n_out // tn,),
        in_specs=[pl.BlockSpec((pl.Element(tn), pl.Element(D)), lambda j: (row0(j), 0))],
        out_specs=pl.BlockSpec((D, tn), lambda j: (0, j)),
        compiler_params=_cparams(("arbitrary",), 32),
        name="castw",
    )(w_t)


def _rope_tables(S):
    half = RET_DK // 2
    freqs = ROPE_THETA ** (-np.arange(half, dtype=np.float64) / half)
    ang = np.arange(S, dtype=np.float64)[:, None] * freqs[None, :]
    cos, sin = np.cos(ang), np.sin(ang)
    return (jnp.asarray(np.concatenate([cos, cos], axis=1), F32),
            jnp.asarray(np.concatenate([-sin, sin], axis=1), F32))


def _aug_lanes(v8):
    rep = jnp.repeat(v8[..., None], 6, axis=-1)
    pad = jnp.zeros(v8.shape + (AUG_STRIDE - 6,), v8.dtype)
    return jnp.concatenate([rep, pad], axis=-1).reshape(v8.shape[:-1] + (LANES,))


def _layer(x, mod, norm1_w, w_in, b_f, ret_proj, fox_proj, w_out, norm2_w,
           w_gate, w_up, w_down, norm_out_w, tables):
    B, S, D = x.shape
    T = B * S
    sh1, sc1, g1, sh2, sc2, g2 = [m.reshape(B, 1, D) for m in jnp.split(mod, 6, axis=-1)]
    offs = np.cumsum([0, RET_HEADS * RET_DK, RET_HEADS * RET_DK, RET_HEADS * RET_DV,
                      RET_HEADS * RET_DV, FOX_HEADS * FOX_DH, FOX_HEADS * FOX_DH,
                      FOX_HEADS * FOX_DH, FOX_HEADS, D, D])
    o_ff, o_gr = int(offs[7]), int(offs[8])
    w_main = _cast_inproj_weights(jnp.swapaxes(w_in, 0, 1), o_ff, o_gr - o_ff)
    w_ff = _aug_lanes(w_in[:, o_ff:o_gr]).astype(BF16)
    bf_l = _aug_lanes(b_f).reshape(1, LANES)

    x2 = x.reshape(T, D)
    z, qt, vt, ft, ka = _inproj(x2, norm1_w.reshape(1, D), sc1, sh1, tables[0], tables[1],
                            w_main, w_ff, bf_l, B, S)
    z3 = z.reshape(B, S, z.shape[1])
    a = _retention(z3, ret_proj.astype(BF16), D)
    yf = _fox(z3, qt, ft, ka, vt)
    out = _tail(a.reshape(T, D), yf, z, x2, g1, sc2, sh2, g2,
                norm2_w.reshape(1, D), norm_out_w.reshape(1, D),
                fox_proj.astype(BF16), w_out.astype(BF16), w_gate.astype(BF16),
                w_up.astype(BF16), w_down.astype(BF16), S)
    return out.reshape(B, S, D)


def kernel(x, c, ada_w, ada_b, norm1_w, w_in, b_f, ret_proj, fox_proj, w_out,
           norm2_w, w_gate, w_up, w_down, norm_f_w):
    depth = ada_w.shape[0]
    assert depth == 1, "the final RMSNorm is fused into the last layer's channel mixer"
    tables = _rope_tables(x.shape[1])
    l = 0
    mod = _ada(c, ada_w[l], ada_b[l])
    return _layer(x, mod, norm1_w[l], w_in[l], b_f[l], ret_proj[l], fox_proj[l], w_out[l],
                  norm2_w[l], w_gate[l], w_up[l], w_down[l], norm_f_w, tables)
```

```python
import functools
import math

import numpy as np
import jax
import jax.numpy as jnp
from jax import lax
from jax.experimental import pallas as pl
from jax.experimental.pallas import tpu as pltpu

F32 = jnp.float32
BF16 = jnp.bfloat16

EPS = 1e-6
ROPE_THETA = 10000.0
RET_HEADS = 4
RET_DK = 128
RET_DV = 256
FOX_HEADS = 8
FOX_DH = 128
RET_CHUNK = 64
LANES = 128
SUBLANES = 8
AUG_STRIDE = 16

NEG_BIG = -1e30
LOG2E = math.log2(math.e)

_NT = (((1,), (1,)), ((), ()))
_TN = (((0,), (0,)), ((), ()))


def _sigmoid(v):
    return 0.5 + 0.5 * jnp.tanh(0.5 * v)


def _cparams(sem, vmem_mb, flags=None):
    return pltpu.CompilerParams(dimension_semantics=sem,
                                vmem_limit_bytes=vmem_mb * 1024 * 1024, flags=flags)


def _ada_kernel(ct_ref, w_ref, b_ref, o_ref):
    ct = ct_ref[...]
    act = ct * jax.nn.sigmoid(ct)
    w = w_ref[...]
    for b in range(o_ref.shape[0]):
        o_ref[b:b + 1, :] = (jnp.sum(act[:, b:b + 1] * w, axis=0, keepdims=True)
                             + b_ref[...])


def _ada(c, w, b):
    B, D = c.shape
    N = w.shape[1]
    tn = 1536
    return pl.pallas_call(
        _ada_kernel,
        out_shape=jax.ShapeDtypeStruct((B, N), F32),
        grid=(N // tn,),
        in_specs=[pl.BlockSpec((D, B), lambda j: (0, 0)),
                  pl.BlockSpec((D, tn), lambda j: (0, j)),
                  pl.BlockSpec((1, tn), lambda j: (0, j))],
        out_specs=pl.BlockSpec((B, tn), lambda j: (0, j)),
        compiler_params=_cparams(("arbitrary",), 40),
        name="ada",
    )(c.T, w, b.reshape(1, N))


INPROJ_CHUNK = 512
FOX_TQ = 2048
FOX_TK = 512
Z_COLS = {"rq": 0, "rk": 512, "rv": 1024, "rg": 2048, "fk": 3072, "gr": 4096, "gf": 5120,
          "end": 6144}
W_COLS = {"rq": 0, "rk": 512, "rv": 1024, "rg": 2048, "fq": 3072, "fk": 4096, "fv": 5120,
          "gr": 6144, "gf": 7168, "end": 8192}


def _split3(v):
    p0 = v.astype(BF16)
    r1 = v - p0.astype(F32)
    p1 = r1.astype(BF16)
    r2 = r1 - p1.astype(F32)
    p2 = r2.astype(BF16)
    return p0, p1, p2


def _inproj_kernel(x0_ref, xn_ref, nw_ref, sc0_ref, sh0_ref, scn_ref, shn_ref, cos_ref, sin_ref,
                   w_ref, wff_ref, bf_ref, tri_ref,
                   z_ref, qt_ref, vt_ref, ft_ref, ka_ref, h_scr, carry,
                   *, fox_scale, rk_scale, spb):
    i = pl.program_id(0)

    def normed(x_ref, sc_ref, sh_ref):
        x = x_ref[...]
        ms = jnp.mean(x * x, axis=-1, keepdims=True)
        y = x * lax.rsqrt(ms + EPS) * nw_ref[...]
        return (y * (1.0 + sc_ref[0]) + sh_ref[0]).astype(BF16)

    @pl.when(i == 0)
    def _():
        h_scr[0] = normed(x0_ref, sc0_ref, sh0_ref)

    h = h_scr[i % 2]
    cos = cos_ref[...]
    sin = sin_ref[...]
    cw = INPROJ_CHUNK

    def chunk(w_ref, wc0):
        return jnp.dot(h, w_ref[:, wc0:wc0 + cw], preferred_element_type=F32)

    def rotary(name, scale):
        acc = chunk(w_ref, W_COLS[name])
        c0 = Z_COLS[name]
        for hh in range(cw // RET_DK):
            a = acc[:, hh * RET_DK:(hh + 1) * RET_DK]
            o = a * cos + pltpu.roll(a, RET_DK // 2, axis=1) * sin
            if scale is not None:
                o = o * scale
            z_ref[:, c0 + hh * RET_DK:c0 + (hh + 1) * RET_DK] = o.astype(BF16)

    def group(w_ref, wc0, name, width, fn):
        for d in range(0, width, cw):
            c0 = Z_COLS[name] + d
            z_ref[:, c0:c0 + cw] = fn(chunk(w_ref, wc0 + d)).astype(BF16)

    h_next = normed(xn_ref, scn_ref, shn_ref)

    xv = jnp.dot(h, wff_ref[...], preferred_element_type=F32) + bf_ref[...]
    lf = jnp.minimum(xv, 0.0) - jnp.log(1.0 + jnp.exp(-jnp.abs(xv)))
    p0, p1, p2 = _split3(lf)

    rotary("rq", None)
    rotary("rk", rk_scale)

    tri = tri_ref[...]
    cs = (jnp.dot(tri, p0, preferred_element_type=F32)
          + jnp.dot(tri, p1, preferred_element_type=F32)
          + jnp.dot(tri, p2, preferred_element_type=F32))
    first = (i % spb) == 0
    fc = cs + jnp.where(first, 0.0, carry[...])
    tm = fc.shape[0]
    carry[...] = fc[tm - 1:tm, :]

    group(w_ref, W_COLS["rv"], "rv", RET_HEADS * RET_DV, lambda a: a)

    f2l = fc * LOG2E
    ft_ref[0, 0] = f2l.T
    f0, f1, f2 = [p.astype(F32) for p in _split3(f2l)]
    c = lax.broadcasted_iota(jnp.int32, fc.shape, 1) % AUG_STRIDE
    ka_ref[0] = jnp.where(c < 3, 1.0, jnp.where(c == 3, -f0, jnp.where(
        c == 4, -f1, jnp.where(c == 5, -f2, 0.0)))).astype(BF16)

    group(w_ref, W_COLS["fk"], "fk", FOX_HEADS * FOX_DH, lambda a: a)
    group(w_ref, W_COLS["rg"], "rg", RET_HEADS * RET_DV, lambda a: a * _sigmoid(a))
    for d in range(0, FOX_HEADS * FOX_DH, cw):
        vt_ref[0, d:d + cw, :] = chunk(w_ref, W_COLS["fv"] + d).T.astype(BF16)
    group(w_ref, W_COLS["gr"], "gr", W_COLS["end"] - W_COLS["gr"], _sigmoid)
    for d in range(0, FOX_HEADS * FOX_DH, cw):
        qt_ref[0, 0, d:d + cw, :] = (chunk(w_ref, W_COLS["fq"] + d) * fox_scale).T.astype(BF16)
    h_scr[(i + 1) % 2] = h_next


def _inproj(x2, nw, sc, sh, cos_t, sin_t, w, wff, bf_l, B, S):
    T, D = x2.shape
    N = Z_COLS["end"]
    NQ = NV = FOX_HEADS * FOX_DH
    assert w.shape[1] == W_COLS["end"] == N + NQ + NV
    tm = 512
    spb = S // tm
    tpq = FOX_TQ // tm
    assert spb % tpq == 0
    tri = jnp.asarray(np.tril(np.ones((tm, tm), np.float32)), BF16)
    kern = functools.partial(_inproj_kernel, fox_scale=LOG2E / math.sqrt(FOX_DH),
                             rk_scale=RET_DK ** -0.5, spb=spb)
    const = lambda shape: pl.BlockSpec(shape, lambda i: (0,) * len(shape),
                                       pipeline_mode=pl.Buffered(1))
    nxt = lambda i: jnp.minimum(i + 1, T // tm - 1)
    return pl.pallas_call(
        kern,
        out_shape=(jax.ShapeDtypeStruct((T, N), BF16),
                   jax.ShapeDtypeStruct((B, S // FOX_TQ, NQ, FOX_TQ), BF16),
                   jax.ShapeDtypeStruct((B, NV, S), BF16),
                   jax.ShapeDtypeStruct((B, S // FOX_TQ, LANES, FOX_TQ), F32),
                   jax.ShapeDtypeStruct((B, S, LANES), BF16)),
        grid=(T // tm,),
        in_specs=[const((tm, D)),
                  pl.BlockSpec((tm, D), lambda i: (nxt(i), 0)),
                  const((1, D)),
                  const((1, 1, D)),
                  const((1, 1, D)),
                  pl.BlockSpec((1, 1, D), lambda i: (nxt(i) // spb, 0, 0)),
                  pl.BlockSpec((1, 1, D), lambda i: (nxt(i) // spb, 0, 0)),
                  pl.BlockSpec((tm, LANES), lambda i: (i % spb, 0)),
                  pl.BlockSpec((tm, LANES), lambda i: (i % spb, 0)),
                  const((D, W_COLS["end"])),
                  const((D, LANES)),
                  const((1, LANES)),
                  const((tm, tm))],
        out_specs=(pl.BlockSpec((tm, N), lambda i: (i, 0)),
                   pl.BlockSpec((1, 1, NQ, tm), lambda i: (i // spb, (i % spb) // tpq, 0, i % tpq)),
                   pl.BlockSpec((1, NV, tm), lambda i: (i // spb, 0, i % spb)),
                   pl.BlockSpec((1, 1, LANES, tm),
                                lambda i: (i // spb, (i % spb) // tpq, 0, i % tpq)),
                   pl.BlockSpec((1, tm, LANES), lambda i: (i // spb, i % spb, 0))),
        scratch_shapes=[pltpu.VMEM((2, tm, D), BF16), pltpu.VMEM((1, LANES), F32)],
        compiler_params=_cparams(("arbitrary",), 56),
        name="inproj",
    )(x2, x2, nw, sc, sh, sc, sh, cos_t, sin_t, w, wff, bf_l, tri)


def _ret_consts(L):
    hs = np.arange(RET_HEADS, dtype=np.float64)
    log_gamma = np.log(1.0 - np.exp2(-5.0 - hs))
    idx = np.arange(L, dtype=np.float64)
    dist = np.abs(idx[:, None] - idx[None, :])
    chunk_ok = (idx[None, :] // RET_CHUNK) <= (idx[:, None] // RET_CHUNK)
    dmask = np.exp(log_gamma[:, None, None] * dist[None]) * chunk_ok[None]
    qdec = np.exp(log_gamma[:, None] * idx[None, :])[..., None]
    kdec = np.exp(log_gamma[:, None] * (L - idx)[None, :])[..., None]
    bdec = [float(v) for v in np.exp(log_gamma * L)]
    return (jnp.asarray(dmask, F32), jnp.asarray(qdec, F32), jnp.asarray(kdec, F32), bdec)


def _ret_kernel(rq_ref, rk_ref, rv_ref, rg_ref, gr_ref, dm_ref, qd_ref, kd_ref, wr_ref,
                o_ref, st_scr, y_scr, *, bdec):
    nbatch, L = rq_ref.shape[0], rq_ref.shape[1]

    @pl.when(pl.program_id(0) == 0)
    def _():
        st_scr[...] = jnp.zeros_like(st_scr)

    for h in range(RET_HEADS):
        for b in range(nbatch):
            q = rq_ref[b, :, h * RET_DK:(h + 1) * RET_DK]
            k = rk_ref[b, :, h * RET_DK:(h + 1) * RET_DK]
            v = rv_ref[b, :, h * RET_DV:(h + 1) * RET_DV]
            s = lax.dot_general(q, k, _NT, preferred_element_type=F32) * dm_ref[h]
            intra = jnp.dot(s.astype(BF16), v, preferred_element_type=F32)
            st = st_scr[b, h]
            inter = jnp.dot(q, st.astype(BF16), preferred_element_type=F32) * qd_ref[h]
            ks = (k.astype(F32) * kd_ref[h]).astype(BF16)
            st_scr[b, h] = bdec[h] * st + lax.dot_general(ks, v, _TN,
                                                          preferred_element_type=F32)
            ro = intra + inter
            ron = ro * lax.rsqrt(jnp.mean(ro * ro, axis=-1, keepdims=True) + EPS)
            g = rg_ref[b, :, h * RET_DV:(h + 1) * RET_DV].astype(F32)
            y_scr[b * L:(b + 1) * L, h * RET_DV:(h + 1) * RET_DV] = (g * ron).astype(BF16)

    proj = jnp.dot(y_scr[...], wr_ref[...], preferred_element_type=F32)
    for b in range(nbatch):
        o_ref[b] = (gr_ref[b].astype(F32) * proj[b * L:(b + 1) * L]).astype(BF16)


def _retention(z3, wr, D):
    B, S, _ = z3.shape
    L = 512
    dmask, qdec, kdec, bdec = _ret_consts(L)
    QK = RET_HEADS * RET_DK
    V = RET_HEADS * RET_DV
    kern = functools.partial(_ret_kernel, bdec=bdec)
    full3 = lambda i: (0, 0, 0)
    return pl.pallas_call(
        kern,
        out_shape=jax.ShapeDtypeStruct((B, S, D), BF16),
        grid=(S // L,),
        in_specs=[pl.BlockSpec((B, L, QK), lambda i: (0, i, 0)),
                  pl.BlockSpec((B, L, QK), lambda i: (0, i, 1)),
                  pl.BlockSpec((B, L, V), lambda i: (0, i, 1)),
                  pl.BlockSpec((B, L, V), lambda i: (0, i, 2)),
                  pl.BlockSpec((B, L, D), lambda i: (0, i, Z_COLS["gr"] // D)),
                  pl.BlockSpec((RET_HEADS, L, L), full3),
                  pl.BlockSpec((RET_HEADS, L, 1), full3),
                  pl.BlockSpec((RET_HEADS, L, 1), full3),
                  pl.BlockSpec((V, D), lambda i: (0, 0))],
        out_specs=pl.BlockSpec((B, L, D), lambda i: (0, i, 0)),
        scratch_shapes=[pltpu.VMEM((B, RET_HEADS, RET_DK, RET_DV), F32),
                        pltpu.VMEM((B * L, V), BF16)],
        compiler_params=_cparams(("arbitrary",), 48),
        name="ret",
    )(z3, z3, z3, z3, z3, dmask, qdec, kdec, wr)


FOX_CHUNK = 256
FOX_KSUB = 256
FOX_MAX_EXCESS = 64.0


def _fox_kernel(qt_ref, ft_ref, k_ref, ka_ref, vt_ref, o_ref,
                kx_scr, vt_scr, qxt_scr, sa_scr, sb_scr, mba_scr, mbb_scr, pa_scr, pb_scr,
                m_scr, al_scr, acc_scr, ex_scr, *, tq, tk):
    h = pl.program_id(1)
    nkb, nks = vt_scr.shape[0], vt_scr.shape[1]

    kx_scr[:, :FOX_DH] = k_ref[0]
    kx_scr[:, FOX_DH:] = ka_ref[0]
    qxt_scr[:, FOX_DH:, :] = jnp.zeros(
        (qxt_scr.shape[0], qxt_scr.shape[1] - FOX_DH, FOX_CHUNK), BF16)
    for n in range(nkb):
        for kk in range(nks):
            k0 = n * tk + kk * FOX_KSUB
            vt_scr[n, kk, :FOX_DH, :] = vt_ref[0, :, k0:k0 + FOX_KSUB]
            vt_scr[n, kk, FOX_DH:, :] = jnp.ones((vt_scr.shape[2] - FOX_DH, FOX_KSUB), BF16)

    def tile(i, carry):
        _fox_tile(i, h, qt_ref, ft_ref, o_ref, kx_scr, vt_scr, qxt_scr, sa_scr, sb_scr,
                  mba_scr, mbb_scr, pa_scr, pb_scr, m_scr, al_scr, acc_scr, ex_scr, tq=tq, tk=tk)
        return carry

    lax.fori_loop(0, qt_ref.shape[1], tile, 0)


def _fox_tile(i, h, qt_ref, ft_ref, o_ref, kx_scr, vt_scr, qxt_scr, sa_scr, sb_scr,
              mba_scr, mbb_scr, pa_scr, pb_scr, m_scr, al_scr, acc_scr, ex_scr, *, tq, tk):
    nks = vt_scr.shape[1]
    f_row = ft_ref[0, i, pl.ds(pl.multiple_of(AUG_STRIDE * h, AUG_STRIDE), SUBLANES), :][0:1, :]
    f0, f1, f2 = [p.astype(F32) for p in _split3(f_row)]
    sub = lax.broadcasted_iota(jnp.int32, (AUG_STRIDE, tq), 0)
    aug = jnp.where(sub == 0, f0, jnp.where(sub == 1, f1, jnp.where(
        sub == 2, f2, jnp.where(sub < 6, 1.0, 0.0))))
    aug = aug.astype(BF16)
    aug_rows = pl.ds(pl.multiple_of(FOX_DH + AUG_STRIDE * h, AUG_STRIDE), AUG_STRIDE)

    chunks = [slice(c0, c0 + FOX_CHUNK) for c0 in range(0, tq, FOX_CHUNK)]
    ci = lambda cs: cs.start // FOX_CHUNK
    for cs in chunks:
        qxt_scr[ci(cs), :FOX_DH, :] = qt_ref[0, i, :, cs]
        qxt_scr[ci(cs), aug_rows, :] = aug[:, cs]

    def qk(blk, cs, nkeys=tk):
        off = pl.multiple_of(blk * tk, tk)
        return jnp.dot(kx_scr[pl.ds(off, nkeys), :], qxt_scr[ci(cs)],
                       preferred_element_type=F32)

    def causal(s, cs, key0):
        key = lax.broadcasted_iota(jnp.int32, s.shape, 0) + key0
        qry = lax.broadcasted_iota(jnp.int32, s.shape, 1) + cs.start
        return jnp.where(key <= qry, s, NEG_BIG)

    def emit():
        for cs in chunks:
            acc = acc_scr[ci(cs)]
            o_ref[0, i, :, cs] = (acc[:FOX_DH] / acc[FOX_DH:FOX_DH + 1]).astype(BF16)

    def pv_dot(blk, p, cs):
        return sum(jnp.dot(vt_scr[blk, kk], p[ci(cs), kk * FOX_KSUB:(kk + 1) * FOX_KSUB, :],
                           preferred_element_type=F32) for kk in range(nks))

    def pv_acc(blk_prev, p_prv, cs):
        acc_scr[ci(cs)] = al_scr[ci(cs)] * (acc_scr[ci(cs)] + pv_dot(blk_prev, p_prv, cs))

    def lazy_chunk(blk, cs, p_cur, key0):
        s = qk(blk, cs)
        if key0 is not None:
            s = causal(s, cs, key0)
        mb = jnp.max(s, axis=0, keepdims=True)
        r_old = m_scr[ci(cs)]
        p_cur[ci(cs)] = jnp.exp2((s - r_old).astype(BF16))
        r_new = jnp.maximum(r_old, mb)
        m_scr[ci(cs)] = r_new
        ex_scr[ci(cs)] = jnp.maximum(ex_scr[ci(cs)], mb - r_old)
        return jnp.exp2(r_old - r_new)

    def lazy_step(blk, blk_prev, p_cur, p_prv):
        for cs in chunks:
            alpha = lazy_chunk(blk, cs, p_cur, None)
            pv_acc(blk_prev, p_prv, cs)
            al_scr[ci(cs)] = alpha

    nb = tq // tk
    first = nb * i
    ones = jnp.ones((1, FOX_CHUNK), F32)

    def own_block(cs):
        return cs.start // tk, (cs.start // tk) * tk

    for cs in chunks:
        bl, key0 = own_block(cs)
        nkeys = min(tk, cs.start + FOX_CHUNK - key0)
        s = causal(qk(first + bl, cs, nkeys), cs, key0)
        mb = jnp.max(s, axis=0, keepdims=True)
        m_scr[ci(cs)] = mb
        pa_scr[ci(cs), :nkeys, :] = jnp.exp2((s - mb).astype(BF16))
        if nkeys < tk:
            pa_scr[ci(cs), nkeys:, :] = jnp.zeros((tk - nkeys, FOX_CHUNK), BF16)
        al_scr[ci(cs)] = ones
        acc_scr[ci(cs)] = jnp.zeros((acc_scr.shape[1], FOX_CHUNK), F32)
        ex_scr[ci(cs)] = jnp.zeros((1, FOX_CHUNK), F32)
    for e in range(1, nb):
        p_cur, p_prv = (pb_scr, pa_scr) if e % 2 == 1 else (pa_scr, pb_scr)
        for cs in chunks:
            bl, _ = own_block(cs)
            prev = first + max(bl - (e - 1), 0)
            if bl >= e:
                alpha = lazy_chunk(first + bl - e, cs, p_cur, None)
                pv_acc(prev, p_prv, cs)
                al_scr[ci(cs)] = alpha
            else:
                if bl == e - 1:
                    pv_acc(prev, p_prv, cs)
                    al_scr[ci(cs)] = ones
                if p_cur is pb_scr and e in (bl + 1, bl + 2):
                    pb_scr[ci(cs)] = jnp.zeros((tk, FOX_CHUNK), BF16)
    assert nb % 2 == 0

    def lazy_pair(jj):
        lazy_step(2 * jj + 1, 2 * jj + 2, pa_scr, pb_scr)
        lazy_step(2 * jj, 2 * jj + 1, pb_scr, pa_scr)

    npairs = first // 2
    pairs_per_trip = nb // 2

    def lazy_body(t, carry):
        for u in range(pairs_per_trip):
            lazy_pair(npairs - 1 - pairs_per_trip * t - u)
        return carry

    lax.fori_loop(0, i, lazy_body, 0)
    for cs in chunks:
        pv_acc(0, pb_scr, cs)
    emit()

    def scores(blk, s_out, mb_out, cs):
        s = qk(blk, cs)
        s_out[ci(cs)] = s
        if mb_out is not None:
            mb_out[ci(cs)] = jnp.max(s, axis=0, keepdims=True)

    def softmax(s_in, mb_in, p_out, key0, cs):
        s = s_in[ci(cs)]
        if key0 is None:
            mb = mb_in[ci(cs)]
        else:
            s = causal(s, cs, key0)
            mb = jnp.max(s, axis=0, keepdims=True)
        m_prev = m_scr[ci(cs)]
        m_new = jnp.maximum(m_prev, mb)
        p_out[ci(cs)] = jnp.exp2((s - m_new).astype(BF16))
        m_scr[ci(cs)] = m_new
        return jnp.exp2(m_prev - m_new)

    def pv(blk_prev, p_in, cs):
        acc_scr[ci(cs)] = al_scr[ci(cs)] * acc_scr[ci(cs)] + pv_dot(blk_prev, p_in, cs)

    def step(n, cur, nxt, p_prv, key0):
        s_cur, mb_cur, p_cur = cur
        for cs in chunks:
            if nxt is not None:
                scores(n + 1, nxt[0], nxt[1], cs)
            alpha = softmax(s_cur, mb_cur, p_cur, key0, cs)
            pv(jnp.maximum(n - 1, 0), p_prv, cs)
            al_scr[ci(cs)] = alpha

    buf_a = (sa_scr, mba_scr, pa_scr)
    buf_b = (sb_scr, mbb_scr, pb_scr)

    def body(jj, carry):
        step(2 * jj, buf_a, buf_b, pb_scr, None)
        step(2 * jj + 1, buf_b, buf_a, pa_scr, None)
        return carry

    @pl.when(jnp.max(ex_scr[...]) > FOX_MAX_EXCESS)
    def _():
        m_scr[...] = jnp.full(m_scr.shape, NEG_BIG, F32)
        al_scr[...] = jnp.ones_like(al_scr)
        acc_scr[...] = jnp.zeros_like(acc_scr)
        pb_scr[...] = jnp.zeros_like(pb_scr)
        for cs in chunks:
            scores(0, sa_scr, mba_scr, cs)
        lax.fori_loop(0, npairs, body, 0)
        for d in range(nb):
            cur, oth = (buf_a, buf_b) if d % 2 == 0 else (buf_b, buf_a)
            nxt = (oth[0], None) if d + 1 < nb else None
            step(first + d, cur, nxt, oth[2], d * tk)
        for cs in chunks:
            pv(first + nb - 1, pb_scr, cs)
        emit()


def _fox(z3, qt, ft, ka, vt):
    B, S, _ = z3.shape
    tq, tk = FOX_TQ, FOX_TK
    assert tq % (2 * tk) == 0 and tk % FOX_KSUB == 0 and tk % FOX_CHUNK == 0 and S % tq == 0
    nq = S // tq
    ones_rows = 16
    nch = tq // FOX_CHUNK
    W = FOX_HEADS * FOX_DH
    kb = Z_COLS["fk"] // FOX_DH
    kern = functools.partial(_fox_kernel, tq=tq, tk=tk)
    return pl.pallas_call(
        kern,
        out_shape=jax.ShapeDtypeStruct((B, nq, W, tq), BF16),
        grid=(B, FOX_HEADS),
        in_specs=[pl.BlockSpec((1, nq, FOX_DH, tq), lambda b, h: (b, 0, h, 0)),
                  pl.BlockSpec((1, nq, LANES, tq), lambda b, h: (b, 0, 0, 0)),
                  pl.BlockSpec((1, S, FOX_DH), lambda b, h: (b, 0, kb + h)),
                  pl.BlockSpec((1, S, LANES), lambda b, h: (b, 0, 0)),
                  pl.BlockSpec((1, FOX_DH, S), lambda b, h: (b, h, 0))],
        out_specs=pl.BlockSpec((1, nq, FOX_DH, tq), lambda b, h: (b, 0, h, 0)),
        scratch_shapes=[pltpu.VMEM((S, 2 * FOX_DH), BF16),
                        pltpu.VMEM((S // tk, tk // FOX_KSUB, FOX_DH + ones_rows, FOX_KSUB), BF16),
                        pltpu.VMEM((nch, 2 * FOX_DH, FOX_CHUNK), BF16),
                        pltpu.VMEM((nch, tk, FOX_CHUNK), F32),
                        pltpu.VMEM((nch, tk, FOX_CHUNK), F32),
                        pltpu.VMEM((nch, 1, FOX_CHUNK), F32),
                        pltpu.VMEM((nch, 1, FOX_CHUNK), F32),
                        pltpu.VMEM((nch, tk, FOX_CHUNK), BF16),
                        pltpu.VMEM((nch, tk, FOX_CHUNK), BF16),
                        pltpu.VMEM((nch, 1, FOX_CHUNK), F32),
                        pltpu.VMEM((nch, 1, FOX_CHUNK), F32),
                        pltpu.VMEM((nch, FOX_DH + ones_rows, FOX_CHUNK), F32),
                        pltpu.VMEM((nch, 1, FOX_CHUNK), F32)],
        compiler_params=_cparams(("arbitrary", "arbitrary"), 56),
        name="fox",
    )(qt, ft, z3, ka, vt)


FFN_CHUNK = 256


def _tail_kernel(a_ref, yft_ref, gf_ref, x_ref, g1_ref, sc_ref, sh_ref, g2_ref, n2_ref, nf_ref,
                 wf_ref, wo_ref, wg_ref, wu_ref, wd_ref, o_ref, act_scr):
    tm = x_ref.shape[0]
    halves = [slice(r0, r0 + tm // 2) for r0 in range(0, tm, tm // 2)]
    x1s, hs = [], []
    for rs in halves:
        fo = lax.dot_general(yft_ref[0, 0, :, rs], wf_ref[...], _TN, preferred_element_type=F32)
        merged = a_ref[rs, :].astype(F32) + gf_ref[rs, :].astype(F32) * fo
        out = jnp.dot(merged.astype(BF16), wo_ref[...], preferred_element_type=F32)
        x1s.append(x_ref[rs, :] + g1_ref[0] * out)
    for x1 in x1s:
        ms = jnp.mean(x1 * x1, axis=-1, keepdims=True)
        y = x1 * lax.rsqrt(ms + EPS) * n2_ref[...]
        hs.append((y * (1.0 + sc_ref[0]) + sh_ref[0]).astype(BF16))
    for c0 in range(0, wg_ref.shape[1], FFN_CHUNK):
        for rs, h in zip(halves, hs):
            g = jnp.dot(h, wg_ref[:, c0:c0 + FFN_CHUNK], preferred_element_type=F32)
            u = jnp.dot(h, wu_ref[:, c0:c0 + FFN_CHUNK], preferred_element_type=F32)
            act_scr[rs, c0:c0 + FFN_CHUNK] = (g * _sigmoid(g) * u).astype(BF16)
    for rs, x1 in zip(halves, x1s):
        ffn = jnp.dot(act_scr[rs, :], wd_ref[...], preferred_element_type=F32)
        x2 = x1 + g2_ref[0] * ffn
        ms2 = jnp.mean(x2 * x2, axis=-1, keepdims=True)
        o_ref[rs, :] = x2 * lax.rsqrt(ms2 + EPS) * nf_ref[...]


def _tail(a2, yft, z2, x2, g1, sc2, sh2, g2, n2, nf, wf, wo, wg, wu, wd, S):
    T, D = x2.shape
    FF = wg.shape[1]
    assert FF % FFN_CHUNK == 0
    tm = 512
    spb = S // tm
    tpq = FOX_TQ // tm
    row = lambda i: (i, 0)
    bidx = lambda i: (i // spb, 0, 0)
    const = lambda shape: pl.BlockSpec(shape, lambda i: (0,) * len(shape),
                                       pipeline_mode=pl.Buffered(1))
    return pl.pallas_call(
        _tail_kernel,
        out_shape=jax.ShapeDtypeStruct((T, D), F32),
        grid=(T // tm,),
        in_specs=[pl.BlockSpec((tm, D), row),
                  pl.BlockSpec((1, 1, D, tm),
                               lambda i: (i // spb, (i % spb) // tpq, 0, i % tpq)),
                  pl.BlockSpec((tm, D), lambda i: (i, Z_COLS["gf"] // D)),
                  pl.BlockSpec((tm, D), row),
                  pl.BlockSpec((1, 1, D), bidx),
                  pl.BlockSpec((1, 1, D), bidx),
                  pl.BlockSpec((1, 1, D), bidx),
                  pl.BlockSpec((1, 1, D), bidx),
                  const((1, D)),
                  const((1, D)),
                  const((D, D)),
                  const((D, D)),
                  const((D, FF)),
                  const((D, FF)),
                  const((FF, D))],
        out_specs=pl.BlockSpec((tm, D), row),
        scratch_shapes=[pltpu.VMEM((tm, FF), BF16)],
        compiler_params=_cparams(("arbitrary",), 58),
        name="tail",
    )(a2, yft, z2, x2, g1, sc2, sh2, g2, n2, nf, wf, wo, wg, wu, wd)


def _castw_kernel(wt_ref, o_ref):
    o_ref[...] = wt_ref[...].T.astype(BF16)


def _cast_inproj_weights(w_t, n_lead, gap):
    NT, D = w_t.shape
    tn = 1024
    n_out = NT - gap
    sub = SUBLANES
    assert n_lead % tn == 0 and n_out % tn == 0 and gap % sub == 0
    row0 = lambda j: sub * jnp.where(j < n_lead // tn, j * (tn // sub), j * (tn // sub) + gap // sub)
    return pl.pallas_call(
        _castw_kernel,
        out_shape=jax.ShapeDtypeStruct((D, n_out), BF16),
        grid=(n_out // tn,),
        in_specs=[pl.BlockSpec((pl.Element(tn), pl.Element(D)), lambda j: (row0(j), 0))],
        out_specs=pl.BlockSpec((D, tn), lambda j: (0, j)),
        compiler_params=_cparams(("arbitrary",), 32),
        name="castw",
    )(w_t)


def _rope_tables(S):
    half = RET_DK // 2
    freqs = ROPE_THETA ** (-np.arange(half, dtype=np.float64) / half)
    ang = np.arange(S, dtype=np.float64)[:, None] * freqs[None, :]
    cos, sin = np.cos(ang), np.sin(ang)
    return (jnp.asarray(np.concatenate([cos, cos], axis=1), F32),
            jnp.asarray(np.concatenate([-sin, sin], axis=1), F32))


def _aug_lanes(v8):
    rep = jnp.repeat(v8[..., None], 6, axis=-1)
    pad = jnp.zeros(v8.shape + (AUG_STRIDE - 6,), v8.dtype)
    return jnp.concatenate([rep, pad], axis=-1).reshape(v8.shape[:-1] + (LANES,))


def _layer(x, mod, norm1_w, w_in, b_f, ret_proj, fox_proj, w_out, norm2_w,
           w_gate, w_up, w_down, norm_out_w, tables):
    B, S, D = x.shape
    T = B * S
    sh1, sc1, g1, sh2, sc2, g2 = [m.reshape(B, 1, D) for m in jnp.split(mod, 6, axis=-1)]
    offs = np.cumsum([0, RET_HEADS * RET_DK, RET_HEADS * RET_DK, RET_HEADS * RET_DV,
                      RET_HEADS * RET_DV, FOX_HEADS * FOX_DH, FOX_HEADS * FOX_DH,
                      FOX_HEADS * FOX_DH, FOX_HEADS, D, D])
    o_ff, o_gr = int(offs[7]), int(offs[8])
    w_main = _cast_inproj_weights(jnp.swapaxes(w_in, 0, 1), o_ff, o_gr - o_ff)
    w_ff = _aug_lanes(w_in[:, o_ff:o_gr]).astype(BF16)
    bf_l = _aug_lanes(b_f).reshape(1, LANES)

    x2 = x.reshape(T, D)
    z, qt, vt, ft, ka = _inproj(x2, norm1_w.reshape(1, D), sc1, sh1, tables[0], tables[1],
                            w_main, w_ff, bf_l, B, S)
    z3 = z.reshape(B, S, z.shape[1])
    a = _retention(z3, ret_proj.astype(BF16), D)
    yf = _fox(z3, qt, ft, ka, vt)
    out = _tail(a.reshape(T, D), yf, z, x2, g1, sc2, sh2, g2,
                norm2_w.reshape(1, D), norm_out_w.reshape(1, D),
                fox_proj.astype(BF16), w_out.astype(BF16), w_gate.astype(BF16),
                w_up.astype(BF16), w_down.astype(BF16), S)
    return out.reshape(B, S, D)


def kernel(x, c, ada_w, ada_b, norm1_w, w_in, b_f, ret_proj, fox_proj, w_out,
           norm2_w, w_gate, w_up, w_down, norm_f_w):
    depth = ada_w.shape[0]
    assert depth == 1, "the final RMSNorm is fused into the last layer's channel mixer"
    tables = _rope_tables(x.shape[1])
    l = 0
    mod = _ada(c, ada_w[l], ada_b[l])
    return _layer(x, mod, norm1_w[l], w_in[l], b_f[l], ret_proj[l], fox_proj[l], w_out[l],
                  norm2_w[l], w_gate[l], w_up[l], w_down[l], norm_f_w, tables)
```
